```python
import math
import jax, jax.numpy as jnp
from jax import lax
import numpy as np

D_MODEL = 2048
BATCH = 2
SEQ = 4096
DEPTH = 1

N_MEM = 256

NSA_HEADS = 16
NSA_GROUPS = 2
NSA_HPG = NSA_HEADS // NSA_GROUPS
NSA_HEAD_DIM = 128
CMP_BLOCK = 32
CMP_STRIDE = 16
CMP_HIDDEN = 256
SLC_BLOCK = 64
SLC_TOPK = 16
WINDOW = 512
Q_BLOCK = 128
SLC_Q_BLOCK = 64
FORCED_SCORE = 1e4

MLA_HEADS = 16
MLA_NOPE = 128
MLA_ROPE = 64
MLA_V = 128
MLA_KV_RANK = 512
ROPE_THETA = 10000.0

REL_BUCKETS = 32
REL_MAX_EXACT = 16
REL_MAX_DIST = 4096

MEM_HEADS = 4
MEM_HEAD_DIM = 128

PEER_HEADS = 8
PEER_N_KEYS = 128
PEER_N_EXPERTS = PEER_N_KEYS * PEER_N_KEYS
PEER_QUERY_DIM = 256
PEER_HALF = PEER_QUERY_DIM // 2
PEER_TOPK = 16
PEER_CHUNK = 128

N_BRANCHES = 2
ALPHA = (2.0 * DEPTH) ** 0.25
BETA = (8.0 * DEPTH) ** -0.25
NEG_INF = -1e30
LN_EPS = 1e-5
RMS_EPS = 1e-6

NSA_Q_COLS = NSA_HEADS * NSA_HEAD_DIM
NSA_KV_COLS = NSA_GROUPS * NSA_HEAD_DIM
NSA_GATE_COLS = 3 * NSA_HEADS
MLA_Q_COLS = MLA_HEADS * (MLA_NOPE + MLA_ROPE)
MLA_KV_COLS = MLA_KV_RANK + MLA_ROPE
OFF_NSA_KV = NSA_Q_COLS
OFF_NSA_GATE = OFF_NSA_KV + 6 * NSA_KV_COLS
OFF_MLA_Q = OFF_NSA_GATE + NSA_GATE_COLS
OFF_MLA_KV = OFF_MLA_Q + MLA_Q_COLS
OFF_MERGE = OFF_MLA_KV + MLA_KV_COLS
COMBINED_COLS = OFF_MERGE + N_BRANCHES * D_MODEL

kernel_name = 'hybrid_nsa_mla_peer_block'


def layer_norm(x, g, b):
    xf = x.astype(jnp.float32)
    mu = jnp.mean(xf, -1, keepdims=True)
    var = jnp.mean(jnp.square(xf - mu), -1, keepdims=True)
    return ((xf - mu) * lax.rsqrt(var + LN_EPS) * g + b).astype(x.dtype)


def rms_norm(x, g):
    xf = x.astype(jnp.float32)
    return (xf * lax.rsqrt(jnp.mean(xf * xf, -1, keepdims=True) + RMS_EPS) * g).astype(x.dtype)


def masked_softmax(scores, mask):
    s = jnp.where(mask, scores.astype(jnp.float32), NEG_INF)
    return jax.nn.softmax(s, axis=-1) * mask


def t5_bucket(dist):
    dist = jnp.maximum(dist, 0)
    n_log = REL_BUCKETS - REL_MAX_EXACT
    large = REL_MAX_EXACT + (jnp.log(jnp.maximum(dist, 1).astype(jnp.float32) / REL_MAX_EXACT)
                             / math.log(REL_MAX_DIST / REL_MAX_EXACT) * n_log).astype(jnp.int32)
    large = jnp.minimum(large, REL_BUCKETS - 1)
    return jnp.where(dist < REL_MAX_EXACT, dist, large)


def rope_tables(T):
    inv = ROPE_THETA ** (-jnp.arange(0, MLA_ROPE, 2, dtype=jnp.float32) / MLA_ROPE)
    ang = jnp.arange(T, dtype=jnp.float32)[:, None] * inv[None, :]
    return jnp.cos(ang), jnp.sin(ang)


def apply_rope(x, cos, sin):
    x1, x2 = jnp.split(x.astype(jnp.float32), 2, axis=-1)
    return jnp.concatenate([x1 * cos - x2 * sin, x1 * sin + x2 * cos], axis=-1).astype(x.dtype)


def _to_blocks(a, axis, block):
    a = jnp.moveaxis(a, axis, 0)
    a = a.reshape((a.shape[0] // block, block) + a.shape[1:])
    return jnp.moveaxis(a, 1, axis + 1)


def _from_blocks(a, axis):
    a = jnp.moveaxis(a, 0, axis)
    return a.reshape(a.shape[:axis] + (a.shape[axis] * a.shape[axis + 1],) + a.shape[axis + 2:])


def nsa_compress(kv, pe, w1, w2):
    B, T = kv.shape[:2]
    n_cmp = (T - CMP_BLOCK) // CMP_STRIDE + 1
    idx = np.arange(n_cmp)[:, None] * CMP_STRIDE + np.arange(CMP_BLOCK)[None, :]
    blocks = kv[:, idx] + pe[None, None, :, None, :]
    flat = jnp.moveaxis(blocks, 3, 2).reshape(B, n_cmp, NSA_GROUPS, CMP_BLOCK * NSA_HEAD_DIM)
    return jax.nn.gelu(flat @ w1) @ w2


def nsa_attention(q, kv, gate_logits, rel_bias, cmp_k_pe, cmp_k_w1, cmp_k_w2, cmp_v_pe, cmp_v_w1, cmp_v_w2):
    B, T = q.shape[:2]
    G, HPG, d = NSA_GROUPS, NSA_HEADS // NSA_GROUPS, NSA_HEAD_DIM
    qg = q.reshape(B, T, G, HPG, d) * (d ** -0.5)
    kv = kv.reshape(B, T, 6, G, d)
    k_cmp, v_cmp, k_slc, v_slc, k_win, v_win = [kv[:, :, i] for i in range(6)]
    gates = jax.nn.sigmoid(gate_logits).reshape(B, T, 3, G, HPG)
    bias_tab = rel_bias.reshape(REL_BUCKETS, G, HPG)
    t_pos = jnp.arange(T)

    kc = nsa_compress(k_cmp, cmp_k_pe, cmp_k_w1, cmp_k_w2)
    vc = nsa_compress(v_cmp, cmp_v_pe, cmp_v_w1, cmp_v_w2)
    n_cmp = kc.shape[1]
    blk_end = jnp.arange(n_cmp) * CMP_STRIDE + CMP_BLOCK - 1
    dist_c = t_pos[:, None] - blk_end[None, :]
    bias_c = bias_tab[t5_bucket(dist_c)].transpose(2, 3, 0, 1)
    s_c = jnp.einsum('btghd,bngd->bghtn', qg, kc) + bias_c
    p_cmp = masked_softmax(s_c, dist_c >= 0)
    o_cmp = jnp.einsum('bghtn,bngd->btghd', p_cmp.astype(vc.dtype), vc)

    n_slc = T // SLC_BLOCK
    slc_start = np.arange(n_slc) * SLC_BLOCK
    cmp_start = np.arange(n_cmp) * CMP_STRIDE
    overlap = ((cmp_start[:, None] < slc_start[None, :] + SLC_BLOCK)
               & (cmp_start[:, None] + CMP_BLOCK > slc_start[None, :])).astype(np.float32)
    imp = jnp.einsum('bghtn,nj->bgtj', p_cmp, jnp.asarray(overlap))
    j = jnp.arange(n_slc)
    cur = t_pos // SLC_BLOCK
    valid = j[None, :] * SLC_BLOCK <= t_pos[:, None]
    forced = (j[None, :] == 0) | (j[None, :] == cur[:, None]) | (j[None, :] == cur[:, None] - 1)
    score = jnp.where(forced, FORCED_SCORE, jnp.where(valid, imp, NEG_INF))
    k_sel = min(SLC_TOPK, n_slc)
    sel_score, sel_idx = lax.top_k(score, k_sel)
    sel_ok = sel_score > 0.5 * NEG_INF
    k_slc_t = jnp.transpose(k_slc, (0, 2, 1, 3))
    v_slc_t = jnp.transpose(v_slc, (0, 2, 1, 3))
    bias_tab_g = jnp.transpose(bias_tab, (1, 0, 2))
    b_i = jnp.arange(B)[:, None, None, None]
    g_i = jnp.arange(G)[None, :, None, None]

    def slc_block(args):
        q_b, idx_b, ok_b, t_b = args
        tok = (idx_b[..., None] * SLC_BLOCK + jnp.arange(SLC_BLOCK)).reshape(idx_b.shape[:3] + (-1,))
        key_ok = jnp.repeat(ok_b, SLC_BLOCK, axis=-1) & (tok <= t_b[:, None])
        kk = k_slc_t[b_i, g_i, tok]
        vv = v_slc_t[b_i, g_i, tok]
        bias = bias_tab_g[g_i, t5_bucket(t_b[:, None] - tok)].transpose(0, 1, 4, 2, 3)
        s = jnp.einsum('bqghd,bgqkd->bghqk', q_b, kk) + bias
        p = masked_softmax(s, key_ok[:, :, None])
        return jnp.einsum('bghqk,bgqkd->bqghd', p.astype(vv.dtype), vv)

    o_slc = _from_blocks(lax.map(slc_block, (
        _to_blocks(qg, 1, SLC_Q_BLOCK),
        _to_blocks(sel_idx, 2, SLC_Q_BLOCK),
        _to_blocks(sel_ok, 2, SLC_Q_BLOCK),
        t_pos.reshape(T // SLC_Q_BLOCK, SLC_Q_BLOCK))), 1)

    span = WINDOW + Q_BLOCK
    k_pad = jnp.pad(k_win, ((0, 0), (WINDOW, 0), (0, 0), (0, 0)))
    v_pad = jnp.pad(v_win, ((0, 0), (WINDOW, 0), (0, 0), (0, 0)))
    dist_w = WINDOW + jnp.arange(Q_BLOCK)[:, None] - jnp.arange(span)[None, :]
    win_mask = (dist_w >= 0) & (dist_w < WINDOW)
    bias_w = bias_tab[t5_bucket(dist_w)].transpose(2, 3, 0, 1)

    def win_block(args):
        q_b, start = args
        kk = lax.dynamic_slice_in_dim(k_pad, start, span, axis=1)
        vv = lax.dynamic_slice_in_dim(v_pad, start, span, axis=1)
        kpos = start - WINDOW + jnp.arange(span)
        s = jnp.einsum('bqghd,bkgd->bghqk', q_b, kk) + bias_w
        p = masked_softmax(s, win_mask & (kpos >= 0)[None, :])
        return jnp.einsum('bghqk,bkgd->bqghd', p.astype(vv.dtype), vv)

    n_qb = T // Q_BLOCK
    o_win = _from_blocks(lax.map(win_block, (_to_blocks(qg, 1, Q_BLOCK), jnp.arange(n_qb) * Q_BLOCK)), 1)

    o = (gates[:, :, 0, :, :, None] * o_cmp + gates[:, :, 1, :, :, None] * o_slc
         + gates[:, :, 2, :, :, None] * o_win)
    return o.reshape(B, T, NSA_HEADS * NSA_HEAD_DIM)


def mla_attention(q, ckv, kv_norm_g, w_uk, w_uv, cos, sin):
    B, T = q.shape[:2]
    q = q.reshape(B, T, MLA_HEADS, MLA_NOPE + MLA_ROPE)
    scale = (MLA_NOPE + MLA_ROPE) ** -0.5
    q_nope = q[..., :MLA_NOPE] * scale
    q_pe = apply_rope(q[..., MLA_NOPE:], cos[:, None, :], sin[:, None, :]) * scale
    c_kv = rms_norm(ckv[..., :MLA_KV_RANK], kv_norm_g)
    k_pe = apply_rope(ckv[..., MLA_KV_RANK:], cos, sin)
    k_nope = (c_kv @ w_uk).reshape(B, T, MLA_HEADS, MLA_NOPE)
    v = (c_kv @ w_uv).reshape(B, T, MLA_HEADS, MLA_V)
    kpos = jnp.arange(T)

    def blk(args):
        qn_b, qp_b, t_b = args
        s = (jnp.einsum('bqhd,bkhd->bhqk', qn_b, k_nope)
             + jnp.einsum('bqhd,bkd->bhqk', qp_b, k_pe))
        p = masked_softmax(s, t_b[:, None] >= kpos[None, :])
        return jnp.einsum('bhqk,bkhd->bqhd', p.astype(v.dtype), v)

    o = lax.map(blk, (_to_blocks(q_nope, 1, Q_BLOCK), _to_blocks(q_pe, 1, Q_BLOCK),
                      kpos.reshape(T // Q_BLOCK, Q_BLOCK)))
    return _from_blocks(o, 1).reshape(B, T, MLA_HEADS * MLA_V)


def hybrid_mixer(x, w_in, rel_bias, cmp_k_pe, cmp_k_w1, cmp_k_w2, cmp_v_pe, cmp_v_w1, cmp_v_w2,
                 mla_kv_norm, mla_w_uk, mla_w_uv, w_branch_nsa, w_branch_mla, w_mix_out, cos, sin):
    B, T, D = x.shape
    proj = x @ w_in
    y_nsa = nsa_attention(proj[..., :OFF_NSA_KV], proj[..., OFF_NSA_KV:OFF_NSA_GATE],
                          proj[..., OFF_NSA_GATE:OFF_MLA_Q], rel_bias,
                          cmp_k_pe, cmp_k_w1, cmp_k_w2, cmp_v_pe, cmp_v_w1, cmp_v_w2) @ w_branch_nsa
    y_mla = mla_attention(proj[..., OFF_MLA_Q:OFF_MLA_KV], proj[..., OFF_MLA_KV:OFF_MERGE],
                          mla_kv_norm, mla_w_uk, mla_w_uv, cos, sin) @ w_branch_mla
    mg = jax.nn.sigmoid(proj[..., OFF_MERGE:]).reshape(B, T, N_BRANCHES, D)
    return (mg[:, :, 0] * y_nsa + mg[:, :, 1] * y_mla) @ w_mix_out


def memory_attention(x, mem, w_q, w_k, w_v, w_o):
    B, T, _ = x.shape
    M = mem.shape[1]
    q = (x @ w_q).reshape(B, T, MEM_HEADS, MEM_HEAD_DIM) * (MEM_HEAD_DIM ** -0.5)
    k = (mem @ w_k).reshape(B, M, MEM_HEADS, MEM_HEAD_DIM)
    v = (mem @ w_v).reshape(B, M, MEM_HEADS, MEM_HEAD_DIM)
    p = jax.nn.softmax(jnp.einsum('bthd,bmhd->bhtm', q, k).astype(jnp.float32), axis=-1)
    o = jnp.einsum('bhtm,bmhd->bthd', p.astype(v.dtype), v).reshape(B, T, MEM_HEADS * MEM_HEAD_DIM)
    return o @ w_o


def peer_ffn(x, w_query, sub_keys, expert_u, expert_v):
    B, T, D = x.shape
    N = B * T
    xt = x.reshape(N, D)
    q = (xt @ w_query).reshape(N, PEER_HEADS, 2, PEER_HALF)
    s1 = jnp.einsum('nhd,kd->nhk', q[:, :, 0], sub_keys[0])
    s2 = jnp.einsum('nhd,kd->nhk', q[:, :, 1], sub_keys[1])
    v1, i1 = lax.top_k(s1, PEER_TOPK)
    v2, i2 = lax.top_k(s2, PEER_TOPK)
    cand = (v1[..., :, None] + v2[..., None, :]).reshape(N, PEER_HEADS, PEER_TOPK * PEER_TOPK)
    top_s, flat = lax.top_k(cand, PEER_TOPK)
    e1 = jnp.take_along_axis(i1, flat // PEER_TOPK, axis=-1)
    e2 = jnp.take_along_axis(i2, flat % PEER_TOPK, axis=-1)
    experts = e1 * PEER_N_KEYS + e2
    gate = jax.nn.softmax(top_s.astype(jnp.float32), axis=-1).astype(x.dtype)

    def chunk(args):
        x_c, e_c, g_c = args
        act = jax.nn.gelu(jnp.einsum('chkd,cd->chk', expert_u[e_c], x_c))
        return jnp.einsum('chk,chkd->cd', g_c * act, expert_v[e_c])

    n_chunks = N // PEER_CHUNK
    out = lax.map(chunk, (xt.reshape(n_chunks, PEER_CHUNK, D),
                          experts.reshape(n_chunks, PEER_CHUNK, PEER_HEADS, PEER_TOPK),
                          gate.reshape(n_chunks, PEER_CHUNK, PEER_HEADS, PEER_TOPK)))
    return out.reshape(B, T, D)


def setup_inputs(seed: int = 0) -> dict:
    key = jax.random.key(seed)
    ks = iter(jax.random.split(key, 40))
    L = DEPTH

    def dense(shape, fan_in, scale=1.0):
        return jax.random.normal(next(ks), shape, jnp.float32) * (scale * fan_in ** -0.5)

    def gain(shape):
        return 1.0 + 0.1 * jax.random.normal(next(ks), shape, jnp.float32)

    def small(shape, s=0.01):
        return s * jax.random.normal(next(ks), shape, jnp.float32)

    nsa_w = NSA_HEADS * NSA_HEAD_DIM
    mla_w = MLA_HEADS * MLA_V
    mem_w = MEM_HEADS * MEM_HEAD_DIM
    return {
        'x': jax.random.normal(next(ks), (BATCH, SEQ, D_MODEL), jnp.float32),
        'mem': jax.random.normal(next(ks), (BATCH, N_MEM, D_MODEL), jnp.float32),
        'rel_bias': small((REL_BUCKETS, NSA_HEADS), 0.5),
        'w_in': dense((L, D_MODEL, COMBINED_COLS), D_MODEL),
        'cmp_k_pe': small((L, CMP_BLOCK, NSA_HEAD_DIM), 0.5),
        'cmp_k_w1': dense((L, CMP_BLOCK * NSA_HEAD_DIM, CMP_HIDDEN), CMP_BLOCK * NSA_HEAD_DIM),
        'cmp_k_w2': dense((L, CMP_HIDDEN, NSA_HEAD_DIM), CMP_HIDDEN),
        'cmp_v_pe': small((L, CMP_BLOCK, NSA_HEAD_DIM), 0.5),
        'cmp_v_w1': dense((L, CMP_BLOCK * NSA_HEAD_DIM, CMP_HIDDEN), CMP_BLOCK * NSA_HEAD_DIM),
        'cmp_v_w2': dense((L, CMP_HIDDEN, NSA_HEAD_DIM), CMP_HIDDEN),
        'mla_kv_norm': gain((L, MLA_KV_RANK)),
        'mla_w_uk': dense((L, MLA_KV_RANK, MLA_HEADS * MLA_NOPE), MLA_KV_RANK),
        'mla_w_uv': dense((L, MLA_KV_RANK, mla_w), MLA_KV_RANK),
        'w_branch_nsa': dense((L, nsa_w, D_MODEL), nsa_w, BETA),
        'w_branch_mla': dense((L, mla_w, D_MODEL), mla_w, BETA),
        'w_mix_out': dense((L, D_MODEL, D_MODEL), D_MODEL, BETA),
        'ln_mix_g': gain((L, D_MODEL)),
        'ln_mix_b': small((L, D_MODEL)),
        'mem_w_q': dense((L, D_MODEL, mem_w), D_MODEL),
        'mem_w_k': dense((L, D_MODEL, mem_w), D_MODEL),
        'mem_w_v': dense((L, D_MODEL, mem_w), D_MODEL),
        'mem_w_o': dense((L, mem_w, D_MODEL), mem_w, BETA),
        'ln_mem_g': gain((L, D_MODEL)),
        'ln_mem_b': small((L, D_MODEL)),
        'peer_w_query': dense((L, D_MODEL, PEER_HEADS * PEER_QUERY_DIM), D_MODEL),
        'peer_sub_keys': dense((L, 2, PEER_N_KEYS, PEER_HALF), PEER_HALF),
        'peer_u': dense((L, PEER_N_EXPERTS, D_MODEL), D_MODEL),
        'peer_v': dense((L, PEER_N_EXPERTS, D_MODEL), 1.0, BETA),
        'ln_ffn_g': gain((L, D_MODEL)),
        'ln_ffn_b': small((L, D_MODEL)),
    }


def reference(x, mem, rel_bias, w_in, cmp_k_pe, cmp_k_w1, cmp_k_w2, cmp_v_pe, cmp_v_w1, cmp_v_w2,
              mla_kv_norm, mla_w_uk, mla_w_uv, w_branch_nsa, w_branch_mla, w_mix_out, ln_mix_g, ln_mix_b,
              mem_w_q, mem_w_k, mem_w_v, mem_w_o, ln_mem_g, ln_mem_b,
              peer_w_query, peer_sub_keys, peer_u, peer_v, ln_ffn_g, ln_ffn_b):
    cos, sin = rope_tables(x.shape[1])
    for l in range(DEPTH):
        mix = hybrid_mixer(x, w_in[l], rel_bias, cmp_k_pe[l], cmp_k_w1[l], cmp_k_w2[l],
                           cmp_v_pe[l], cmp_v_w1[l], cmp_v_w2[l], mla_kv_norm[l], mla_w_uk[l], mla_w_uv[l],
                           w_branch_nsa[l], w_branch_mla[l], w_mix_out[l], cos, sin)
        x = layer_norm(ALPHA * x + mix, ln_mix_g[l], ln_mix_b[l])
        x = layer_norm(ALPHA * x + memory_attention(x, mem, mem_w_q[l], mem_w_k[l], mem_w_v[l], mem_w_o[l]),
                       ln_mem_g[l], ln_mem_b[l])
        x = layer_norm(ALPHA * x + peer_ffn(x, peer_w_query[l], peer_sub_keys[l], peer_u[l], peer_v[l]),
                       ln_ffn_g[l], ln_ffn_b[l])
    return x
```

```python
import functools
import math

import numpy as np
import jax
import jax.numpy as jnp
from jax import lax
from jax.experimental import pallas as pl
from jax.experimental.pallas import tpu as pltpu

f32 = jnp.float32
bf16 = jnp.bfloat16
i32 = jnp.int32

D_MODEL = 2048
N_MEM = 256
NSA_HEADS = 16
NSA_GROUPS = 2
NSA_HPG = NSA_HEADS // NSA_GROUPS
HEAD_DIM = 128
CMP_BLOCK = 32
CMP_STRIDE = 16
CMP_HIDDEN = 256
SLC_BLOCK = 64
SLC_TOPK = 16
WINDOW = 512
FORCED_SCORE = 1e4
MLA_HEADS = 16
MLA_NOPE = 128
MLA_ROPE = 64
MLA_KV_RANK = 512
MLA_SLOT = 256
ROPE_THETA = 10000.0
REL_BUCKETS = 32
REL_MAX_EXACT = 16
REL_MAX_DIST = 4096
MEM_HEADS = 4
PEER_HEADS = 8
PEER_N_KEYS = 128
PEER_TOPK = 16
PEER_HK = PEER_HEADS * PEER_TOPK
DEPTH = 1
ALPHA = (2.0 * DEPTH) ** 0.25
NEG_INF = -1e30
PAD_SCORE = -3e38
LN_EPS = 1e-5
RMS_EPS = 1e-6

NSA_Q_COLS = NSA_HEADS * HEAD_DIM
NSA_KV_COLS = NSA_GROUPS * HEAD_DIM
OFF_NSA_KV = NSA_Q_COLS
OFF_NSA_GATE = OFF_NSA_KV + 6 * NSA_KV_COLS
OFF_MLA_Q = OFF_NSA_GATE + 3 * NSA_HEADS
OFF_MLA_KV = OFF_MLA_Q + MLA_HEADS * (MLA_NOPE + MLA_ROPE)
OFF_MERGE = OFF_MLA_KV + MLA_KV_RANK + MLA_ROPE

LANES = 128
ATT_TILE = 256
CMP_TQ = 128
VMEM_LIMIT = 56 * 1024 * 1024

NT = (((1,), (1,)), ((), ()))


def _params(sem):
    return pltpu.CompilerParams(dimension_semantics=sem, vmem_limit_bytes=VMEM_LIMIT)


def _mm_body(a_ref, b_ref, o_ref, *, scale, act):
    acc = jnp.dot(a_ref[...].astype(bf16), b_ref[...], preferred_element_type=f32)
    if scale != 1.0:
        acc = acc * scale
    if act == "sigmoid":
        acc = jax.nn.sigmoid(acc)
    o_ref[...] = acc.astype(o_ref.dtype)


def _mm(a, b, out_dtype, tm, tn, scale=1.0, act=None, name="mm"):
    M, K = a.shape
    Nc = b.shape[1]
    return pl.pallas_call(
        functools.partial(_mm_body, scale=scale, act=act),
        grid=(M // tm, Nc // tn),
        in_specs=[pl.BlockSpec((tm, K), lambda i, j: (i, 0)), pl.BlockSpec((K, tn), lambda i, j: (0, j))],
        out_specs=pl.BlockSpec((tm, tn), lambda i, j: (i, j)),
        out_shape=jax.ShapeDtypeStruct((M, Nc), out_dtype),
        compiler_params=_params(("parallel", "parallel")),
        name=name,
    )(a, b)


def _rope_slot(y, ta, tb, tc):
    return y * ta + pltpu.roll(y, LANES - 32, 1) * tb + pltpu.roll(y, 32, 1) * tc


def _mlaq_body(a_ref, b_ref, ta_ref, tb_ref, tc_ref, o_ref, *, scale, nslot):
    acc = jnp.dot(a_ref[...], b_ref[...], preferred_element_type=f32)
    ta, tb, tc = ta_ref[...], tb_ref[...], tc_ref[...]
    for s in range(nslot):
        lo = s * MLA_SLOT
        o_ref[:, lo:lo + MLA_NOPE] = (acc[:, lo:lo + MLA_NOPE] * scale).astype(o_ref.dtype)
        rot = _rope_slot(acc[:, lo + MLA_NOPE:lo + MLA_SLOT], ta, tb, tc)
        o_ref[:, lo + MLA_NOPE:lo + MLA_SLOT] = (rot * scale).astype(o_ref.dtype)


def _mla_q(xb, w, ropes, T, tm, tn):
    M, K = xb.shape
    Nc = w.shape[1]
    nt = T // tm
    tab = pl.BlockSpec((tm, LANES), lambda i, j: (i % nt, 0))
    return pl.pallas_call(
        functools.partial(_mlaq_body, scale=(MLA_NOPE + MLA_ROPE) ** -0.5, nslot=tn // MLA_SLOT),
        grid=(M // tm, Nc // tn),
        in_specs=[pl.BlockSpec((tm, K), lambda i, j: (i, 0)), pl.BlockSpec((K, tn), lambda i, j: (0, j)), tab, tab, tab],
        out_specs=pl.BlockSpec((tm, tn), lambda i, j: (i, j)),
        out_shape=jax.ShapeDtypeStruct((M, Nc), bf16),
        compiler_params=_params(("parallel", "parallel")),
        name="mla_q_proj_rope",
    )(xb, w, *ropes)


def _mlakv_body(c_ref, g_ref, wuk_ref, wuv_ref, ta_ref, tb_ref, tc_ref, k_ref, v_ref):
    x = c_ref[...]
    c = x[:, :MLA_KV_RANK]
    ms = jnp.mean(c * c, axis=-1, keepdims=True)
    cn = (c * lax.rsqrt(ms + RMS_EPS) * g_ref[...]).astype(bf16)
    kpe = _rope_slot(x[:, MLA_KV_RANK:], ta_ref[...], tb_ref[...], tc_ref[...]).astype(bf16)
    kn = jnp.dot(cn, wuk_ref[...], preferred_element_type=f32).astype(bf16)
    v_ref[...] = jnp.dot(cn, wuv_ref[...], preferred_element_type=f32).astype(bf16)
    for h in range(MLA_HEADS):
        k_ref[:, h * MLA_SLOT:h * MLA_SLOT + MLA_NOPE] = kn[:, h * MLA_NOPE:(h + 1) * MLA_NOPE]
        k_ref[:, h * MLA_SLOT + MLA_NOPE:(h + 1) * MLA_SLOT] = kpe


def _mla_kv(ckv, g, wuk, wuv, ropes, T, tm):
    M, K = ckv.shape
    nt = T // tm
    tab = pl.BlockSpec((tm, LANES), lambda i: (i % nt, 0))
    full = lambda shape: pl.BlockSpec(shape, lambda i: (0,) * len(shape))
    return pl.pallas_call(
        _mlakv_body,
        grid=(M // tm,),
        in_specs=[pl.BlockSpec((tm, K), lambda i: (i, 0)), full(g.shape), full(wuk.shape), full(wuv.shape), tab, tab, tab],
        out_specs=[pl.BlockSpec((tm, MLA_HEADS * MLA_SLOT), lambda i: (i, 0)),
                   pl.BlockSpec((tm, MLA_HEADS * HEAD_DIM), lambda i: (i, 0))],
        out_shape=[jax.ShapeDtypeStruct((M, MLA_HEADS * MLA_SLOT), bf16),
                   jax.ShapeDtypeStruct((M, MLA_HEADS * HEAD_DIM), bf16)],
        compiler_params=_params(("parallel",)),
        name="mla_kv_prep",
    )(ckv, g, wuk, wuv, *ropes)


def _online_softmax_step(s, v, m_ref, l_ref, acc_ref, h, valid=None):
    m_prev = m_ref[h]
    m_new = jnp.maximum(m_prev, jnp.max(s, axis=-1, keepdims=True))
    alpha = jnp.exp(m_prev - m_new)
    p = jnp.exp(s - m_new)
    if valid is not None:
        p = jnp.where(valid, p, 0.0)
    l_ref[h] = alpha * l_ref[h] + jnp.sum(p, axis=-1, keepdims=True)
    acc_ref[h] = alpha * acc_ref[h] + jnp.dot(p.astype(bf16), v, preferred_element_type=f32)
    m_ref[h] = m_new


def _mla_attn_body(q_ref, k_ref, v_ref, o_ref, m_ref, l_ref, acc_ref, *, hb, ts):
    qi = pl.program_id(2)
    ki = pl.program_id(3)

    @pl.when(ki == 0)
    def _():
        m_ref[...] = jnp.full(m_ref.shape, NEG_INF, f32)
        l_ref[...] = jnp.zeros(l_ref.shape, f32)
        acc_ref[...] = jnp.zeros(acc_ref.shape, f32)

    def step(diag):
        if diag:
            row = lax.broadcasted_iota(i32, (ts, ts), 0)
            col = lax.broadcasted_iota(i32, (ts, ts), 1)
            causal = row >= col
        for h in range(hb):
            s = lax.dot_general(q_ref[:, h * MLA_SLOT:(h + 1) * MLA_SLOT], k_ref[:, h * MLA_SLOT:(h + 1) * MLA_SLOT],
                                NT, preferred_element_type=f32)
            if diag:
                s = jnp.where(causal, s, NEG_INF)
            _online_softmax_step(s, v_ref[:, h * HEAD_DIM:(h + 1) * HEAD_DIM], m_ref, l_ref, acc_ref, h)

    pl.when(ki < qi)(lambda: step(False))

    @pl.when(ki == qi)
    def _():
        step(True)
        for h in range(hb):
            o_ref[:, h * HEAD_DIM:(h + 1) * HEAD_DIM] = (acc_ref[h] / l_ref[h]).astype(o_ref.dtype)


def _mla_attn(q, k, v, B, T, hb=8, ts=ATT_TILE):
    nt = T // ts
    return pl.pallas_call(
        functools.partial(_mla_attn_body, hb=hb, ts=ts),
        grid=(B, MLA_HEADS // hb, nt, nt),
        in_specs=[pl.BlockSpec((ts, hb * MLA_SLOT), lambda b, h, qi, ki: (b * nt + qi, h)),
                  pl.BlockSpec((ts, hb * MLA_SLOT), lambda b, h, qi, ki: (b * nt + jnp.minimum(ki, qi), h)),
                  pl.BlockSpec((ts, hb * HEAD_DIM), lambda b, h, qi, ki: (b * nt + jnp.minimum(ki, qi), h))],
        out_specs=pl.BlockSpec((ts, hb * HEAD_DIM), lambda b, h, qi, ki: (b * nt + qi, h)),
        out_shape=jax.ShapeDtypeStruct((B * T, MLA_HEADS * HEAD_DIM), bf16),
        scratch_shapes=[pltpu.VMEM((hb, ts, 1), f32), pltpu.VMEM((hb, ts, 1), f32), pltpu.VMEM((hb, ts, HEAD_DIM), f32)],
        compiler_params=_params(("parallel", "parallel", "parallel", "arbitrary")),
        name="mla_flash_attention",
    )(q, k, v)


def _toeplitz_bias(row, ts):
    x = jnp.broadcast_to(row, (ts, 2 * ts))
    return pltpu.roll(x, 0, 1, stride=1, stride_axis=0)[:, ts:]


def _nsa_band_body(q_ref, k_ref, v_ref, rw_ref, *rest, ts, select, nback):
    if select:
        sel_ref, e_ref, o_ref, m_ref, l_ref, acc_ref = rest
    else:
        o_ref, m_ref, l_ref, acc_ref = rest
    qi = pl.program_id(2)
    kk = pl.program_id(3)
    ki = kk if select else qi - nback + kk

    @pl.when(kk == 0)
    def _():
        m_ref[...] = jnp.full(m_ref.shape, NEG_INF, f32)
        l_ref[...] = jnp.zeros(l_ref.shape, f32)
        acc_ref[...] = jnp.zeros(acc_ref.shape, f32)

    @pl.when((ki >= 0) & (ki <= qi))
    def _():
        dist = (qi - ki) * ts + lax.broadcasted_iota(i32, (ts, ts), 0) - lax.broadcasted_iota(i32, (ts, ts), 1)
        if select:
            picked = jnp.dot(sel_ref[...], e_ref[...], preferred_element_type=f32)
            valid = (dist >= 0) & (picked > 0.5)
        else:
            valid = (dist >= 0) & (dist < WINDOW)
        k = k_ref[...].astype(bf16)
        v = v_ref[...].astype(bf16)
        for h in range(NSA_HPG):
            s = lax.dot_general(q_ref[:, h * HEAD_DIM:(h + 1) * HEAD_DIM], k, NT, preferred_element_type=f32)
            s = jnp.where(valid, s + _toeplitz_bias(rw_ref[0, h:h + 1, :], ts), NEG_INF)
            _online_softmax_step(s, v, m_ref, l_ref, acc_ref, h, valid=valid)

    @pl.when(ki == qi)
    def _():
        for h in range(NSA_HPG):
            o_ref[:, h * HEAD_DIM:(h + 1) * HEAD_DIM] = (acc_ref[h] / l_ref[h]).astype(o_ref.dtype)


def _nsa_band(q, kv, rw, B, T, kcol, vcol, sel=None, emat=None, ts=ATT_TILE):
    nt = T // ts
    select = sel is not None
    nback = 0 if select else (WINDOW + ts - 2) // ts
    nk = nt if select else nback + 1
    gw = NSA_HPG * HEAD_DIM

    def ktile(qi, kk):
        return jnp.minimum(kk, qi) if select else jnp.maximum(qi - nback + kk, 0)

    in_specs = [
        pl.BlockSpec((ts, gw), lambda b, g, qi, kk: (b * nt + qi, g)),
        pl.BlockSpec((ts, HEAD_DIM), lambda b, g, qi, kk: (b * nt + ktile(qi, kk), kcol * NSA_GROUPS + g)),
        pl.BlockSpec((ts, HEAD_DIM), lambda b, g, qi, kk: (b * nt + ktile(qi, kk), vcol * NSA_GROUPS + g)),
        pl.BlockSpec((1, NSA_HPG, 2 * ts), lambda b, g, qi, kk: (qi - ktile(qi, kk), g, 0)),
    ]
    args = [q, kv, kv, rw]
    if select:
        in_specs += [pl.BlockSpec((ts, LANES), lambda b, g, qi, kk: (b * nt + qi, g)),
                     pl.BlockSpec((LANES, ts), lambda b, g, qi, kk: (0, ktile(qi, kk)))]
        args += [sel, emat]
    return pl.pallas_call(
        functools.partial(_nsa_band_body, ts=ts, select=select, nback=nback),
        grid=(B, NSA_GROUPS, nt, nk),
        in_specs=in_specs,
        out_specs=pl.BlockSpec((ts, gw), lambda b, g, qi, kk: (b * nt + qi, g)),
        out_shape=jax.ShapeDtypeStruct((B * T, NSA_HEADS * HEAD_DIM), bf16),
        scratch_shapes=[pltpu.VMEM((NSA_HPG, ts, 1), f32), pltpu.VMEM((NSA_HPG, ts, 1), f32),
                        pltpu.VMEM((NSA_HPG, ts, HEAD_DIM), f32)],
        compiler_params=_params(("parallel", "parallel", "parallel", "arbitrary")),
        name="nsa_selected_attention" if select else "nsa_window_attention",
    )(*args)


def _compress_body(x_ref, pe_ref, w1_ref, w2_ref, o_ref, ot_ref, *, nc):
    half = CMP_BLOCK // 2
    h_lo = jnp.zeros((nc, CMP_HIDDEN), f32)
    h_hi = jnp.zeros((nc, CMP_HIDDEN), f32)
    for p in range(half):
        rows = x_ref[pl.ds(p, nc, stride=CMP_STRIDE), :]
        h_lo += jnp.dot((rows + pe_ref[0, p:p + 1, :]).astype(bf16), w1_ref[0, p], preferred_element_type=f32)
        h_hi += jnp.dot((rows + pe_ref[0, half + p:half + p + 1, :]).astype(bf16), w1_ref[0, half + p],
                        preferred_element_type=f32)
    hid = h_lo + pltpu.roll(h_hi, nc - 1, 0)
    out = jnp.dot(jax.nn.gelu(hid).astype(bf16), w2_ref[0], preferred_element_type=f32)
    out = jnp.where(lax.broadcasted_iota(i32, out.shape, 0) < nc - 1, out, 0.0)
    o_ref[0, 0, 0] = out.astype(bf16)
    ot_ref[0, 0, 0] = out.T.astype(bf16)


def _compress(kv, pe, w1, w2, B, T):
    nc = T // CMP_STRIDE
    return pl.pallas_call(
        functools.partial(_compress_body, nc=nc),
        grid=(2, B, NSA_GROUPS),
        in_specs=[pl.BlockSpec((T, HEAD_DIM), lambda c, b, g: (b, c * NSA_GROUPS + g)),
                  pl.BlockSpec((1, CMP_BLOCK, HEAD_DIM), lambda c, b, g: (c, 0, 0)),
                  pl.BlockSpec((1, CMP_BLOCK, HEAD_DIM, CMP_HIDDEN), lambda c, b, g: (c, 0, 0, 0)),
                  pl.BlockSpec((1, CMP_HIDDEN, HEAD_DIM), lambda c, b, g: (c, 0, 0))],
        out_specs=[pl.BlockSpec((1, 1, 1, nc, HEAD_DIM), lambda c, b, g: (c, b, g, 0, 0)),
                   pl.BlockSpec((1, 1, 1, HEAD_DIM, nc), lambda c, b, g: (c, b, g, 0, 0))],
        out_shape=[jax.ShapeDtypeStruct((2, B, NSA_GROUPS, nc, HEAD_DIM), bf16),
                   jax.ShapeDtypeStruct((2, B, NSA_GROUPS, HEAD_DIM, nc), bf16)],
        compiler_params=_params(("parallel", "parallel", "parallel")),
        name="nsa_compress_mlp",
    )(kv, pe, w1, w2)


def _cmp_body(q_ref, kc_ref, vct_ref, cw_ref, ovt_ref, o_ref, sel_ref, *, nc, nq):
    tq = CMP_TQ
    qi = pl.program_id(2)
    t0 = qi * tq
    key_row = lax.broadcasted_iota(i32, (nc, tq), 0)
    q_col = lax.broadcasted_iota(i32, (nc, tq), 1)
    valid = (t0 + q_col - CMP_STRIDE * key_row - (CMP_BLOCK - 1)) >= 0
    kc = kc_ref[0, 0, 0]
    vct = vct_ref[0, 0, 0]
    psum = jnp.zeros((nc, tq), f32)
    for h in range(NSA_HPG):
        pieces = []
        for c in range(nc // 8):
            win = jnp.broadcast_to(cw_ref[qi - c + nq - 1, h:h + 1, :], (8, 2 * tq))
            pieces.append(pltpu.roll(win, 0, 1, stride=CMP_STRIDE, stride_axis=0)[:, :tq])
        bias = jnp.concatenate(pieces, axis=0)
        s = lax.dot_general(kc, q_ref[:, h * HEAD_DIM:(h + 1) * HEAD_DIM], NT, preferred_element_type=f32)
        s = jnp.where(valid, s + bias, NEG_INF)
        m = jnp.max(s, axis=0, keepdims=True)
        e = jnp.where(valid, jnp.exp(s - m), 0.0)
        l = jnp.sum(e, axis=0, keepdims=True)
        p = e / jnp.where(l > 0.0, l, 1.0)
        psum += p
        o_t = jnp.dot(vct, p.astype(bf16), preferred_element_type=f32)
        o_ref[:, h * HEAD_DIM:(h + 1) * HEAD_DIM] = o_t.T.astype(o_ref.dtype)

    imp = jnp.dot(ovt_ref[...], psum, preferred_element_type=f32, precision=lax.Precision.HIGHEST)
    nrow = imp.shape[0]
    n_slc = (nc * CMP_STRIDE) // SLC_BLOCK
    j = lax.broadcasted_iota(i32, (nrow, tq), 0)
    t = t0 + lax.broadcasted_iota(i32, (nrow, tq), 1)
    cur = jnp.right_shift(t, 6)
    ok = j * SLC_BLOCK <= t
    forced = (j == 0) | (j == cur) | (j == cur - 1)
    score = jnp.where(forced, FORCED_SCORE, jnp.where(ok, imp, NEG_INF))
    score = jnp.where(j < n_slc, score, PAD_SCORE)
    chosen = jnp.zeros((nrow, tq), f32)
    for _ in range(min(SLC_TOPK, n_slc)):
        top = jnp.max(score, axis=0, keepdims=True)
        first = jnp.min(jnp.where(score == top, j, nrow), axis=0, keepdims=True)
        hit = j == first
        chosen = jnp.where(hit, 1.0, chosen)
        score = jnp.where(hit, PAD_SCORE, score)
    chosen = jnp.where(ok & (j < n_slc), chosen, 0.0)
    sel_ref[...] = chosen.T.astype(sel_ref.dtype)


def _nsa_cmp(q, kc, vct, cw, ovt, B, T):
    nc = T // CMP_STRIDE
    nq = T // CMP_TQ
    gw = NSA_HPG * HEAD_DIM
    return pl.pallas_call(
        functools.partial(_cmp_body, nc=nc, nq=nq),
        grid=(B, NSA_GROUPS, nq),
        in_specs=[pl.BlockSpec((CMP_TQ, gw), lambda b, g, qi: (b * nq + qi, g)),
                  pl.BlockSpec((1, 1, 1, nc, HEAD_DIM), lambda b, g, qi: (0, b, g, 0, 0)),
                  pl.BlockSpec((1, 1, 1, HEAD_DIM, nc), lambda b, g, qi: (1, b, g, 0, 0)),
                  pl.BlockSpec((2 * nq - 1, NSA_HPG, 2 * CMP_TQ), lambda b, g, qi: (0, g, 0)),
                  pl.BlockSpec(ovt.shape, lambda b, g, qi: (0, 0))],
        out_specs=[pl.BlockSpec((CMP_TQ, gw), lambda b, g, qi: (b * nq + qi, g)),
                   pl.BlockSpec((CMP_TQ, LANES), lambda b, g, qi: (b * nq + qi, g))],
        out_shape=[jax.ShapeDtypeStruct((B * T, NSA_HEADS * HEAD_DIM), bf16),
                   jax.ShapeDtypeStruct((B * T, NSA_GROUPS * LANES), bf16)],
        compiler_params=_params(("parallel", "parallel", "parallel")),
        name="nsa_compressed_attention_select",
    )(q, kc, vct, cw, ovt)


def _layer_norm(y, g, b):
    mu = jnp.mean(y, axis=-1, keepdims=True)
    var = jnp.mean(jnp.square(y - mu), axis=-1, keepdims=True)
    return (y - mu) * lax.rsqrt(var + LN_EPS) * g + b


def _gate_combine_body(oc_ref, os_ref, ow_ref, g_ref, o_ref):
    g = g_ref[...]
    for h in range(NSA_HEADS):
        sl = slice(h * HEAD_DIM, (h + 1) * HEAD_DIM)
        o = (g[:, h:h + 1] * oc_ref[:, sl].astype(f32)
             + g[:, NSA_HEADS + h:NSA_HEADS + h + 1] * os_ref[:, sl].astype(f32)
             + g[:, 2 * NSA_HEADS + h:2 * NSA_HEADS + h + 1] * ow_ref[:, sl].astype(f32))
        o_ref[:, sl] = o.astype(o_ref.dtype)


def _gate_combine(oc, osl, ow, gates, tm):
    M, W = oc.shape
    blk = pl.BlockSpec((tm, W), lambda i: (i, 0))
    return pl.pallas_call(
        _gate_combine_body,
        grid=(M // tm,),
        in_specs=[blk, blk, blk, pl.BlockSpec((tm, LANES), lambda i: (i, 0))],
        out_specs=blk,
        out_shape=jax.ShapeDtypeStruct((M, W), bf16),
        compiler_params=_params(("parallel",)),
        name="nsa_branch_gate_combine",
    )(oc, osl, ow, gates)


def _branch_merge_body(a_ref, b_ref, wa_ref, wb_ref, ga_ref, gb_ref, o_ref):
    ya = jnp.dot(a_ref[...], wa_ref[...], preferred_element_type=f32)
    yb = jnp.dot(b_ref[...], wb_ref[...], preferred_element_type=f32)
    o_ref[...] = (ga_ref[...] * ya + gb_ref[...] * yb).astype(o_ref.dtype)


def _branch_merge(a, b, wa, wb, mg, tm, tn):
    M, K = a.shape
    Nc = wa.shape[1]
    nj = Nc // tn
    return pl.pallas_call(
        _branch_merge_body,
        grid=(M // tm, nj),
        in_specs=[pl.BlockSpec((tm, K), lambda i, j: (i, 0)), pl.BlockSpec((tm, K), lambda i, j: (i, 0)),
                  pl.BlockSpec((K, tn), lambda i, j: (0, j)), pl.BlockSpec((K, tn), lambda i, j: (0, j)),
                  pl.BlockSpec((tm, tn), lambda i, j: (i, j)), pl.BlockSpec((tm, tn), lambda i, j: (i, nj + j))],
        out_specs=pl.BlockSpec((tm, tn), lambda i, j: (i, j)),
        out_shape=jax.ShapeDtypeStruct((M, Nc), bf16),
        compiler_params=_params(("parallel", "parallel")),
        name="branch_proj_merge",
    )(a, b, wa, wb, mg, mg)


def _proj_ln_body(z_ref, w_ref, x_ref, g_ref, b_ref, o_ref):
    y = jnp.dot(z_ref[...], w_ref[...], preferred_element_type=f32)
    o_ref[...] = _layer_norm(ALPHA * x_ref[...] + y, g_ref[...], b_ref[...])


def _proj_ln(z, w, x, g, b, tm):
    M, K = z.shape
    D = w.shape[1]
    vec = pl.BlockSpec((1, D), lambda i: (0, 0))
    return pl.pallas_call(
        _proj_ln_body,
        grid=(M // tm,),
        in_specs=[pl.BlockSpec((tm, K), lambda i: (i, 0)), pl.BlockSpec((K, D), lambda i: (0, 0)),
                  pl.BlockSpec((tm, D), lambda i: (i, 0)), vec, vec],
        out_specs=pl.BlockSpec((tm, D), lambda i: (i, 0)),
        out_shape=jax.ShapeDtypeStruct((M, D), f32),
        compiler_params=_params(("parallel",)),
        name="mix_out_proj_layernorm",
    )(z, w, x, g, b)


def _mem_attn_body(x_ref, wq_ref, k_ref, v_ref, wo_ref, g_ref, b_ref, o_ref):
    x = x_ref[...]
    q = (jnp.dot(x.astype(bf16), wq_ref[...], preferred_element_type=f32) * HEAD_DIM ** -0.5).astype(bf16)
    heads = []
    for h in range(MEM_HEADS):
        sl = slice(h * HEAD_DIM, (h + 1) * HEAD_DIM)
        s = lax.dot_general(q[:, sl], k_ref[0, :, sl], NT, preferred_element_type=f32)
        e = jnp.exp(s - jnp.max(s, axis=-1, keepdims=True))
        p = e / jnp.sum(e, axis=-1, keepdims=True)
        heads.append(jnp.dot(p.astype(bf16), v_ref[0, :, sl], preferred_element_type=f32).astype(bf16))
    y = jnp.dot(jnp.concatenate(heads, axis=1), wo_ref[...], preferred_element_type=f32)
    o_ref[...] = _layer_norm(ALPHA * x + y, g_ref[...], b_ref[...])


def _mem_attn(x, wq, km, vm, wo, g, b, T, tm):
    M, D = x.shape
    nt = T // tm
    W = wq.shape[1]
    vec = pl.BlockSpec((1, D), lambda i: (0, 0))
    return pl.pallas_call(
        _mem_attn_body,
        grid=(M // tm,),
        in_specs=[pl.BlockSpec((tm, D), lambda i: (i, 0)), pl.BlockSpec((D, W), lambda i: (0, 0)),
                  pl.BlockSpec((1, N_MEM, W), lambda i: (i // nt, 0, 0)),
                  pl.BlockSpec((1, N_MEM, W), lambda i: (i // nt, 0, 0)),
                  pl.BlockSpec((W, D), lambda i: (0, 0)), vec, vec],
        out_specs=pl.BlockSpec((tm, D), lambda i: (i, 0)),
        out_shape=jax.ShapeDtypeStruct((M, D), f32),
        compiler_params=_params(("parallel",)),
        name="memory_cross_attention_layernorm",
    )(x, wq, km, vm, wo, g, b)


def _topk_rows(s, k):
    rows, n = s.shape
    idx = lax.broadcasted_iota(i32, s.shape, 0)
    slot = lax.broadcasted_iota(i32, (k, n), 0)
    vals = jnp.zeros((k, n), f32)
    inds = jnp.zeros((k, n), i32)
    for r in range(k):
        top = jnp.max(s, axis=0, keepdims=True)
        first = jnp.min(jnp.where(s == top, idx, rows), axis=0, keepdims=True)
        vals = jnp.where(slot == r, top, vals)
        inds = jnp.where(slot == r, first, inds)
        s = jnp.where(idx == first, PAD_SCORE, s)
    return vals, inds


def _pick_rows(table, sel):
    out = jnp.zeros(sel.shape, table.dtype)
    for a in range(table.shape[0]):
        out = jnp.where(sel == a, table[a:a + 1, :], out)
    return out


def _peer_route_body(q_ref, keys_ref, e1_ref, e2_ref, gt_ref, e1_scr, e2_scr, g_scr):
    k1 = keys_ref[0].astype(bf16)
    k2 = keys_ref[1].astype(bf16)
    for h in range(PEER_HEADS):
        q1 = q_ref[:, (2 * h) * LANES:(2 * h + 1) * LANES]
        q2 = q_ref[:, (2 * h + 1) * LANES:(2 * h + 2) * LANES]
        s1 = lax.dot_general(k1, q1, NT, preferred_element_type=f32)
        s2 = lax.dot_general(k2, q2, NT, preferred_element_type=f32)
        v1, i1 = _topk_rows(s1, PEER_TOPK)
        v2, i2 = _topk_rows(s2, PEER_TOPK)
        cand = jnp.concatenate([v1[a:a + 1, :] + v2 for a in range(PEER_TOPK)], axis=0)
        top_s, flat = _topk_rows(cand, PEER_TOPK)
        e = jnp.exp(top_s - top_s[0:1, :])
        rows = slice(h * PEER_TOPK, (h + 1) * PEER_TOPK)
        g_scr[rows, :] = e / jnp.sum(e, axis=0, keepdims=True)
        e1_scr[rows, :] = _pick_rows(i1, jnp.right_shift(flat, 4))
        e2_scr[rows, :] = _pick_rows(i2, jnp.bitwise_and(flat, PEER_TOPK - 1))
    e1_ref[...] = e1_scr[...].T
    e2_ref[...] = e2_scr[...].T
    gt_ref[...] = g_scr[...].T


def _peer_route(q, keys, tn=LANES):
    M, W = q.shape
    out = pl.BlockSpec((tn, PEER_HK), lambda i: (i, 0))
    return pl.pallas_call(
        _peer_route_body,
        grid=(M // tn,),
        in_specs=[pl.BlockSpec((tn, W), lambda i: (i, 0)), pl.BlockSpec(keys.shape, lambda i: (0, 0, 0))],
        out_specs=[out, out, out],
        out_shape=[jax.ShapeDtypeStruct((M, PEER_HK), i32), jax.ShapeDtypeStruct((M, PEER_HK), i32),
                   jax.ShapeDtypeStruct((M, PEER_HK), f32)],
        scratch_shapes=[pltpu.VMEM((PEER_HK, tn), i32), pltpu.VMEM((PEER_HK, tn), i32), pltpu.VMEM((PEER_HK, tn), f32)],
        compiler_params=_params(("parallel",)),
        name="peer_product_key_routing",
    )(q, keys)


def _peer_weights_body(e1_ref, e2_ref, g_ref, w_ref, *, tn):
    sub = lax.broadcasted_iota(i32, (PEER_N_KEYS, PEER_HK), 0)

    def one(n, carry):
        r1 = e1_ref[pl.ds(n, 1), :]
        r2 = e2_ref[pl.ds(n, 1), :]
        g = g_ref[pl.ds(n, 1), :]
        a_t = (sub == r1).astype(bf16)
        b_t = jnp.where(sub == r2, g, 0.0).astype(bf16)
        w_ref[n] = lax.dot_general(a_t, b_t, NT, preferred_element_type=f32).astype(w_ref.dtype)
        return carry

    lax.fori_loop(0, tn, one, 0)


def _peer_weights(e1, e2, g, tn=64):
    M = e1.shape[0]
    blk = pl.BlockSpec((tn, PEER_HK), lambda i: (i, 0))
    return pl.pallas_call(
        functools.partial(_peer_weights_body, tn=tn),
        grid=(M // tn,),
        in_specs=[blk, blk, blk],
        out_specs=pl.BlockSpec((tn, PEER_N_KEYS, PEER_N_KEYS), lambda i: (i, 0, 0)),
        out_shape=jax.ShapeDtypeStruct((M, PEER_N_KEYS, PEER_N_KEYS), bf16),
        compiler_params=_params(("parallel",)),
        name="peer_routing_weights",
    )(e1, e2, g)


def _peer_dense_body(x_ref, w_ref, u_ref, v_ref, g_ref, b_ref, o_ref, acc_ref, *, rows):
    e = pl.program_id(1)

    @pl.when(e == 0)
    def _():
        acc_ref[...] = jnp.zeros(acc_ref.shape, f32)

    xb = x_ref[...].astype(bf16)
    for r in range(rows):
        sl = slice(r * PEER_N_KEYS, (r + 1) * PEER_N_KEYS)
        act = jax.nn.gelu(lax.dot_general(xb, u_ref[sl, :], NT, preferred_element_type=f32))
        wa = (w_ref[r].astype(f32) * act).astype(bf16)
        acc_ref[...] += jnp.dot(wa, v_ref[sl, :], preferred_element_type=f32)

    @pl.when(e == pl.num_programs(1) - 1)
    def _():
        o_ref[...] = _layer_norm(ALPHA * x_ref[...] + acc_ref[...], g_ref[...], b_ref[...])


def _peer_dense(x, w, u, v, g, b, tm=512, rows=4):
    M, D = x.shape
    te = rows * PEER_N_KEYS
    vec = pl.BlockSpec((1, D), lambda i, e: (0, 0))
    return pl.pallas_call(
        functools.partial(_peer_dense_body, rows=rows),
        grid=(M // tm, PEER_N_KEYS // rows),
        in_specs=[pl.BlockSpec((tm, D), lambda i, e: (i, 0)),
                  pl.BlockSpec((rows, tm, PEER_N_KEYS), lambda i, e: (e, i, 0)),
                  pl.BlockSpec((te, D), lambda i, e: (e, 0)), pl.BlockSpec((te, D), lambda i, e: (e, 0)), vec, vec],
        out_specs=pl.BlockSpec((tm, D), lambda i, e: (i, 0)),
        out_shape=jax.ShapeDtypeStruct((M, D), f32),
        scratch_shapes=[pltpu.VMEM((tm, D), f32)],
        compiler_params=_params(("parallel", "arbitrary")),
        name="peer_dense_experts_layernorm",
    )(x, w, u, v, g, b)


def _t5_bucket(dist):
    dist = jnp.maximum(dist, 0)
    n_log = REL_BUCKETS - REL_MAX_EXACT
    large = REL_MAX_EXACT + (jnp.log(jnp.maximum(dist, 1).astype(f32) / REL_MAX_EXACT)
                             / math.log(REL_MAX_DIST / REL_MAX_EXACT) * n_log).astype(i32)
    large = jnp.minimum(large, REL_BUCKETS - 1)
    return jnp.where(dist < REL_MAX_EXACT, dist, large)


def _position_tables(rel_bias, T):
    by_dist = rel_bias[_t5_bucket(jnp.arange(T))]

    def lookup(d):
        return by_dist[jnp.clip(d, 0, T - 1)]

    ts = ATT_TILE
    c = np.arange(2 * ts)
    band = lookup(np.arange(T // ts)[:, None] * ts + ts - c[None, :])
    band = jnp.transpose(band, (0, 2, 1))
    nq = T // CMP_TQ
    c = np.arange(2 * CMP_TQ)
    wrap = np.where(c < CMP_TQ, c, c - 2 * CMP_TQ)
    delta = np.arange(-(nq - 1), nq)
    cmpw = lookup(delta[:, None] * CMP_TQ - (CMP_BLOCK - 1) + wrap[None, :])
    cmpw = jnp.transpose(cmpw, (0, 2, 1))
    return band.astype(f32), cmpw.astype(f32)


def _rope_slot_tables(T):
    inv = ROPE_THETA ** (-jnp.arange(0, MLA_ROPE, 2, dtype=f32) / MLA_ROPE)
    ang = jnp.arange(T, dtype=f32)[:, None] * inv[None, :]
    cos, sin = jnp.cos(ang), jnp.sin(ang)
    z = jnp.zeros_like(cos)
    ta = jnp.concatenate([cos, cos, z, z], axis=1)
    tb = jnp.concatenate([-sin, z, z, z], axis=1)
    tc = jnp.concatenate([z, sin, z, z], axis=1)
    return ta, tb, tc


def _layer(x, mem, rel_bias, w_in, cmp_k_pe, cmp_k_w1, cmp_k_w2, cmp_v_pe, cmp_v_w1, cmp_v_w2,
           mla_kv_norm, mla_w_uk, mla_w_uv, w_branch_nsa, w_branch_mla, w_mix_out, ln_mix_g, ln_mix_b,
           mem_w_q, mem_w_k, mem_w_v, mem_w_o, ln_mem_g, ln_mem_b,
           peer_w_query, peer_sub_keys, peer_u, peer_v, ln_ffn_g, ln_ffn_b, ropes):
    B, T, D = x.shape
    N = B * T
    x2d = x.reshape(N, D)
    xb = x2d.astype(bf16)
    row = lambda a: a.reshape(1, -1)

    w_nsa_q = w_in[:, :OFF_NSA_KV].astype(bf16)
    w_nsa_kv = w_in[:, OFF_NSA_KV:OFF_NSA_GATE].astype(bf16)
    w_gate = jnp.pad(w_in[:, OFF_NSA_GATE:OFF_MLA_Q], ((0, 0), (0, LANES - 3 * NSA_HEADS))).astype(bf16)
    w_mla_q = jnp.pad(w_in[:, OFF_MLA_Q:OFF_MLA_KV].reshape(D, MLA_HEADS, MLA_NOPE + MLA_ROPE),
                      ((0, 0), (0, 0), (0, MLA_SLOT - MLA_NOPE - MLA_ROPE))).reshape(D, MLA_HEADS * MLA_SLOT).astype(bf16)
    w_mla_kv = jnp.pad(w_in[:, OFF_MLA_KV:OFF_MERGE], ((0, 0), (0, LANES - MLA_ROPE))).astype(bf16)
    w_merge = w_in[:, OFF_MERGE:].astype(bf16)

    nsa_q = _mm(xb, w_nsa_q, bf16, 1024, 512, scale=HEAD_DIM ** -0.5, name="nsa_q_proj")
    nsa_kv = _mm(xb, w_nsa_kv, f32, 1024, 512, name="nsa_kv_proj")
    gates = _mm(xb, w_gate, f32, 1024, LANES, act="sigmoid", name="nsa_gate_proj")
    mla_q = _mla_q(xb, w_mla_q, ropes, T, 1024, 512)
    mla_ckv = _mm(xb, w_mla_kv, f32, 1024, MLA_KV_RANK + LANES, name="mla_kv_proj")
    merge = _mm(xb, w_merge, f32, 1024, 512, act="sigmoid", name="merge_gate_proj")

    band_tab, cmp_tab = _position_tables(rel_bias, T)
    pe = jnp.stack([cmp_k_pe, cmp_v_pe])
    w1 = jnp.stack([cmp_k_w1, cmp_v_w1]).reshape(2, CMP_BLOCK, HEAD_DIM, CMP_HIDDEN).astype(bf16)
    w2 = jnp.stack([cmp_k_w2, cmp_v_w2]).astype(bf16)
    kvc, kvc_t = _compress(nsa_kv, pe, w1, w2, B, T)
    nc = T // CMP_STRIDE
    n_slc = T // SLC_BLOCK
    cs = np.arange(nc)[None, :] * CMP_STRIDE
    ss = np.arange(LANES)[:, None] * SLC_BLOCK
    ovt = ((cs < ss + SLC_BLOCK) & (cs + CMP_BLOCK > ss) & (np.arange(nc)[None, :] < nc - 1)
           & (np.arange(LANES)[:, None] < n_slc)).astype(np.float32)
    o_cmp, sel = _nsa_cmp(nsa_q, kvc, kvc_t, cmp_tab, jnp.asarray(ovt), B, T)
    emat = (np.arange(T)[None, :] // SLC_BLOCK == np.arange(LANES)[:, None]).astype(np.float32)
    o_slc = _nsa_band(nsa_q, nsa_kv, band_tab, B, T, 2, 3, sel=sel, emat=jnp.asarray(emat, dtype=bf16))
    o_win = _nsa_band(nsa_q, nsa_kv, band_tab, B, T, 4, 5)
    o_nsa = _gate_combine(o_cmp, o_slc, o_win, gates, 512)

    mla_k, mla_v = _mla_kv(mla_ckv, row(mla_kv_norm),
                           jnp.asarray(mla_w_uk, bf16), jnp.asarray(mla_w_uv, bf16), ropes, T, 512)
    o_mla = _mla_attn(mla_q, mla_k, mla_v, B, T)

    z = _branch_merge(o_nsa, o_mla, w_branch_nsa.astype(bf16), w_branch_mla.astype(bf16), merge, 1024, 512)
    x1 = _proj_ln(z, w_mix_out.astype(bf16), x2d, row(ln_mix_g), row(ln_mix_b), 512)

    mem2d = mem.reshape(B * N_MEM, D).astype(bf16)
    km = _mm(mem2d, mem_w_k.astype(bf16), bf16, B * N_MEM, 256, name="mem_k_proj").reshape(B, N_MEM, -1)
    vm = _mm(mem2d, mem_w_v.astype(bf16), bf16, B * N_MEM, 256, name="mem_v_proj").reshape(B, N_MEM, -1)
    x2 = _mem_attn(x1, mem_w_q.astype(bf16), km, vm, mem_w_o.astype(bf16), row(ln_mem_g), row(ln_mem_b), T, 512)

    pq = _mm(x2, peer_w_query.astype(bf16), bf16, 1024, 512, name="peer_query_proj")
    e1, e2, gt = _peer_route(pq, peer_sub_keys)
    wts = jnp.transpose(_peer_weights(e1, e2, gt), (1, 0, 2))
    x3 = _peer_dense(x2, wts, peer_u.astype(bf16), peer_v.astype(bf16), row(ln_ffn_g), row(ln_ffn_b))
    return x3.reshape(B, T, D)


def kernel(x, mem, rel_bias, w_in, cmp_k_pe, cmp_k_w1, cmp_k_w2, cmp_v_pe, cmp_v_w1, cmp_v_w2, mla_kv_norm, mla_w_uk, mla_w_uv, w_branch_nsa, w_branch_mla, w_mix_out, ln_mix_g, ln_mix_b, mem_w_q, mem_w_k, mem_w_v, mem_w_o, ln_mem_g, ln_mem_b, peer_w_query, peer_sub_keys, peer_u, peer_v, ln_ffn_g, ln_ffn_b):
    ropes = _rope_slot_tables(x.shape[1])
    for l in range(DEPTH):
        x = _layer(x, mem, rel_bias, w_in[l], cmp_k_pe[l], cmp_k_w1[l], cmp_k_w2[l], cmp_v_pe[l], cmp_v_w1[l],
                   cmp_v_w2[l], mla_kv_norm[l], mla_w_uk[l], mla_w_uv[l], w_branch_nsa[l], w_branch_mla[l],
                   w_mix_out[l], ln_mix_g[l], ln_mix_b[l], mem_w_q[l], mem_w_k[l], mem_w_v[l], mem_w_o[l],
                   ln_mem_g[l], ln_mem_b[l], peer_w_query[l], peer_sub_keys[l], peer_u[l], peer_v[l],
                   ln_ffn_g[l], ln_ffn_b[l], ropes)
    return x
```

```python
import functools
import math

import numpy as np
import jax
import jax.numpy as jnp
from jax import lax
from jax.experimental import pallas as pl
from jax.experimental.pallas import tpu as pltpu

f32 = jnp.float32
bf16 = jnp.bfloat16
i32 = jnp.int32

D_MODEL = 2048
N_MEM = 256
NSA_HEADS = 16
NSA_GROUPS = 2
NSA_HPG = NSA_HEADS // NSA_GROUPS
HEAD_DIM = 128
CMP_BLOCK = 32
CMP_STRIDE = 16
CMP_HIDDEN = 256
SLC_BLOCK = 64
SLC_TOPK = 16
WINDOW = 512
FORCED_SCORE = 1e4
MLA_HEADS = 16
MLA_NOPE = 128
MLA_ROPE = 64
MLA_KV_RANK = 512
MLA_SLOT = 256
ROPE_THETA = 10000.0
REL_BUCKETS = 32
REL_MAX_EXACT = 16
REL_MAX_DIST = 4096
MEM_HEADS = 4
PEER_HEADS = 8
PEER_N_KEYS = 128
PEER_TOPK = 16
PEER_HK = PEER_HEADS * PEER_TOPK
DEPTH = 1
ALPHA = (2.0 * DEPTH) ** 0.25
NEG_INF = -1e30
M_INIT = -1e29
PAD_SCORE = -3e38
LN_EPS = 1e-5
RMS_EPS = 1e-6

NSA_Q_COLS = NSA_HEADS * HEAD_DIM
NSA_KV_COLS = NSA_GROUPS * HEAD_DIM
OFF_NSA_KV = NSA_Q_COLS
OFF_NSA_GATE = OFF_NSA_KV + 6 * NSA_KV_COLS
OFF_MLA_Q = OFF_NSA_GATE + 3 * NSA_HEADS
OFF_MLA_KV = OFF_MLA_Q + MLA_HEADS * (MLA_NOPE + MLA_ROPE)
OFF_MERGE = OFF_MLA_KV + MLA_KV_RANK + MLA_ROPE

LANES = 128
Q_TILE = 256
K_TILE = 512
ROW_CHUNK = 32
CMP_TQ = 128
VMEM_LIMIT = 56 * 1024 * 1024

NT = (((1,), (1,)), ((), ()))


def _params(sem):
    return pltpu.CompilerParams(dimension_semantics=sem, vmem_limit_bytes=VMEM_LIMIT)


def _mm_body(a_ref, b_ref, o_ref, *, scale, act):
    acc = jnp.dot(a_ref[...].astype(bf16), b_ref[...], preferred_element_type=f32)
    if scale != 1.0:
        acc = acc * scale
    if act == "sigmoid":
        acc = jax.nn.sigmoid(acc)
    o_ref[...] = acc.astype(o_ref.dtype)


def _mm(a, b, out_dtype, tm, tn, scale=1.0, act=None, name="mm"):
    M, K = a.shape
    Nc = b.shape[1]
    return pl.pallas_call(
        functools.partial(_mm_body, scale=scale, act=act),
        grid=(M // tm, Nc // tn),
        in_specs=[pl.BlockSpec((tm, K), lambda i, j: (i, 0)), pl.BlockSpec((K, tn), lambda i, j: (0, j))],
        out_specs=pl.BlockSpec((tm, tn), lambda i, j: (i, j)),
        out_shape=jax.ShapeDtypeStruct((M, Nc), out_dtype),
        compiler_params=_params(("parallel", "parallel")),
        name=name,
    )(a, b)


def _rope_slot(y, ta, tb, tc):
    return y * ta + pltpu.roll(y, LANES - 32, 1) * tb + pltpu.roll(y, 32, 1) * tc


def _mlaq_body(a_ref, b_ref, ta_ref, tb_ref, tc_ref, o_ref, *, scale, nslot):
    acc = jnp.dot(a_ref[...], b_ref[...], preferred_element_type=f32)
    ta, tb, tc = ta_ref[...], tb_ref[...], tc_ref[...]
    for s in range(nslot):
        lo = s * MLA_SLOT
        o_ref[:, lo:lo + MLA_NOPE] = (acc[:, lo:lo + MLA_NOPE] * scale).astype(o_ref.dtype)
        rot = _rope_slot(acc[:, lo + MLA_NOPE:lo + MLA_SLOT], ta, tb, tc)
        o_ref[:, lo + MLA_NOPE:lo + MLA_SLOT] = (rot * scale).astype(o_ref.dtype)


def _mla_q(xb, w, ropes, T, tm, tn):
    M, K = xb.shape
    Nc = w.shape[1]
    nt = T // tm
    tab = pl.BlockSpec((tm, LANES), lambda i, j: (i % nt, 0))
    return pl.pallas_call(
        functools.partial(_mlaq_body, scale=(MLA_NOPE + MLA_ROPE) ** -0.5, nslot=tn // MLA_SLOT),
        grid=(M // tm, Nc // tn),
        in_specs=[pl.BlockSpec((tm, K), lambda i, j: (i, 0)), pl.BlockSpec((K, tn), lambda i, j: (0, j)), tab, tab, tab],
        out_specs=pl.BlockSpec((tm, tn), lambda i, j: (i, j)),
        out_shape=jax.ShapeDtypeStruct((M, Nc), bf16),
        compiler_params=_params(("parallel", "parallel")),
        name="mla_q_proj_rope",
    )(xb, w, *ropes)


def _mlakv_body(c_ref, g_ref, wuk_ref, wuv_ref, ta_ref, tb_ref, tc_ref, k_ref, v_ref):
    x = c_ref[...]
    c = x[:, :MLA_KV_RANK]
    ms = jnp.mean(c * c, axis=-1, keepdims=True)
    cn = (c * lax.rsqrt(ms + RMS_EPS) * g_ref[...]).astype(bf16)
    kpe = _rope_slot(x[:, MLA_KV_RANK:], ta_ref[...], tb_ref[...], tc_ref[...]).astype(bf16)
    kn = jnp.dot(cn, wuk_ref[...], preferred_element_type=f32).astype(bf16)
    v_ref[...] = jnp.dot(cn, wuv_ref[...], preferred_element_type=f32).astype(bf16)
    for h in range(MLA_HEADS):
        k_ref[:, h * MLA_SLOT:h * MLA_SLOT + MLA_NOPE] = kn[:, h * MLA_NOPE:(h + 1) * MLA_NOPE]
        k_ref[:, h * MLA_SLOT + MLA_NOPE:(h + 1) * MLA_SLOT] = kpe


def _mla_kv(ckv, g, wuk, wuv, ropes, T, tm):
    M, K = ckv.shape
    nt = T // tm
    tab = pl.BlockSpec((tm, LANES), lambda i: (i % nt, 0))
    full = lambda shape: pl.BlockSpec(shape, lambda i: (0,) * len(shape))
    return pl.pallas_call(
        _mlakv_body,
        grid=(M // tm,),
        in_specs=[pl.BlockSpec((tm, K), lambda i: (i, 0)), full(g.shape), full(wuk.shape), full(wuv.shape), tab, tab, tab],
        out_specs=[pl.BlockSpec((tm, MLA_HEADS * MLA_SLOT), lambda i: (i, 0)),
                   pl.BlockSpec((tm, MLA_HEADS * HEAD_DIM), lambda i: (i, 0))],
        out_shape=[jax.ShapeDtypeStruct((M, MLA_HEADS * MLA_SLOT), bf16),
                   jax.ShapeDtypeStruct((M, MLA_HEADS * HEAD_DIM), bf16)],
        compiler_params=_params(("parallel",)),
        name="mla_kv_prep",
    )(ckv, g, wuk, wuv, *ropes)


FLAG_FIRST, FLAG_LAST, FLAG_CAUSAL = 1, 2, 4


def _tile_schedule(nq, tq, tk, first_key_tile):
    qs, ks, fl = [], [], []
    for qi in range(nq):
        lo = first_key_tile(qi)
        hi = (qi * tq + tq - 1) // tk
        for ki in range(lo, hi + 1):
            qs.append(qi)
            ks.append(ki)
            fl.append((FLAG_FIRST if ki == lo else 0) | (FLAG_LAST if ki == hi else 0)
                      | (FLAG_CAUSAL if ki * tk + tk - 1 > qi * tq else 0))
    return tuple(jnp.asarray(np.array(a, np.int32)) for a in (qs, ks, fl))


def _flash_scratch(heads, tq, tk):
    stat = pltpu.VMEM((heads, tq, LANES), f32)
    return [stat, stat, pltpu.VMEM((heads, tq, HEAD_DIM), f32),
            pltpu.VMEM((3, tq, tk), f32), pltpu.VMEM((2, tq, tk), bf16), pltpu.VMEM((2, tq, LANES), f32),
            pltpu.VMEM((2, tq, LANES), f32), pltpu.VMEM((tq, tk), f32)]


def _flash_init(m_ref, l_ref, acc_ref):
    m_ref[...] = jnp.full(m_ref.shape, M_INIT, f32)
    l_ref[...] = jnp.zeros(l_ref.shape, f32)
    acc_ref[...] = jnp.zeros(acc_ref.shape, f32)


def _flash_heads(nheads, tq, tk, score_fn, add_fn, value_fn, m_ref, l_ref, acc_ref, s_scr, p_scr, mx_scr, a_scr):
    nkc = tk // LANES
    chunks = [(pl.ds(c * ROW_CHUNK, ROW_CHUNK), c * ROW_CHUNK) for c in range(tq // ROW_CHUNK)]

    def scores(h):
        s_scr[h % 3] = score_fn(h)

    def pass_a(h):
        s_ref = s_scr.at[h % 3]
        for rows, r0 in chunks:
            mx = None
            for kc in range(nkc):
                cols = slice(kc * LANES, (kc + 1) * LANES)
                s = s_ref[rows, cols]
                if add_fn is not None:
                    for term in add_fn(h, rows, r0, kc):
                        s = s + term
                    s_ref[rows, cols] = s
                mx = s if mx is None else jnp.maximum(mx, s)
            mx_scr[h % 2, rows, :] = mx
        m_prev = m_ref[h]
        m_new = jnp.maximum(m_prev, jnp.max(mx_scr[h % 2], axis=-1, keepdims=True))
        a_scr[h % 2] = jnp.exp(m_prev - m_new)
        m_ref[h] = m_new

    def pass_b(h):
        s_ref = s_scr.at[h % 3]
        for rows, _ in chunks:
            m_rows = m_ref[h, rows, :]
            for kc in range(nkc):
                cols = slice(kc * LANES, (kc + 1) * LANES)
                p_scr[h % 2, rows, cols] = jnp.exp(s_ref[rows, cols] - m_rows).astype(bf16)
        pv = jnp.dot(p_scr[h % 2], value_fn(h), preferred_element_type=f32)
        alpha = a_scr[h % 2]
        acc_ref[h] = alpha * acc_ref[h] + pv[:, :HEAD_DIM]
        l_ref[h] = alpha * l_ref[h] + pv[:, HEAD_DIM:]

    scores(0)
    for h in range(nheads):
        if h + 1 < nheads:
            scores(h + 1)
        pass_a(h)
        if h >= 1:
            pass_b(h - 1)
    pass_b(nheads - 1)


def _with_ones(v):
    return jnp.concatenate([v, jnp.ones((v.shape[0], LANES), v.dtype)], axis=1)


def _mla_attn_body(qi_tab, ki_tab, fl_tab, q_ref, k_ref, v_ref, o_ref, m_ref, l_ref, acc_ref,
                   s_scr, p_scr, mx_scr, a_scr, mask_scr, *, hb, tq, tk):
    st = pl.program_id(2)
    qi, ki, fl = qi_tab[st], ki_tab[st], fl_tab[st]

    pl.when((fl & FLAG_FIRST) != 0)(lambda: _flash_init(m_ref, l_ref, acc_ref))

    def score(h):
        qk = slice(h * MLA_SLOT, (h + 1) * MLA_SLOT)
        return lax.dot_general(q_ref[:, qk], k_ref[:, qk], NT, preferred_element_type=f32)

    def value(h):
        return _with_ones(v_ref[:, h * HEAD_DIM:(h + 1) * HEAD_DIM])

    def tile(causal):
        add = None
        if causal:
            row = qi * tq + lax.broadcasted_iota(i32, (tq, tk), 0)
            col = ki * tk + lax.broadcasted_iota(i32, (tq, tk), 1)
            mask_scr[...] = jnp.where(row >= col, 0.0, NEG_INF)
            add = lambda h, rows, r0, kc: (mask_scr[rows, kc * LANES:(kc + 1) * LANES],)
        _flash_heads(hb, tq, tk, score, add, value, m_ref, l_ref, acc_ref, s_scr, p_scr, mx_scr, a_scr)

    pl.when((fl & FLAG_CAUSAL) != 0)(lambda: tile(True))
    pl.when((fl & FLAG_CAUSAL) == 0)(lambda: tile(False))

    @pl.when((fl & FLAG_LAST) != 0)
    def _():
        for h in range(hb):
            o_ref[:, h * HEAD_DIM:(h + 1) * HEAD_DIM] = (acc_ref[h] / l_ref[h]).astype(o_ref.dtype)


def _mla_attn(q, k, v, B, T, hb=8, tq=Q_TILE, tk=K_TILE):
    tk = min(tk, T)
    nq, nk = T // tq, T // tk
    tabs = _tile_schedule(nq, tq, tk, lambda qi: 0)
    grid_spec = pltpu.PrefetchScalarGridSpec(
        num_scalar_prefetch=3,
        grid=(B, MLA_HEADS // hb, int(tabs[0].shape[0])),
        in_specs=[pl.BlockSpec((tq, hb * MLA_SLOT), lambda b, h, s, qt, kt, ft: (b * nq + qt[s], h)),
                  pl.BlockSpec((tk, hb * MLA_SLOT), lambda b, h, s, qt, kt, ft: (b * nk + kt[s], h)),
                  pl.BlockSpec((tk, hb * HEAD_DIM), lambda b, h, s, qt, kt, ft: (b * nk + kt[s], h))],
        out_specs=pl.BlockSpec((tq, hb * HEAD_DIM), lambda b, h, s, qt, kt, ft: (b * nq + qt[s], h)),
        scratch_shapes=_flash_scratch(hb, tq, tk))
    return pl.pallas_call(
        functools.partial(_mla_attn_body, hb=hb, tq=tq, tk=tk),
        grid_spec=grid_spec,
        out_shape=jax.ShapeDtypeStruct((B * T, MLA_HEADS * HEAD_DIM), bf16),
        compiler_params=_params(("parallel", "parallel", "arbitrary")),
        name="mla_flash_attention",
    )(*tabs, q, k, v)


def _bias_table_body(rev_ref, o_ref):
    width = rev_ref.shape[-1]
    x = jnp.broadcast_to(rev_ref[0], (LANES, width))
    rolled = pltpu.roll(x, 0, 1, stride=1, stride_axis=0)
    for kb in range(width // LANES):
        o_ref[0, kb] = rolled[:, kb * LANES:(kb + 1) * LANES]


def _bias_table(rev):
    H, _, U = rev.shape
    return pl.pallas_call(
        _bias_table_body,
        grid=(H,),
        in_specs=[pl.BlockSpec((1, 1, U), lambda h: (h, 0, 0))],
        out_specs=pl.BlockSpec((1, U // LANES, LANES, LANES), lambda h: (h, 0, 0, 0)),
        out_shape=jax.ShapeDtypeStruct((H, U // LANES, LANES, LANES), f32),
        compiler_params=_params(("parallel",)),
        name="relative_bias_toeplitz_table",
    )(rev)


def _nsa_band_body(qi_tab, ki_tab, fl_tab, q_ref, k_ref, v_ref, g_ref, *rest, tq, tk, seq, select):
    if select:
        sel_ref, e_ref, o_ref, m_ref, l_ref, acc_ref, s_scr, p_scr, mx_scr, a_scr, mask_scr = rest
    else:
        o_ref, m_ref, l_ref, acc_ref, s_scr, p_scr, mx_scr, a_scr, mask_scr = rest
    st = pl.program_id(2)
    qi, ki, fl = qi_tab[st], ki_tab[st], fl_tab[st]

    pl.when((fl & FLAG_FIRST) != 0)(lambda: _flash_init(m_ref, l_ref, acc_ref))

    d0 = qi * tq - ki * tk
    dist = d0 + lax.broadcasted_iota(i32, (tq, tk), 0) - lax.broadcasted_iota(i32, (tq, tk), 1)
    if select:
        picked = jnp.dot(sel_ref[...], e_ref[...], preferred_element_type=f32)
        valid = (dist >= 0) & (picked > 0.5)
    else:
        valid = (dist >= 0) & (dist < WINDOW)
    mask_scr[...] = jnp.where(valid, 0.0, NEG_INF)
    k = k_ref[...].astype(bf16)
    v = _with_ones(v_ref[...].astype(bf16))
    kb0 = (seq - d0) // LANES

    def score(h):
        return lax.dot_general(q_ref[:, h * HEAD_DIM:(h + 1) * HEAD_DIM], k, NT, preferred_element_type=f32)

    def add(h, rows, r0, kc):
        bias = g_ref[h, kb0 - r0 // LANES + kc, pl.ds(r0 % LANES, ROW_CHUNK), :]
        return bias, mask_scr[rows, kc * LANES:(kc + 1) * LANES]

    _flash_heads(NSA_HPG, tq, tk, score, add, lambda h: v, m_ref, l_ref, acc_ref, s_scr, p_scr, mx_scr, a_scr)

    @pl.when((fl & FLAG_LAST) != 0)
    def _():
        for h in range(NSA_HPG):
            o_ref[:, h * HEAD_DIM:(h + 1) * HEAD_DIM] = (acc_ref[h] / l_ref[h]).astype(o_ref.dtype)


def _nsa_band(q, kv, gtab, B, T, kcol, vcol, tk, sel=None, emat=None, tq=Q_TILE):
    select = sel is not None
    nq, nk = T // tq, T // tk
    first = (lambda qi: 0) if select else (lambda qi: max(qi * tq - (WINDOW - 1), 0) // tk)
    tabs = _tile_schedule(nq, tq, tk, first)
    gw = NSA_HPG * HEAD_DIM
    in_specs = [
        pl.BlockSpec((tq, gw), lambda b, g, s, qt, kt, ft: (b * nq + qt[s], g)),
        pl.BlockSpec((tk, HEAD_DIM), lambda b, g, s, qt, kt, ft: (b * nk + kt[s], kcol * NSA_GROUPS + g)),
        pl.BlockSpec((tk, HEAD_DIM), lambda b, g, s, qt, kt, ft: (b * nk + kt[s], vcol * NSA_GROUPS + g)),
        pl.BlockSpec((NSA_HPG,) + gtab.shape[1:], lambda b, g, s, qt, kt, ft: (g, 0, 0, 0),
                     pipeline_mode=pl.Buffered(1)),
    ]
    args = [q, kv, kv, gtab]
    if select:
        in_specs += [pl.BlockSpec((tq, LANES), lambda b, g, s, qt, kt, ft: (b * nq + qt[s], g)),
                     pl.BlockSpec((LANES, tk), lambda b, g, s, qt, kt, ft: (0, kt[s]))]
        args += [sel, emat]
    grid_spec = pltpu.PrefetchScalarGridSpec(
        num_scalar_prefetch=3,
        grid=(B, NSA_GROUPS, int(tabs[0].shape[0])),
        in_specs=in_specs,
        out_specs=pl.BlockSpec((tq, gw), lambda b, g, s, qt, kt, ft: (b * nq + qt[s], g)),
        scratch_shapes=_flash_scratch(NSA_HPG, tq, tk))
    return pl.pallas_call(
        functools.partial(_nsa_band_body, tq=tq, tk=tk, seq=T, select=select),
        grid_spec=grid_spec,
        out_shape=jax.ShapeDtypeStruct((B * T, NSA_HEADS * HEAD_DIM), bf16),
        compiler_params=_params(("parallel", "parallel", "arbitrary")),
        name="nsa_selected_attention" if select else "nsa_window_attention",
    )(*tabs, *args)


def _compress_body(x_ref, pe_ref, w1_ref, w2_ref, o_ref, ot_ref, *, nc):
    half = CMP_BLOCK // 2
    h_lo = jnp.zeros((nc, CMP_HIDDEN), f32)
    h_hi = jnp.zeros((nc, CMP_HIDDEN), f32)
    for p in range(half):
        rows = x_ref[pl.ds(p, nc, stride=CMP_STRIDE), :]
        h_lo += jnp.dot((rows + pe_ref[0, p:p + 1, :]).astype(bf16), w1_ref[0, p], preferred_element_type=f32)
        h_hi += jnp.dot((rows + pe_ref[0, half + p:half + p + 1, :]).astype(bf16), w1_ref[0, half + p],
                        preferred_element_type=f32)
    hid = h_lo + pltpu.roll(h_hi, nc - 1, 0)
    out = jnp.dot(jax.nn.gelu(hid).astype(bf16), w2_ref[0], preferred_element_type=f32)
    out = jnp.where(lax.broadcasted_iota(i32, out.shape, 0) < nc - 1, out, 0.0)
    o_ref[0, 0, 0] = out.astype(bf16)
    ot_ref[0, 0, 0] = out.T.astype(bf16)


def _compress(kv, pe, w1, w2, B, T):
    nc = T // CMP_STRIDE
    return pl.pallas_call(
        functools.partial(_compress_body, nc=nc),
        grid=(2, B, NSA_GROUPS),
        in_specs=[pl.BlockSpec((T, HEAD_DIM), lambda c, b, g: (b, c * NSA_GROUPS + g)),
                  pl.BlockSpec((1, CMP_BLOCK, HEAD_DIM), lambda c, b, g: (c, 0, 0)),
                  pl.BlockSpec((1, CMP_BLOCK, HEAD_DIM, CMP_HIDDEN), lambda c, b, g: (c, 0, 0, 0)),
                  pl.BlockSpec((1, CMP_HIDDEN, HEAD_DIM), lambda c, b, g: (c, 0, 0))],
        out_specs=[pl.BlockSpec((1, 1, 1, nc, HEAD_DIM), lambda c, b, g: (c, b, g, 0, 0)),
                   pl.BlockSpec((1, 1, 1, HEAD_DIM, nc), lambda c, b, g: (c, b, g, 0, 0))],
        out_shape=[jax.ShapeDtypeStruct((2, B, NSA_GROUPS, nc, HEAD_DIM), bf16),
                   jax.ShapeDtypeStruct((2, B, NSA_GROUPS, HEAD_DIM, nc), bf16)],
        compiler_params=_params(("parallel", "parallel", "parallel")),
        name="nsa_compress_mlp",
    )(kv, pe, w1, w2)


def _cmp_body(q_ref, kc_ref, vct_ref, cw_ref, ovt_ref, o_ref, sel_ref, *, nc, nq):
    tq = CMP_TQ
    qi = pl.program_id(2)
    t0 = qi * tq
    key_row = lax.broadcasted_iota(i32, (nc, tq), 0)
    q_col = lax.broadcasted_iota(i32, (nc, tq), 1)
    valid = (t0 + q_col - CMP_STRIDE * key_row - (CMP_BLOCK - 1)) >= 0
    kc = kc_ref[0, 0, 0]
    vct = vct_ref[0, 0, 0]
    psum = jnp.zeros((nc, tq), f32)
    for h in range(NSA_HPG):
        pieces = []
        for c in range(nc // 8):
            win = jnp.broadcast_to(cw_ref[qi - c + nq - 1, h:h + 1, :], (8, 2 * tq))
            pieces.append(pltpu.roll(win, 0, 1, stride=CMP_STRIDE, stride_axis=0)[:, :tq])
        bias = jnp.concatenate(pieces, axis=0)
        s = lax.dot_general(kc, q_ref[:, h * HEAD_DIM:(h + 1) * HEAD_DIM], NT, preferred_element_type=f32)
        s = jnp.where(valid, s + bias, NEG_INF)
        m = jnp.max(s, axis=0, keepdims=True)
        e = jnp.where(valid, jnp.exp(s - m), 0.0)
        l = jnp.sum(e, axis=0, keepdims=True)
        p = e / jnp.where(l > 0.0, l, 1.0)
        psum += p
        o_t = jnp.dot(vct, p.astype(bf16), preferred_element_type=f32)
        o_ref[:, h * HEAD_DIM:(h + 1) * HEAD_DIM] = o_t.T.astype(o_ref.dtype)

    imp = jnp.dot(ovt_ref[...], psum, preferred_element_type=f32, precision=lax.Precision.HIGHEST)
    nrow = imp.shape[0]
    n_slc = (nc * CMP_STRIDE) // SLC_BLOCK
    j = lax.broadcasted_iota(i32, (nrow, tq), 0)
    t = t0 + lax.broadcasted_iota(i32, (nrow, tq), 1)
    cur = jnp.right_shift(t, 6)
    ok = j * SLC_BLOCK <= t
    forced = (j == 0) | (j == cur) | (j == cur - 1)
    score = jnp.where(forced, FORCED_SCORE, jnp.where(ok, imp, NEG_INF))
    score = jnp.where(j < n_slc, score, PAD_SCORE)
    chosen = jnp.zeros((nrow, tq), f32)
    for _ in range(min(SLC_TOPK, n_slc)):
        top = jnp.max(score, axis=0, keepdims=True)
        first = jnp.min(jnp.where(score == top, j, nrow), axis=0, keepdims=True)
        hit = j == first
        chosen = jnp.where(hit, 1.0, chosen)
        score = jnp.where(hit, PAD_SCORE, score)
    chosen = jnp.where(ok & (j < n_slc), chosen, 0.0)
    sel_ref[...] = chosen.T.astype(sel_ref.dtype)


def _nsa_cmp(q, kc, vct, cw, ovt, B, T):
    nc = T // CMP_STRIDE
    nq = T // CMP_TQ
    gw = NSA_HPG * HEAD_DIM
    return pl.pallas_call(
        functools.partial(_cmp_body, nc=nc, nq=nq),
        grid=(B, NSA_GROUPS, nq),
        in_specs=[pl.BlockSpec((CMP_TQ, gw), lambda b, g, qi: (b * nq + qi, g)),
                  pl.BlockSpec((1, 1, 1, nc, HEAD_DIM), lambda b, g, qi: (0, b, g, 0, 0)),
                  pl.BlockSpec((1, 1, 1, HEAD_DIM, nc), lambda b, g, qi: (1, b, g, 0, 0)),
                  pl.BlockSpec((2 * nq - 1, NSA_HPG, 2 * CMP_TQ), lambda b, g, qi: (0, g, 0)),
                  pl.BlockSpec(ovt.shape, lambda b, g, qi: (0, 0))],
        out_specs=[pl.BlockSpec((CMP_TQ, gw), lambda b, g, qi: (b * nq + qi, g)),
                   pl.BlockSpec((CMP_TQ, LANES), lambda b, g, qi: (b * nq + qi, g))],
        out_shape=[jax.ShapeDtypeStruct((B * T, NSA_HEADS * HEAD_DIM), bf16),
                   jax.ShapeDtypeStruct((B * T, NSA_GROUPS * LANES), bf16)],
        compiler_params=_params(("parallel", "parallel", "parallel")),
        name="nsa_compressed_attention_select",
    )(q, kc, vct, cw, ovt)


def _layer_norm(y, g, b):
    mu = jnp.mean(y, axis=-1, keepdims=True)
    var = jnp.mean(jnp.square(y - mu), axis=-1, keepdims=True)
    return (y - mu) * lax.rsqrt(var + LN_EPS) * g + b


def _gate_combine_body(oc_ref, os_ref, ow_ref, g_ref, o_ref):
    g = g_ref[...]
    for h in range(NSA_HEADS):
        sl = slice(h * HEAD_DIM, (h + 1) * HEAD_DIM)
        o = (g[:, h:h + 1] * oc_ref[:, sl].astype(f32)
             + g[:, NSA_HEADS + h:NSA_HEADS + h + 1] * os_ref[:, sl].astype(f32)
             + g[:, 2 * NSA_HEADS + h:2 * NSA_HEADS + h + 1] * ow_ref[:, sl].astype(f32))
        o_ref[:, sl] = o.astype(o_ref.dtype)


def _gate_combine(oc, osl, ow, gates, tm):
    M, W = oc.shape
    blk = pl.BlockSpec((tm, W), lambda i: (i, 0))
    return pl.pallas_call(
        _gate_combine_body,
        grid=(M // tm,),
        in_specs=[blk, blk, blk, pl.BlockSpec((tm, LANES), lambda i: (i, 0))],
        out_specs=blk,
        out_shape=jax.ShapeDtypeStruct((M, W), bf16),
        compiler_params=_params(("parallel",)),
        name="nsa_branch_gate_combine",
    )(oc, osl, ow, gates)


def _branch_merge_body(a_ref, b_ref, wa_ref, wb_ref, ga_ref, gb_ref, o_ref):
    ya = jnp.dot(a_ref[...], wa_ref[...], preferred_element_type=f32)
    yb = jnp.dot(b_ref[...], wb_ref[...], preferred_element_type=f32)
    o_ref[...] = (ga_ref[...] * ya + gb_ref[...] * yb).astype(o_ref.dtype)


def _branch_merge(a, b, wa, wb, mg, tm, tn):
    M, K = a.shape
    Nc = wa.shape[1]
    nj = Nc // tn
    return pl.pallas_call(
        _branch_merge_body,
        grid=(M // tm, nj),
        in_specs=[pl.BlockSpec((tm, K), lambda i, j: (i, 0)), pl.BlockSpec((tm, K), lambda i, j: (i, 0)),
                  pl.BlockSpec((K, tn), lambda i, j: (0, j)), pl.BlockSpec((K, tn), lambda i, j: (0, j)),
                  pl.BlockSpec((tm, tn), lambda i, j: (i, j)), pl.BlockSpec((tm, tn), lambda i, j: (i, nj + j))],
        out_specs=pl.BlockSpec((tm, tn), lambda i, j: (i, j)),
        out_shape=jax.ShapeDtypeStruct((M, Nc), bf16),
        compiler_params=_params(("parallel", "parallel")),
        name="branch_proj_merge",
    )(a, b, wa, wb, mg, mg)


def _proj_ln_body(z_ref, w_ref, x_ref, g_ref, b_ref, o_ref):
    y = jnp.dot(z_ref[...], w_ref[...], preferred_element_type=f32)
    o_ref[...] = _layer_norm(ALPHA * x_ref[...] + y, g_ref[...], b_ref[...])


def _proj_ln(z, w, x, g, b, tm):
    M, K = z.shape
    D = w.shape[1]
    vec = pl.BlockSpec((1, D), lambda i: (0, 0))
    return pl.pallas_call(
        _proj_ln_body,
        grid=(M // tm,),
        in_specs=[pl.BlockSpec((tm, K), lambda i: (i, 0)), pl.BlockSpec((K, D), lambda i: (0, 0)),
                  pl.BlockSpec((tm, D), lambda i: (i, 0)), vec, vec],
        out_specs=pl.BlockSpec((tm, D), lambda i: (i, 0)),
        out_shape=jax.ShapeDtypeStruct((M, D), f32),
        compiler_params=_params(("parallel",)),
        name="mix_out_proj_layernorm",
    )(z, w, x, g, b)


def _mem_attn_body(x_ref, wq_ref, k_ref, v_ref, wo_ref, g_ref, b_ref, o_ref, ob_ref):
    x = x_ref[...]
    q = (jnp.dot(x.astype(bf16), wq_ref[...], preferred_element_type=f32) * HEAD_DIM ** -0.5).astype(bf16)
    heads = []
    for h in range(MEM_HEADS):
        sl = slice(h * HEAD_DIM, (h + 1) * HEAD_DIM)
        s = lax.dot_general(q[:, sl], k_ref[0, :, sl], NT, preferred_element_type=f32)
        e = jnp.exp(s - jnp.max(s, axis=-1, keepdims=True))
        p = e / jnp.sum(e, axis=-1, keepdims=True)
        heads.append(jnp.dot(p.astype(bf16), v_ref[0, :, sl], preferred_element_type=f32).astype(bf16))
    y = jnp.dot(jnp.concatenate(heads, axis=1), wo_ref[...], preferred_element_type=f32)
    out = _layer_norm(ALPHA * x + y, g_ref[...], b_ref[...])
    o_ref[...] = out
    ob_ref[...] = out.astype(bf16)


def _mem_attn(x, wq, km, vm, wo, g, b, T, tm):
    M, D = x.shape
    nt = T // tm
    W = wq.shape[1]
    vec = pl.BlockSpec((1, D), lambda i: (0, 0))
    blk = pl.BlockSpec((tm, D), lambda i: (i, 0))
    return pl.pallas_call(
        _mem_attn_body,
        grid=(M // tm,),
        in_specs=[blk, pl.BlockSpec((D, W), lambda i: (0, 0)),
                  pl.BlockSpec((1, N_MEM, W), lambda i: (i // nt, 0, 0)),
                  pl.BlockSpec((1, N_MEM, W), lambda i: (i // nt, 0, 0)),
                  pl.BlockSpec((W, D), lambda i: (0, 0)), vec, vec],
        out_specs=[blk, blk],
        out_shape=[jax.ShapeDtypeStruct((M, D), f32), jax.ShapeDtypeStruct((M, D), bf16)],
        compiler_params=_params(("parallel",)),
        name="memory_cross_attention_layernorm",
    )(x, wq, km, vm, wo, g, b)


def _add_ln_body(x_ref, y_ref, g_ref, b_ref, o_ref):
    o_ref[...] = _layer_norm(ALPHA * x_ref[...] + y_ref[...], g_ref[...], b_ref[...])


def _add_ln(x, y, g, b, tm):
    M, D = x.shape
    vec = pl.BlockSpec((1, D), lambda i: (0, 0))
    blk = pl.BlockSpec((tm, D), lambda i: (i, 0))
    return pl.pallas_call(
        _add_ln_body,
        grid=(M // tm,),
        in_specs=[blk, blk, vec, vec],
        out_specs=blk,
        out_shape=jax.ShapeDtypeStruct((M, D), f32),
        compiler_params=_params(("parallel",)),
        name="residual_layernorm",
    )(x, y, g, b)


def _topk_rows(s, k):
    rows, n = s.shape
    idx = lax.broadcasted_iota(i32, s.shape, 0)
    slot = lax.broadcasted_iota(i32, (k, n), 0)
    vals = jnp.zeros((k, n), f32)
    inds = jnp.zeros((k, n), i32)
    for r in range(k):
        top = jnp.max(s, axis=0, keepdims=True)
        first = jnp.min(jnp.where(s == top, idx, rows), axis=0, keepdims=True)
        vals = jnp.where(slot == r, top, vals)
        inds = jnp.where(slot == r, first, inds)
        s = jnp.where(idx == first, PAD_SCORE, s)
    return vals, inds


def _pick_rows(table, sel):
    out = jnp.zeros(sel.shape, table.dtype)
    for a in range(table.shape[0]):
        out = jnp.where(sel == a, table[a:a + 1, :], out)
    return out


def _peer_route_body(q_ref, keys_ref, e1_ref, e2_ref, gt_ref, e1_scr, e2_scr, g_scr):
    k1 = keys_ref[0].astype(bf16)
    k2 = keys_ref[1].astype(bf16)
    for h in range(PEER_HEADS):
        q1 = q_ref[:, (2 * h) * LANES:(2 * h + 1) * LANES]
        q2 = q_ref[:, (2 * h + 1) * LANES:(2 * h + 2) * LANES]
        s1 = lax.dot_general(k1, q1, NT, preferred_element_type=f32)
        s2 = lax.dot_general(k2, q2, NT, preferred_element_type=f32)
        v1, i1 = _topk_rows(s1, PEER_TOPK)
        v2, i2 = _topk_rows(s2, PEER_TOPK)
        cand = jnp.concatenate([v1[a:a + 1, :] + v2 for a in range(PEER_TOPK)], axis=0)
        top_s, flat = _topk_rows(cand, PEER_TOPK)
        e = jnp.exp(top_s - top_s[0:1, :])
        rows = slice(h * PEER_TOPK, (h + 1) * PEER_TOPK)
        g_scr[rows, :] = e / jnp.sum(e, axis=0, keepdims=True)
        e1_scr[rows, :] = _pick_rows(i1, jnp.right_shift(flat, 4))
        e2_scr[rows, :] = _pick_rows(i2, jnp.bitwise_and(flat, PEER_TOPK - 1))
    e1_ref[...] = e1_scr[...].T
    e2_ref[...] = e2_scr[...].T
    gt_ref[...] = g_scr[...].T


def _peer_route(q, keys, tn=LANES):
    M, W = q.shape
    out = pl.BlockSpec((tn, PEER_HK), lambda i: (i, 0))
    return pl.pallas_call(
        _peer_route_body,
        grid=(M // tn,),
        in_specs=[pl.BlockSpec((tn, W), lambda i: (i, 0)), pl.BlockSpec(keys.shape, lambda i: (0, 0, 0))],
        out_specs=[out, out, out],
        out_shape=[jax.ShapeDtypeStruct((M, PEER_HK), i32), jax.ShapeDtypeStruct((M, PEER_HK), i32),
                   jax.ShapeDtypeStruct((M, PEER_HK), f32)],
        scratch_shapes=[pltpu.VMEM((PEER_HK, tn), i32), pltpu.VMEM((PEER_HK, tn), i32), pltpu.VMEM((PEER_HK, tn), f32)],
        compiler_params=_params(("parallel",)),
        name="peer_product_key_routing",
    )(q, keys)


def _peer_weights_body(e1_ref, e2_ref, g_ref, w_ref, *, tn):
    sub = lax.broadcasted_iota(i32, (PEER_N_KEYS, PEER_HK), 0)

    def one(n, carry):
        r1 = e1_ref[pl.ds(n, 1), :]
        r2 = e2_ref[pl.ds(n, 1), :]
        g = g_ref[pl.ds(n, 1), :]
        a_t = (sub == r1).astype(bf16)
        b_t = jnp.where(sub == r2, g, 0.0).astype(bf16)
        w_ref[n] = lax.dot_general(a_t, b_t, NT, preferred_element_type=f32).astype(w_ref.dtype)
        return carry

    lax.fori_loop(0, tn, one, 0)


def _peer_weights(e1, e2, g, tn=64):
    M = e1.shape[0]
    blk = pl.BlockSpec((tn, PEER_HK), lambda i: (i, 0))
    return pl.pallas_call(
        functools.partial(_peer_weights_body, tn=tn),
        grid=(M // tn,),
        in_specs=[blk, blk, blk],
        out_specs=pl.BlockSpec((tn, PEER_N_KEYS, PEER_N_KEYS), lambda i: (i, 0, 0)),
        out_shape=jax.ShapeDtypeStruct((M, PEER_N_KEYS, PEER_N_KEYS), bf16),
        compiler_params=_params(("parallel",)),
        name="peer_routing_weights",
    )(e1, e2, g)


def _peer_dense_body(x_ref, w_ref, u_ref, v_ref, o_ref, *, rows):
    @pl.when(pl.program_id(1) == 0)
    def _():
        o_ref[...] = jnp.zeros(o_ref.shape, f32)

    act = jax.nn.gelu(lax.dot_general(x_ref[...], u_ref[...], NT, preferred_element_type=f32))
    w = jnp.concatenate([w_ref[r] for r in range(rows)], axis=1).astype(f32)
    o_ref[...] += jnp.dot((w * act).astype(bf16), v_ref[...], preferred_element_type=f32)


def _peer_dense(xb, w, u, v, tm=1024, rows=4):
    M, D = xb.shape
    te = rows * PEER_N_KEYS
    return pl.pallas_call(
        functools.partial(_peer_dense_body, rows=rows),
        grid=(M // tm, PEER_N_KEYS // rows),
        in_specs=[pl.BlockSpec((tm, D), lambda i, e: (i, 0)),
                  pl.BlockSpec((rows, tm, PEER_N_KEYS), lambda i, e: (e, i, 0)),
                  pl.BlockSpec((te, D), lambda i, e: (e, 0)), pl.BlockSpec((te, D), lambda i, e: (e, 0))],
        out_specs=pl.BlockSpec((tm, D), lambda i, e: (i, 0)),
        out_shape=jax.ShapeDtypeStruct((M, D), f32),
        compiler_params=_params(("parallel", "arbitrary")),
        name="peer_dense_experts",
    )(xb, w, u, v)


def _t5_bucket(dist):
    dist = jnp.maximum(dist, 0)
    n_log = REL_BUCKETS - REL_MAX_EXACT
    large = REL_MAX_EXACT + (jnp.log(jnp.maximum(dist, 1).astype(f32) / REL_MAX_EXACT)
                             / math.log(REL_MAX_DIST / REL_MAX_EXACT) * n_log).astype(i32)
    large = jnp.minimum(large, REL_BUCKETS - 1)
    return jnp.where(dist < REL_MAX_EXACT, dist, large)


def _position_tables(rel_bias, T):
    by_dist = rel_bias[_t5_bucket(jnp.arange(T))]

    def lookup(d):
        return jnp.transpose(by_dist[jnp.clip(d, 0, T - 1)], (0, 2, 1)).astype(f32)

    rev = jnp.transpose(lookup((T - np.arange(T + K_TILE))[None, :]), (1, 0, 2))
    nq = T // CMP_TQ
    c = np.arange(2 * CMP_TQ)
    wrap = np.where(c < CMP_TQ, c, c - 2 * CMP_TQ)
    delta = np.arange(-(nq - 1), nq)
    cmpw = lookup(delta[:, None] * CMP_TQ - (CMP_BLOCK - 1) + wrap[None, :])
    return rev, cmpw


def _rope_slot_tables(T):
    inv = ROPE_THETA ** (-jnp.arange(0, MLA_ROPE, 2, dtype=f32) / MLA_ROPE)
    ang = jnp.arange(T, dtype=f32)[:, None] * inv[None, :]
    cos, sin = jnp.cos(ang), jnp.sin(ang)
    z = jnp.zeros_like(cos)
    ta = jnp.concatenate([cos, cos, z, z], axis=1)
    tb = jnp.concatenate([-sin, z, z, z], axis=1)
    tc = jnp.concatenate([z, sin, z, z], axis=1)
    return ta, tb, tc


def _layer(x, mem, rel_bias, w_in, cmp_k_pe, cmp_k_w1, cmp_k_w2, cmp_v_pe, cmp_v_w1, cmp_v_w2,
           mla_kv_norm, mla_w_uk, mla_w_uv, w_branch_nsa, w_branch_mla, w_mix_out, ln_mix_g, ln_mix_b,
           mem_w_q, mem_w_k, mem_w_v, mem_w_o, ln_mem_g, ln_mem_b,
           peer_w_query, peer_sub_keys, peer_u, peer_v, ln_ffn_g, ln_ffn_b, ropes):
    B, T, D = x.shape
    N = B * T
    x2d = x.reshape(N, D)
    xb = x2d.astype(bf16)
    row = lambda a: a.reshape(1, -1)

    w_nsa_q = w_in[:, :OFF_NSA_KV].astype(bf16)
    w_nsa_kv = w_in[:, OFF_NSA_KV:OFF_NSA_GATE].astype(bf16)
    w_gate = jnp.pad(w_in[:, OFF_NSA_GATE:OFF_MLA_Q], ((0, 0), (0, LANES - 3 * NSA_HEADS))).astype(bf16)
    w_mla_q = jnp.pad(w_in[:, OFF_MLA_Q:OFF_MLA_KV].reshape(D, MLA_HEADS, MLA_NOPE + MLA_ROPE),
                      ((0, 0), (0, 0), (0, MLA_SLOT - MLA_NOPE - MLA_ROPE))).reshape(D, MLA_HEADS * MLA_SLOT).astype(bf16)
    w_mla_kv = jnp.pad(w_in[:, OFF_MLA_KV:OFF_MERGE], ((0, 0), (0, LANES - MLA_ROPE))).astype(bf16)
    w_merge = w_in[:, OFF_MERGE:].astype(bf16)

    nsa_q = _mm(xb, w_nsa_q, bf16, 1024, 512, scale=HEAD_DIM ** -0.5, name="nsa_q_proj")
    nsa_kv = _mm(xb, w_nsa_kv, f32, 1024, 512, name="nsa_kv_proj")
    gates = _mm(xb, w_gate, f32, 1024, LANES, act="sigmoid", name="nsa_gate_proj")
    mla_q = _mla_q(xb, w_mla_q, ropes, T, 1024, 512)
    mla_ckv = _mm(xb, w_mla_kv, f32, 1024, MLA_KV_RANK + LANES, name="mla_kv_proj")
    merge = _mm(xb, w_merge, f32, 1024, 512, act="sigmoid", name="merge_gate_proj")

    slc_tk = min(K_TILE, T)
    rev_tab, cmp_tab = _position_tables(rel_bias, T)
    bias_tab = _bias_table(rev_tab)
    pe = jnp.stack([cmp_k_pe, cmp_v_pe])
    w1 = jnp.stack([cmp_k_w1, cmp_v_w1]).reshape(2, CMP_BLOCK, HEAD_DIM, CMP_HIDDEN).astype(bf16)
    w2 = jnp.stack([cmp_k_w2, cmp_v_w2]).astype(bf16)
    kvc, kvc_t = _compress(nsa_kv, pe, w1, w2, B, T)
    nc = T // CMP_STRIDE
    n_slc = T // SLC_BLOCK
    cs = np.arange(nc)[None, :] * CMP_STRIDE
    ss = np.arange(LANES)[:, None] * SLC_BLOCK
    ovt = ((cs < ss + SLC_BLOCK) & (cs + CMP_BLOCK > ss) & (np.arange(nc)[None, :] < nc - 1)
           & (np.arange(LANES)[:, None] < n_slc)).astype(np.float32)
    o_cmp, sel = _nsa_cmp(nsa_q, kvc, kvc_t, cmp_tab, jnp.asarray(ovt), B, T)
    emat = (np.arange(T)[None, :] // SLC_BLOCK == np.arange(LANES)[:, None]).astype(np.float32)
    o_slc = _nsa_band(nsa_q, nsa_kv, bias_tab, B, T, 2, 3, slc_tk, sel=sel, emat=jnp.asarray(emat, dtype=bf16))
    o_win = _nsa_band(nsa_q, nsa_kv, bias_tab, B, T, 4, 5, Q_TILE)
    o_nsa = _gate_combine(o_cmp, o_slc, o_win, gates, 512)

    mla_k, mla_v = _mla_kv(mla_ckv, row(mla_kv_norm),
                           jnp.asarray(mla_w_uk, bf16), jnp.asarray(mla_w_uv, bf16), ropes, T, 512)
    o_mla = _mla_attn(mla_q, mla_k, mla_v, B, T)

    z = _branch_merge(o_nsa, o_mla, w_branch_nsa.astype(bf16), w_branch_mla.astype(bf16), merge, 1024, 512)
    x1 = _proj_ln(z, w_mix_out.astype(bf16), x2d, row(ln_mix_g), row(ln_mix_b), 512)

    mem2d = mem.reshape(B * N_MEM, D).astype(bf16)
    km = _mm(mem2d, mem_w_k.astype(bf16), bf16, B * N_MEM, 256, name="mem_k_proj").reshape(B, N_MEM, -1)
    vm = _mm(mem2d, mem_w_v.astype(bf16), bf16, B * N_MEM, 256, name="mem_v_proj").reshape(B, N_MEM, -1)
    x2, x2b = _mem_attn(x1, mem_w_q.astype(bf16), km, vm, mem_w_o.astype(bf16), row(ln_mem_g), row(ln_mem_b), T, 512)

    pq = _mm(x2b, peer_w_query.astype(bf16), bf16, 1024, 512, name="peer_query_proj")
    e1, e2, gt = _peer_route(pq, peer_sub_keys)
    wts = jnp.transpose(_peer_weights(e1, e2, gt), (1, 0, 2))
    y = _peer_dense(x2b, wts, peer_u.astype(bf16), peer_v.astype(bf16))
    x3 = _add_ln(x2, y, row(ln_ffn_g), row(ln_ffn_b), 512)
    return x3.reshape(B, T, D)


def kernel(x, mem, rel_bias, w_in, cmp_k_pe, cmp_k_w1, cmp_k_w2, cmp_v_pe, cmp_v_w1, cmp_v_w2, mla_kv_norm, mla_w_uk, mla_w_uv, w_branch_nsa, w_branch_mla, w_mix_out, ln_mix_g, ln_mix_b, mem_w_q, mem_w_k, mem_w_v, mem_w_o, ln_mem_g, ln_mem_b, peer_w_query, peer_sub_keys, peer_u, peer_v, ln_ffn_g, ln_ffn_b):
    ropes = _rope_slot_tables(x.shape[1])
    for l in range(DEPTH):
        x = _layer(x, mem, rel_bias, w_in[l], cmp_k_pe[l], cmp_k_w1[l], cmp_k_w2[l], cmp_v_pe[l], cmp_v_w1[l],
                   cmp_v_w2[l], mla_kv_norm[l], mla_w_uk[l], mla_w_uv[l], w_branch_nsa[l], w_branch_mla[l],
                   w_mix_out[l], ln_mix_g[l], ln_mix_b[l], mem_w_q[l], mem_w_k[l], mem_w_v[l], mem_w_o[l],
                   ln_mem_g[l], ln_mem_b[l], peer_w_query[l], peer_sub_keys[l], peer_u[l], peer_v[l],
                   ln_ffn_g[l], ln_ffn_b[l], ropes)
    return x
```

```python
import functools
import math

import numpy as np
import jax
import jax.numpy as jnp
from jax import lax
from jax.experimental import pallas as pl
from jax.experimental.pallas import tpu as pltpu

f32 = jnp.float32
bf16 = jnp.bfloat16
i32 = jnp.int32

D_MODEL = 2048
N_MEM = 256
NSA_HEADS = 16
NSA_GROUPS = 2
NSA_HPG = NSA_HEADS // NSA_GROUPS
HEAD_DIM = 128
CMP_BLOCK = 32
CMP_STRIDE = 16
CMP_HIDDEN = 256
SLC_BLOCK = 64
SLC_TOPK = 16
WINDOW = 512
FORCED_SCORE = 1e4
MLA_HEADS = 16
MLA_NOPE = 128
MLA_ROPE = 64
MLA_KV_RANK = 512
MLA_SLOT = 256
ROPE_THETA = 10000.0
REL_BUCKETS = 32
REL_MAX_EXACT = 16
REL_MAX_DIST = 4096
MEM_HEADS = 4
PEER_HEADS = 8
PEER_N_KEYS = 128
PEER_TOPK = 16
PEER_HK = PEER_HEADS * PEER_TOPK
DEPTH = 1
ALPHA = (2.0 * DEPTH) ** 0.25
NEG_INF = -1e30
M_INIT = -1e29
PAD_SCORE = -3e38
PAD_ID = 1e9
LOG2E = math.log2(math.e)
LN_EPS = 1e-5
RMS_EPS = 1e-6

NSA_Q_COLS = NSA_HEADS * HEAD_DIM
NSA_KV_COLS = NSA_GROUPS * HEAD_DIM
OFF_NSA_KV = NSA_Q_COLS
OFF_NSA_GATE = OFF_NSA_KV + 6 * NSA_KV_COLS
OFF_MLA_Q = OFF_NSA_GATE + 3 * NSA_HEADS
OFF_MLA_KV = OFF_MLA_Q + MLA_HEADS * (MLA_NOPE + MLA_ROPE)
OFF_MERGE = OFF_MLA_KV + MLA_KV_RANK + MLA_ROPE

LANES = 128
Q_TILE = 512
K_TILE = 512
WIN_TILE = 256
ROW_CHUNK = 32
CMP_TQ = 128
VMEM_LIMIT = 56 * 1024 * 1024

NT = (((1,), (1,)), ((), ()))


def _params(sem):
    return pltpu.CompilerParams(dimension_semantics=sem, vmem_limit_bytes=VMEM_LIMIT)


def _mm_body(a_ref, b_ref, o_ref, *, scale, act):
    acc = jnp.dot(a_ref[...].astype(bf16), b_ref[...], preferred_element_type=f32)
    if scale != 1.0:
        acc = acc * scale
    if act == "sigmoid":
        acc = jax.nn.sigmoid(acc)
    o_ref[...] = acc.astype(o_ref.dtype)


def _mm(a, b, out_dtype, tm, tn, scale=1.0, act=None, name="mm"):
    M, K = a.shape
    Nc = b.shape[1]
    return pl.pallas_call(
        functools.partial(_mm_body, scale=scale, act=act),
        grid=(M // tm, Nc // tn),
        in_specs=[pl.BlockSpec((tm, K), lambda i, j: (i, 0)), pl.BlockSpec((K, tn), lambda i, j: (0, j))],
        out_specs=pl.BlockSpec((tm, tn), lambda i, j: (i, j)),
        out_shape=jax.ShapeDtypeStruct((M, Nc), out_dtype),
        compiler_params=_params(("parallel", "parallel")),
        name=name,
    )(a, b)


def _rope_slot(y, ta, tb, tc):
    return y * ta + pltpu.roll(y, LANES - 32, 1) * tb + pltpu.roll(y, 32, 1) * tc


def _mlaq_body(a_ref, b_ref, ta_ref, tb_ref, tc_ref, o_ref, *, scale, nslot):
    acc = jnp.dot(a_ref[...], b_ref[...], preferred_element_type=f32)
    ta, tb, tc = ta_ref[...], tb_ref[...], tc_ref[...]
    for s in range(nslot):
        lo = s * MLA_SLOT
        o_ref[:, lo:lo + MLA_NOPE] = (acc[:, lo:lo + MLA_NOPE] * scale).astype(o_ref.dtype)
        rot = _rope_slot(acc[:, lo + MLA_NOPE:lo + MLA_SLOT], ta, tb, tc)
        o_ref[:, lo + MLA_NOPE:lo + MLA_SLOT] = (rot * scale).astype(o_ref.dtype)


def _mla_q(xb, w, ropes, T, tm, tn):
    M, K = xb.shape
    Nc = w.shape[1]
    nt = T // tm
    tab = pl.BlockSpec((tm, LANES), lambda i, j: (i % nt, 0))
    return pl.pallas_call(
        functools.partial(_mlaq_body, scale=(MLA_NOPE + MLA_ROPE) ** -0.5 * LOG2E, nslot=tn // MLA_SLOT),
        grid=(M // tm, Nc // tn),
        in_specs=[pl.BlockSpec((tm, K), lambda i, j: (i, 0)), pl.BlockSpec((K, tn), lambda i, j: (0, j)), tab, tab, tab],
        out_specs=pl.BlockSpec((tm, tn), lambda i, j: (i, j)),
        out_shape=jax.ShapeDtypeStruct((M, Nc), bf16),
        compiler_params=_params(("parallel", "parallel")),
        name="mla_q_proj_rope",
    )(xb, w, *ropes)


def _mlakv_body(c_ref, g_ref, wuk_ref, wuv_ref, ta_ref, tb_ref, tc_ref, k_ref, v_ref):
    x = c_ref[...]
    c = x[:, :MLA_KV_RANK]
    ms = jnp.mean(c * c, axis=-1, keepdims=True)
    cn = (c * lax.rsqrt(ms + RMS_EPS) * g_ref[...]).astype(bf16)
    kpe = _rope_slot(x[:, MLA_KV_RANK:], ta_ref[...], tb_ref[...], tc_ref[...]).astype(bf16)
    kn = jnp.dot(cn, wuk_ref[...], preferred_element_type=f32).astype(bf16)
    v_ref[...] = jnp.dot(cn, wuv_ref[...], preferred_element_type=f32).astype(bf16)
    for h in range(MLA_HEADS):
        k_ref[:, h * MLA_SLOT:h * MLA_SLOT + MLA_NOPE] = kn[:, h * MLA_NOPE:(h + 1) * MLA_NOPE]
        k_ref[:, h * MLA_SLOT + MLA_NOPE:(h + 1) * MLA_SLOT] = kpe


def _mla_kv(ckv, g, wuk, wuv, ropes, T, tm):
    M, K = ckv.shape
    nt = T // tm
    tab = pl.BlockSpec((tm, LANES), lambda i: (i % nt, 0))
    full = lambda shape: pl.BlockSpec(shape, lambda i: (0,) * len(shape))
    return pl.pallas_call(
        _mlakv_body,
        grid=(M // tm,),
        in_specs=[pl.BlockSpec((tm, K), lambda i: (i, 0)), full(g.shape), full(wuk.shape), full(wuv.shape), tab, tab, tab],
        out_specs=[pl.BlockSpec((tm, MLA_HEADS * MLA_SLOT), lambda i: (i, 0)),
                   pl.BlockSpec((tm, MLA_HEADS * HEAD_DIM), lambda i: (i, 0))],
        out_shape=[jax.ShapeDtypeStruct((M, MLA_HEADS * MLA_SLOT), bf16),
                   jax.ShapeDtypeStruct((M, MLA_HEADS * HEAD_DIM), bf16)],
        compiler_params=_params(("parallel",)),
        name="mla_kv_prep",
    )(ckv, g, wuk, wuv, *ropes)


FLAG_FIRST, FLAG_LAST, FLAG_CAUSAL = 1, 2, 4


def _tile_schedule(nq, tq, tk, first_key_tile):
    qs, ks, fl = [], [], []
    for qi in range(nq):
        lo = first_key_tile(qi)
        hi = (qi * tq + tq - 1) // tk
        for ki in range(lo, hi + 1):
            qs.append(qi)
            ks.append(ki)
            fl.append((FLAG_FIRST if ki == lo else 0) | (FLAG_LAST if ki == hi else 0)
                      | (FLAG_CAUSAL if ki * tk + tk - 1 > qi * tq else 0))
    return tuple(jnp.asarray(np.array(a, np.int32)) for a in (qs, ks, fl))


def _flash_scratch(heads, tq, tk):
    stat = pltpu.VMEM((heads, tq, LANES), f32)
    return [stat, stat, pltpu.VMEM((heads, tq, HEAD_DIM), f32),
            pltpu.VMEM((3, tq, tk), f32), pltpu.VMEM((2, tq, tk), bf16), pltpu.VMEM((2, tq, LANES), f32),
            pltpu.VMEM((2, tq, LANES), f32), pltpu.VMEM((tq, tk), f32)]


def _flash_init(m_ref, l_ref, acc_ref):
    m_ref[...] = jnp.full(m_ref.shape, M_INIT, f32)
    l_ref[...] = jnp.zeros(l_ref.shape, f32)
    acc_ref[...] = jnp.zeros(acc_ref.shape, f32)


def _flash_heads(nheads, tq, tk, score_fn, add_fn, value_fn, m_ref, l_ref, acc_ref, s_scr, p_scr, mx_scr, a_scr):
    nkc = tk // LANES
    chunks = [(pl.ds(c * ROW_CHUNK, ROW_CHUNK), c * ROW_CHUNK) for c in range(tq // ROW_CHUNK)]

    def scores(h):
        s_scr[h % 3] = score_fn(h)

    def pass_a(h):
        s_ref = s_scr.at[h % 3]
        for rows, r0 in chunks:
            mx = None
            for kc in range(nkc):
                cols = slice(kc * LANES, (kc + 1) * LANES)
                s = s_ref[rows, cols]
                if add_fn is not None:
                    for term in add_fn(h, rows, r0, kc):
                        s = s + term
                    s_ref[rows, cols] = s
                mx = s if mx is None else jnp.maximum(mx, s)
            mx_scr[h % 2, rows, :] = mx
        m_prev = m_ref[h]
        m_new = jnp.maximum(m_prev, jnp.max(mx_scr[h % 2], axis=-1, keepdims=True))
        a_scr[h % 2] = jnp.exp2(m_prev - m_new)
        m_ref[h] = m_new

    def pass_b(h):
        s_ref = s_scr.at[h % 3]
        for rows, _ in chunks:
            m_rows = m_ref[h, rows, :]
            for kc in range(nkc):
                cols = slice(kc * LANES, (kc + 1) * LANES)
                p_scr[h % 2, rows, cols] = jnp.exp2(s_ref[rows, cols] - m_rows).astype(bf16)
        pv = jnp.dot(p_scr[h % 2], value_fn(h), preferred_element_type=f32)
        alpha = a_scr[h % 2]
        acc_ref[h] = alpha * acc_ref[h] + pv[:, :HEAD_DIM]
        l_ref[h] = alpha * l_ref[h] + pv[:, HEAD_DIM:]

    scores(0)
    for h in range(nheads):
        if h + 1 < nheads:
            scores(h + 1)
        pass_a(h)
        if h >= 1:
            pass_b(h - 1)
    pass_b(nheads - 1)


def _with_ones(v):
    return jnp.concatenate([v, jnp.ones((v.shape[0], LANES), v.dtype)], axis=1)


def _mla_attn_body(qi_tab, ki_tab, fl_tab, q_ref, k_ref, v_ref, o_ref, m_ref, l_ref, acc_ref,
                   s_scr, p_scr, mx_scr, a_scr, mask_scr, *, hb, tq, tk):
    st = pl.program_id(2)
    qi, ki, fl = qi_tab[st], ki_tab[st], fl_tab[st]

    pl.when((fl & FLAG_FIRST) != 0)(lambda: _flash_init(m_ref, l_ref, acc_ref))

    def score(h):
        qk = slice(h * MLA_SLOT, (h + 1) * MLA_SLOT)
        return lax.dot_general(q_ref[:, qk], k_ref[:, qk], NT, preferred_element_type=f32)

    def value(h):
        return _with_ones(v_ref[:, h * HEAD_DIM:(h + 1) * HEAD_DIM])

    def tile(causal):
        add = None
        if causal:
            row = qi * tq + lax.broadcasted_iota(i32, (tq, tk), 0)
            col = ki * tk + lax.broadcasted_iota(i32, (tq, tk), 1)
            mask_scr[...] = jnp.where(row >= col, 0.0, NEG_INF)
            add = lambda h, rows, r0, kc: (mask_scr[rows, kc * LANES:(kc + 1) * LANES],)
        _flash_heads(hb, tq, tk, score, add, value, m_ref, l_ref, acc_ref, s_scr, p_scr, mx_scr, a_scr)

    pl.when((fl & FLAG_CAUSAL) != 0)(lambda: tile(True))
    pl.when((fl & FLAG_CAUSAL) == 0)(lambda: tile(False))

    @pl.when((fl & FLAG_LAST) != 0)
    def _():
        for h in range(hb):
            o_ref[:, h * HEAD_DIM:(h + 1) * HEAD_DIM] = (acc_ref[h] / l_ref[h]).astype(o_ref.dtype)


def _mla_attn(q, k, v, B, T, hb=8, tq=Q_TILE, tk=K_TILE):
    tk = min(tk, T)
    nq, nk = T // tq, T // tk
    tabs = _tile_schedule(nq, tq, tk, lambda qi: 0)
    grid_spec = pltpu.PrefetchScalarGridSpec(
        num_scalar_prefetch=3,
        grid=(B, MLA_HEADS // hb, int(tabs[0].shape[0])),
        in_specs=[pl.BlockSpec((tq, hb * MLA_SLOT), lambda b, h, s, qt, kt, ft: (b * nq + qt[s], h)),
                  pl.BlockSpec((tk, hb * MLA_SLOT), lambda b, h, s, qt, kt, ft: (b * nk + kt[s], h)),
                  pl.BlockSpec((tk, hb * HEAD_DIM), lambda b, h, s, qt, kt, ft: (b * nk + kt[s], h))],
        out_specs=pl.BlockSpec((tq, hb * HEAD_DIM), lambda b, h, s, qt, kt, ft: (b * nq + qt[s], h)),
        scratch_shapes=_flash_scratch(hb, tq, tk))
    return pl.pallas_call(
        functools.partial(_mla_attn_body, hb=hb, tq=tq, tk=tk),
        grid_spec=grid_spec,
        out_shape=jax.ShapeDtypeStruct((B * T, MLA_HEADS * HEAD_DIM), bf16),
        compiler_params=_params(("parallel", "parallel", "arbitrary")),
        name="mla_flash_attention",
    )(*tabs, q, k, v)


def _bias_table_body(rev_ref, o_ref):
    width = rev_ref.shape[-1]
    x = jnp.broadcast_to(rev_ref[0], (LANES, width))
    rolled = pltpu.roll(x, 0, 1, stride=1, stride_axis=0)
    for kb in range(width // LANES):
        o_ref[0, kb] = rolled[:, kb * LANES:(kb + 1) * LANES]


def _bias_table(rev):
    H, _, U = rev.shape
    return pl.pallas_call(
        _bias_table_body,
        grid=(H,),
        in_specs=[pl.BlockSpec((1, 1, U), lambda h: (h, 0, 0))],
        out_specs=pl.BlockSpec((1, U // LANES, LANES, LANES), lambda h: (h, 0, 0, 0)),
        out_shape=jax.ShapeDtypeStruct((H, U // LANES, LANES, LANES), f32),
        compiler_params=_params(("parallel",)),
        name="relative_bias_toeplitz_table",
    )(rev)


def _nsa_band_body(qi_tab, ki_tab, fl_tab, q_ref, k_ref, v_ref, g_ref, *rest, tq, tk, seq, select):
    if select:
        sel_ref, e_ref, o_ref, m_ref, l_ref, acc_ref, s_scr, p_scr, mx_scr, a_scr, mask_scr = rest
    else:
        o_ref, m_ref, l_ref, acc_ref, s_scr, p_scr, mx_scr, a_scr, mask_scr = rest
    st = pl.program_id(2)
    qi, ki, fl = qi_tab[st], ki_tab[st], fl_tab[st]

    pl.when((fl & FLAG_FIRST) != 0)(lambda: _flash_init(m_ref, l_ref, acc_ref))

    d0 = qi * tq - ki * tk
    dist = d0 + lax.broadcasted_iota(i32, (tq, tk), 0) - lax.broadcasted_iota(i32, (tq, tk), 1)
    if select:
        picked = jnp.dot(sel_ref[...], e_ref[...], preferred_element_type=f32)
        valid = (dist >= 0) & (picked > 0.5)
    else:
        valid = (dist >= 0) & (dist < WINDOW)
    mask_scr[...] = jnp.where(valid, 0.0, NEG_INF)
    k = k_ref[...].astype(bf16)
    v = _with_ones(v_ref[...].astype(bf16))
    kb0 = (seq - d0) // LANES

    def score(h):
        return lax.dot_general(q_ref[:, h * HEAD_DIM:(h + 1) * HEAD_DIM], k, NT, preferred_element_type=f32)

    def add(h, rows, r0, kc):
        bias = g_ref[h, kb0 - r0 // LANES + kc, pl.ds(r0 % LANES, ROW_CHUNK), :]
        return bias, mask_scr[rows, kc * LANES:(kc + 1) * LANES]

    _flash_heads(NSA_HPG, tq, tk, score, add, lambda h: v, m_ref, l_ref, acc_ref, s_scr, p_scr, mx_scr, a_scr)

    @pl.when((fl & FLAG_LAST) != 0)
    def _():
        for h in range(NSA_HPG):
            o_ref[:, h * HEAD_DIM:(h + 1) * HEAD_DIM] = (acc_ref[h] / l_ref[h]).astype(o_ref.dtype)


def _nsa_band(q, kv, gtab, B, T, kcol, vcol, tk, sel=None, emat=None, tq=Q_TILE):
    select = sel is not None
    nq, nk = T // tq, T // tk
    first = (lambda qi: 0) if select else (lambda qi: max(qi * tq - (WINDOW - 1), 0) // tk)
    tabs = _tile_schedule(nq, tq, tk, first)
    gw = NSA_HPG * HEAD_DIM
    in_specs = [
        pl.BlockSpec((tq, gw), lambda b, g, s, qt, kt, ft: (b * nq + qt[s], g)),
        pl.BlockSpec((tk, HEAD_DIM), lambda b, g, s, qt, kt, ft: (b * nk + kt[s], kcol * NSA_GROUPS + g)),
        pl.BlockSpec((tk, HEAD_DIM), lambda b, g, s, qt, kt, ft: (b * nk + kt[s], vcol * NSA_GROUPS + g)),
        pl.BlockSpec((NSA_HPG,) + gtab.shape[1:], lambda b, g, s, qt, kt, ft: (g, 0, 0, 0),
                     pipeline_mode=pl.Buffered(1)),
    ]
    args = [q, kv, kv, gtab]
    if select:
        in_specs += [pl.BlockSpec((tq, LANES), lambda b, g, s, qt, kt, ft: (b * nq + qt[s], g)),
                     pl.BlockSpec((LANES, tk), lambda b, g, s, qt, kt, ft: (0, kt[s]))]
        args += [sel, emat]
    grid_spec = pltpu.PrefetchScalarGridSpec(
        num_scalar_prefetch=3,
        grid=(B, NSA_GROUPS, int(tabs[0].shape[0])),
        in_specs=in_specs,
        out_specs=pl.BlockSpec((tq, gw), lambda b, g, s, qt, kt, ft: (b * nq + qt[s], g)),
        scratch_shapes=_flash_scratch(NSA_HPG, tq, tk))
    return pl.pallas_call(
        functools.partial(_nsa_band_body, tq=tq, tk=tk, seq=T, select=select),
        grid_spec=grid_spec,
        out_shape=jax.ShapeDtypeStruct((B * T, NSA_HEADS * HEAD_DIM), bf16),
        compiler_params=_params(("parallel", "parallel", "arbitrary")),
        name="nsa_selected_attention" if select else "nsa_window_attention",
    )(*tabs, *args)


def _compress_body(x_ref, pe_ref, w1_ref, w2_ref, o_ref, ot_ref, *, nc):
    half = CMP_BLOCK // 2
    h_lo = jnp.zeros((nc, CMP_HIDDEN), f32)
    h_hi = jnp.zeros((nc, CMP_HIDDEN), f32)
    for p in range(half):
        rows = x_ref[pl.ds(p, nc, stride=CMP_STRIDE), :]
        h_lo += jnp.dot((rows + pe_ref[0, p:p + 1, :]).astype(bf16), w1_ref[0, p], preferred_element_type=f32)
        h_hi += jnp.dot((rows + pe_ref[0, half + p:half + p + 1, :]).astype(bf16), w1_ref[0, half + p],
                        preferred_element_type=f32)
    hid = h_lo + pltpu.roll(h_hi, nc - 1, 0)
    out = jnp.dot(jax.nn.gelu(hid).astype(bf16), w2_ref[0], preferred_element_type=f32)
    out = jnp.where(lax.broadcasted_iota(i32, out.shape, 0) < nc - 1, out, 0.0)
    o_ref[0, 0, 0] = out.astype(bf16)
    ot_ref[0, 0, 0] = out.T.astype(bf16)


def _compress(kv, pe, w1, w2, B, T):
    nc = T // CMP_STRIDE
    return pl.pallas_call(
        functools.partial(_compress_body, nc=nc),
        grid=(2, B, NSA_GROUPS),
        in_specs=[pl.BlockSpec((T, HEAD_DIM), lambda c, b, g: (b, c * NSA_GROUPS + g)),
                  pl.BlockSpec((1, CMP_BLOCK, HEAD_DIM), lambda c, b, g: (c, 0, 0)),
                  pl.BlockSpec((1, CMP_BLOCK, HEAD_DIM, CMP_HIDDEN), lambda c, b, g: (c, 0, 0, 0)),
                  pl.BlockSpec((1, CMP_HIDDEN, HEAD_DIM), lambda c, b, g: (c, 0, 0))],
        out_specs=[pl.BlockSpec((1, 1, 1, nc, HEAD_DIM), lambda c, b, g: (c, b, g, 0, 0)),
                   pl.BlockSpec((1, 1, 1, HEAD_DIM, nc), lambda c, b, g: (c, b, g, 0, 0))],
        out_shape=[jax.ShapeDtypeStruct((2, B, NSA_GROUPS, nc, HEAD_DIM), bf16),
                   jax.ShapeDtypeStruct((2, B, NSA_GROUPS, HEAD_DIM, nc), bf16)],
        compiler_params=_params(("parallel", "parallel", "parallel")),
        name="nsa_compress_mlp",
    )(kv, pe, w1, w2)


def _cmp_body(q_ref, kc_ref, vct_ref, cw_ref, ovt_ref, o_ref, sel_ref, *, nc, nq):
    tq = CMP_TQ
    qi = pl.program_id(2)
    t0 = qi * tq
    key_row = lax.broadcasted_iota(i32, (nc, tq), 0)
    q_col = lax.broadcasted_iota(i32, (nc, tq), 1)
    valid = (t0 + q_col - CMP_STRIDE * key_row - (CMP_BLOCK - 1)) >= 0
    kc = kc_ref[0, 0, 0]
    vct = vct_ref[0, 0, 0]
    psum = jnp.zeros((nc, tq), f32)
    for h in range(NSA_HPG):
        pieces = []
        for c in range(nc // 8):
            win = jnp.broadcast_to(cw_ref[qi - c + nq - 1, h:h + 1, :], (8, 2 * tq))
            pieces.append(pltpu.roll(win, 0, 1, stride=CMP_STRIDE, stride_axis=0)[:, :tq])
        bias = jnp.concatenate(pieces, axis=0)
        s = lax.dot_general(kc, q_ref[:, h * HEAD_DIM:(h + 1) * HEAD_DIM], NT, preferred_element_type=f32)
        s = jnp.where(valid, s + bias, NEG_INF)
        m = jnp.max(s, axis=0, keepdims=True)
        e = jnp.where(valid, jnp.exp2(s - m), 0.0)
        l = jnp.sum(e, axis=0, keepdims=True)
        p = e / jnp.where(l > 0.0, l, 1.0)
        psum += p
        o_t = jnp.dot(vct, p.astype(bf16), preferred_element_type=f32)
        o_ref[:, h * HEAD_DIM:(h + 1) * HEAD_DIM] = o_t.T.astype(o_ref.dtype)

    imp = jnp.dot(ovt_ref[...], psum, preferred_element_type=f32, precision=lax.Precision.HIGHEST)
    nrow = imp.shape[0]
    n_slc = (nc * CMP_STRIDE) // SLC_BLOCK
    j = lax.broadcasted_iota(i32, (nrow, tq), 0)
    t = t0 + lax.broadcasted_iota(i32, (nrow, tq), 1)
    cur = jnp.right_shift(t, 6)
    ok = j * SLC_BLOCK <= t
    forced = (j == 0) | (j == cur) | (j == cur - 1)
    score = jnp.where(forced, FORCED_SCORE, jnp.where(ok, imp, NEG_INF))
    score = jnp.where(j < n_slc, score, PAD_SCORE)
    chosen = jnp.zeros((nrow, tq), f32)
    for _ in range(min(SLC_TOPK, n_slc)):
        top = jnp.max(score, axis=0, keepdims=True)
        first = jnp.min(jnp.where(score == top, j, nrow), axis=0, keepdims=True)
        hit = j == first
        chosen = jnp.where(hit, 1.0, chosen)
        score = jnp.where(hit, PAD_SCORE, score)
    chosen = jnp.where(ok & (j < n_slc), chosen, 0.0)
    sel_ref[...] = chosen.T.astype(sel_ref.dtype)


def _nsa_cmp(q, kc, vct, cw, ovt, B, T):
    nc = T // CMP_STRIDE
    nq = T // CMP_TQ
    gw = NSA_HPG * HEAD_DIM
    return pl.pallas_call(
        functools.partial(_cmp_body, nc=nc, nq=nq),
        grid=(B, NSA_GROUPS, nq),
        in_specs=[pl.BlockSpec((CMP_TQ, gw), lambda b, g, qi: (b * nq + qi, g)),
                  pl.BlockSpec((1, 1, 1, nc, HEAD_DIM), lambda b, g, qi: (0, b, g, 0, 0)),
                  pl.BlockSpec((1, 1, 1, HEAD_DIM, nc), lambda b, g, qi: (1, b, g, 0, 0)),
                  pl.BlockSpec((2 * nq - 1, NSA_HPG, 2 * CMP_TQ), lambda b, g, qi: (0, g, 0)),
                  pl.BlockSpec(ovt.shape, lambda b, g, qi: (0, 0))],
        out_specs=[pl.BlockSpec((CMP_TQ, gw), lambda b, g, qi: (b * nq + qi, g)),
                   pl.BlockSpec((CMP_TQ, LANES), lambda b, g, qi: (b * nq + qi, g))],
        out_shape=[jax.ShapeDtypeStruct((B * T, NSA_HEADS * HEAD_DIM), bf16),
                   jax.ShapeDtypeStruct((B * T, NSA_GROUPS * LANES), bf16)],
        compiler_params=_params(("parallel", "parallel", "parallel")),
        name="nsa_compressed_attention_select",
    )(q, kc, vct, cw, ovt)


def _layer_norm(y, g, b):
    mu = jnp.mean(y, axis=-1, keepdims=True)
    var = jnp.mean(jnp.square(y - mu), axis=-1, keepdims=True)
    return (y - mu) * lax.rsqrt(var + LN_EPS) * g + b


def _gate_combine_body(oc_ref, os_ref, ow_ref, g_ref, o_ref):
    g = g_ref[...]
    for h in range(NSA_HEADS):
        sl = slice(h * HEAD_DIM, (h + 1) * HEAD_DIM)
        o = (g[:, h:h + 1] * oc_ref[:, sl].astype(f32)
             + g[:, NSA_HEADS + h:NSA_HEADS + h + 1] * os_ref[:, sl].astype(f32)
             + g[:, 2 * NSA_HEADS + h:2 * NSA_HEADS + h + 1] * ow_ref[:, sl].astype(f32))
        o_ref[:, sl] = o.astype(o_ref.dtype)


def _gate_combine(oc, osl, ow, gates, tm):
    M, W = oc.shape
    blk = pl.BlockSpec((tm, W), lambda i: (i, 0))
    return pl.pallas_call(
        _gate_combine_body,
        grid=(M // tm,),
        in_specs=[blk, blk, blk, pl.BlockSpec((tm, LANES), lambda i: (i, 0))],
        out_specs=blk,
        out_shape=jax.ShapeDtypeStruct((M, W), bf16),
        compiler_params=_params(("parallel",)),
        name="nsa_branch_gate_combine",
    )(oc, osl, ow, gates)


def _branch_merge_body(a_ref, b_ref, wa_ref, wb_ref, ga_ref, gb_ref, o_ref):
    ya = jnp.dot(a_ref[...], wa_ref[...], preferred_element_type=f32)
    yb = jnp.dot(b_ref[...], wb_ref[...], preferred_element_type=f32)
    o_ref[...] = (ga_ref[...] * ya + gb_ref[...] * yb).astype(o_ref.dtype)


def _branch_merge(a, b, wa, wb, mg, tm, tn):
    M, K = a.shape
    Nc = wa.shape[1]
    nj = Nc // tn
    return pl.pallas_call(
        _branch_merge_body,
        grid=(M // tm, nj),
        in_specs=[pl.BlockSpec((tm, K), lambda i, j: (i, 0)), pl.BlockSpec((tm, K), lambda i, j: (i, 0)),
                  pl.BlockSpec((K, tn), lambda i, j: (0, j)), pl.BlockSpec((K, tn), lambda i, j: (0, j)),
                  pl.BlockSpec((tm, tn), lambda i, j: (i, j)), pl.BlockSpec((tm, tn), lambda i, j: (i, nj + j))],
        out_specs=pl.BlockSpec((tm, tn), lambda i, j: (i, j)),
        out_shape=jax.ShapeDtypeStruct((M, Nc), bf16),
        compiler_params=_params(("parallel", "parallel")),
        name="branch_proj_merge",
    )(a, b, wa, wb, mg, mg)


def _proj_ln_body(z_ref, w_ref, x_ref, g_ref, b_ref, o_ref):
    y = jnp.dot(z_ref[...], w_ref[...], preferred_element_type=f32)
    o_ref[...] = _layer_norm(ALPHA * x_ref[...] + y, g_ref[...], b_ref[...])


def _proj_ln(z, w, x, g, b, tm):
    M, K = z.shape
    D = w.shape[1]
    vec = pl.BlockSpec((1, D), lambda i: (0, 0))
    return pl.pallas_call(
        _proj_ln_body,
        grid=(M // tm,),
        in_specs=[pl.BlockSpec((tm, K), lambda i: (i, 0)), pl.BlockSpec((K, D), lambda i: (0, 0)),
                  pl.BlockSpec((tm, D), lambda i: (i, 0)), vec, vec],
        out_specs=pl.BlockSpec((tm, D), lambda i: (i, 0)),
        out_shape=jax.ShapeDtypeStruct((M, D), f32),
        compiler_params=_params(("parallel",)),
        name="mix_out_proj_layernorm",
    )(z, w, x, g, b)


def _mem_attn_body(x_ref, wq_ref, k_ref, v_ref, wo_ref, g_ref, b_ref, o_ref, ob_ref):
    x = x_ref[...]
    q = (jnp.dot(x.astype(bf16), wq_ref[...], preferred_element_type=f32) * HEAD_DIM ** -0.5).astype(bf16)
    heads = []
    for h in range(MEM_HEADS):
        sl = slice(h * HEAD_DIM, (h + 1) * HEAD_DIM)
        s = lax.dot_general(q[:, sl], k_ref[0, :, sl], NT, preferred_element_type=f32)
        e = jnp.exp(s - jnp.max(s, axis=-1, keepdims=True))
        p = e / jnp.sum(e, axis=-1, keepdims=True)
        heads.append(jnp.dot(p.astype(bf16), v_ref[0, :, sl], preferred_element_type=f32).astype(bf16))
    y = jnp.dot(jnp.concatenate(heads, axis=1), wo_ref[...], preferred_element_type=f32)
    out = _layer_norm(ALPHA * x + y, g_ref[...], b_ref[...])
    o_ref[...] = out
    ob_ref[...] = out.astype(bf16)


def _mem_attn(x, wq, km, vm, wo, g, b, T, tm):
    M, D = x.shape
    nt = T // tm
    W = wq.shape[1]
    vec = pl.BlockSpec((1, D), lambda i: (0, 0))
    blk = pl.BlockSpec((tm, D), lambda i: (i, 0))
    return pl.pallas_call(
        _mem_attn_body,
        grid=(M // tm,),
        in_specs=[blk, pl.BlockSpec((D, W), lambda i: (0, 0)),
                  pl.BlockSpec((1, N_MEM, W), lambda i: (i // nt, 0, 0)),
                  pl.BlockSpec((1, N_MEM, W), lambda i: (i // nt, 0, 0)),
                  pl.BlockSpec((W, D), lambda i: (0, 0)), vec, vec],
        out_specs=[blk, blk],
        out_shape=[jax.ShapeDtypeStruct((M, D), f32), jax.ShapeDtypeStruct((M, D), bf16)],
        compiler_params=_params(("parallel",)),
        name="memory_cross_attention_layernorm",
    )(x, wq, km, vm, wo, g, b)


def _add_ln_body(x_ref, y_ref, g_ref, b_ref, o_ref):
    o_ref[...] = _layer_norm(ALPHA * x_ref[...] + y_ref[...], g_ref[...], b_ref[...])


def _add_ln(x, y, g, b, tm):
    M, D = x.shape
    vec = pl.BlockSpec((1, D), lambda i: (0, 0))
    blk = pl.BlockSpec((tm, D), lambda i: (i, 0))
    return pl.pallas_call(
        _add_ln_body,
        grid=(M // tm,),
        in_specs=[blk, blk, vec, vec],
        out_specs=blk,
        out_shape=jax.ShapeDtypeStruct((M, D), f32),
        compiler_params=_params(("parallel",)),
        name="residual_layernorm",
    )(x, y, g, b)


def _topk_rows(s, ids, k):
    n = s.shape[1]
    slot = lax.broadcasted_iota(i32, (k, n), 0)
    vals = jnp.zeros((k, n), f32)
    inds = jnp.zeros((k, n), f32)
    for r in range(k):
        top = jnp.max(s, axis=0, keepdims=True)
        first = jnp.min(jnp.where(s == top, ids, PAD_ID), axis=0, keepdims=True)
        vals = jnp.where(slot == r, top, vals)
        inds = jnp.where(slot == r, first, inds)
        s = jnp.where(ids == first, PAD_SCORE, s)
    return vals, inds


def _pick_rows(table, sel):
    out = jnp.zeros(sel.shape, table.dtype)
    for a in range(table.shape[0]):
        out = jnp.where(sel == a, table[a:a + 1, :], out)
    return out


CAND_COUNTS = tuple(PEER_TOPK // (a + 1) for a in range(PEER_TOPK))
CAND_ROWS = -(-sum(CAND_COUNTS) // 8) * 8


def _cand_ids(tn):
    ids = [a * PEER_TOPK + b for a, nb in enumerate(CAND_COUNTS) for b in range(nb)]
    ids = np.array(ids + [PAD_ID] * (CAND_ROWS - len(ids)), np.float32)
    return jnp.asarray(np.broadcast_to(ids[:, None], (CAND_ROWS, tn)))


def _peer_route_body(q_ref, keys_ref, ids_ref, e1_ref, e2_ref, gt_ref, e1_scr, e2_scr, g_scr, cand_scr):
    k1 = keys_ref[0].astype(bf16)
    k2 = keys_ref[1].astype(bf16)
    tn = q_ref.shape[0]
    key_ids = lax.broadcasted_iota(i32, (PEER_N_KEYS, tn), 0).astype(f32)
    cand_scr[...] = jnp.full(cand_scr.shape, PAD_SCORE, f32)
    for h in range(PEER_HEADS):
        q1 = q_ref[:, (2 * h) * LANES:(2 * h + 1) * LANES]
        q2 = q_ref[:, (2 * h + 1) * LANES:(2 * h + 2) * LANES]
        s1 = lax.dot_general(k1, q1, NT, preferred_element_type=f32)
        s2 = lax.dot_general(k2, q2, NT, preferred_element_type=f32)
        v1, i1 = _topk_rows(s1, key_ids, PEER_TOPK)
        v2, i2 = _topk_rows(s2, key_ids, PEER_TOPK)
        off = 0
        for a, nb in enumerate(CAND_COUNTS):
            cand_scr[off:off + nb, :] = v1[a:a + 1, :] + v2[0:nb, :]
            off += nb
        top_s, flat = _topk_rows(cand_scr[...], ids_ref[...], PEER_TOPK)
        flat = flat.astype(i32)
        e = jnp.exp(top_s - top_s[0:1, :])
        rows = slice(h * PEER_TOPK, (h + 1) * PEER_TOPK)
        g_scr[rows, :] = e / jnp.sum(e, axis=0, keepdims=True)
        e1_scr[rows, :] = _pick_rows(i1, jnp.right_shift(flat, 4)).astype(i32)
        e2_scr[rows, :] = _pick_rows(i2, jnp.bitwise_and(flat, PEER_TOPK - 1)).astype(i32)
    e1_ref[...] = e1_scr[...].T
    e2_ref[...] = e2_scr[...].T
    gt_ref[...] = g_scr[...].T


def _peer_route(q, keys, tn=LANES):
    M, W = q.shape
    out = pl.BlockSpec((tn, PEER_HK), lambda i: (i, 0))
    return pl.pallas_call(
        _peer_route_body,
        grid=(M // tn,),
        in_specs=[pl.BlockSpec((tn, W), lambda i: (i, 0)), pl.BlockSpec(keys.shape, lambda i: (0, 0, 0)),
                  pl.BlockSpec((CAND_ROWS, tn), lambda i: (0, 0))],
        out_specs=[out, out, out],
        out_shape=[jax.ShapeDtypeStruct((M, PEER_HK), i32), jax.ShapeDtypeStruct((M, PEER_HK), i32),
                   jax.ShapeDtypeStruct((M, PEER_HK), f32)],
        scratch_shapes=[pltpu.VMEM((PEER_HK, tn), i32), pltpu.VMEM((PEER_HK, tn), i32), pltpu.VMEM((PEER_HK, tn), f32),
                        pltpu.VMEM((CAND_ROWS, tn), f32)],
        compiler_params=_params(("parallel",)),
        name="peer_product_key_routing",
    )(q, keys, _cand_ids(tn))


def _peer_weights_body(e1_ref, e2_ref, g_ref, w_ref, *, tn):
    sub = lax.broadcasted_iota(i32, (PEER_N_KEYS, PEER_HK), 0)

    def one(n, carry):
        r1 = e1_ref[pl.ds(n, 1), :]
        r2 = e2_ref[pl.ds(n, 1), :]
        g = g_ref[pl.ds(n, 1), :]
        a_t = (sub == r1).astype(bf16)
        b_t = jnp.where(sub == r2, g, 0.0).astype(bf16)
        w_ref[n] = lax.dot_general(a_t, b_t, NT, preferred_element_type=f32).astype(w_ref.dtype)
        return carry

    lax.fori_loop(0, tn, one, 0, unroll=8)


def _peer_weights(e1, e2, g, tn=64):
    M = e1.shape[0]
    blk = pl.BlockSpec((tn, PEER_HK), lambda i: (i, 0))
    return pl.pallas_call(
        functools.partial(_peer_weights_body, tn=tn),
        grid=(M // tn,),
        in_specs=[blk, blk, blk],
        out_specs=pl.BlockSpec((tn, PEER_N_KEYS, PEER_N_KEYS), lambda i: (i, 0, 0)),
        out_shape=jax.ShapeDtypeStruct((M, PEER_N_KEYS, PEER_N_KEYS), bf16),
        compiler_params=_params(("parallel",)),
        name="peer_routing_weights",
    )(e1, e2, g)


def _peer_dense_body(x_ref, w_ref, u_ref, v_ref, o_ref, *, rows):
    @pl.when(pl.program_id(1) == 0)
    def _():
        o_ref[...] = jnp.zeros(o_ref.shape, f32)

    act = jax.nn.gelu(lax.dot_general(x_ref[...], u_ref[...].astype(bf16), NT, preferred_element_type=f32))
    w = jnp.concatenate([w_ref[r] for r in range(rows)], axis=1).astype(f32)
    o_ref[...] += jnp.dot((w * act).astype(bf16), v_ref[...].astype(bf16), preferred_element_type=f32)


def _peer_dense(xb, w, u, v, tm=1024, rows=4):
    M, D = xb.shape
    te = rows * PEER_N_KEYS
    return pl.pallas_call(
        functools.partial(_peer_dense_body, rows=rows),
        grid=(M // tm, PEER_N_KEYS // rows),
        in_specs=[pl.BlockSpec((tm, D), lambda i, e: (i, 0)),
                  pl.BlockSpec((rows, tm, PEER_N_KEYS), lambda i, e: (e, i, 0)),
                  pl.BlockSpec((te, D), lambda i, e: (e, 0)), pl.BlockSpec((te, D), lambda i, e: (e, 0))],
        out_specs=pl.BlockSpec((tm, D), lambda i, e: (i, 0)),
        out_shape=jax.ShapeDtypeStruct((M, D), f32),
        compiler_params=_params(("parallel", "arbitrary")),
        name="peer_dense_experts",
    )(xb, w, u, v)


def _t5_bucket(dist):
    dist = jnp.maximum(dist, 0)
    n_log = REL_BUCKETS - REL_MAX_EXACT
    large = REL_MAX_EXACT + (jnp.log(jnp.maximum(dist, 1).astype(f32) / REL_MAX_EXACT)
                             / math.log(REL_MAX_DIST / REL_MAX_EXACT) * n_log).astype(i32)
    large = jnp.minimum(large, REL_BUCKETS - 1)
    return jnp.where(dist < REL_MAX_EXACT, dist, large)


def _position_tables(rel_bias, T):
    by_dist = rel_bias[_t5_bucket(jnp.arange(T))] * LOG2E

    def lookup(d):
        return jnp.transpose(by_dist[jnp.clip(d, 0, T - 1)], (0, 2, 1)).astype(f32)

    rev = jnp.transpose(lookup((T - np.arange(T + K_TILE))[None, :]), (1, 0, 2))
    nq = T // CMP_TQ
    c = np.arange(2 * CMP_TQ)
    wrap = np.where(c < CMP_TQ, c, c - 2 * CMP_TQ)
    delta = np.arange(-(nq - 1), nq)
    cmpw = lookup(delta[:, None] * CMP_TQ - (CMP_BLOCK - 1) + wrap[None, :])
    return rev, cmpw


def _rope_slot_tables(T):
    inv = ROPE_THETA ** (-jnp.arange(0, MLA_ROPE, 2, dtype=f32) / MLA_ROPE)
    ang = jnp.arange(T, dtype=f32)[:, None] * inv[None, :]
    cos, sin = jnp.cos(ang), jnp.sin(ang)
    z = jnp.zeros_like(cos)
    ta = jnp.concatenate([cos, cos, z, z], axis=1)
    tb = jnp.concatenate([-sin, z, z, z], axis=1)
    tc = jnp.concatenate([z, sin, z, z], axis=1)
    return ta, tb, tc


def _layer(x, mem, rel_bias, w_in, cmp_k_pe, cmp_k_w1, cmp_k_w2, cmp_v_pe, cmp_v_w1, cmp_v_w2,
           mla_kv_norm, mla_w_uk, mla_w_uv, w_branch_nsa, w_branch_mla, w_mix_out, ln_mix_g, ln_mix_b,
           mem_w_q, mem_w_k, mem_w_v, mem_w_o, ln_mem_g, ln_mem_b,
           peer_w_query, peer_sub_keys, peer_u, peer_v, ln_ffn_g, ln_ffn_b, ropes):
    B, T, D = x.shape
    N = B * T
    x2d = x.reshape(N, D)
    xb = x2d.astype(bf16)
    row = lambda a: a.reshape(1, -1)

    w_nsa_q = w_in[:, :OFF_NSA_KV].astype(bf16)
    w_nsa_kv = w_in[:, OFF_NSA_KV:OFF_NSA_GATE].astype(bf16)
    w_gate = jnp.pad(w_in[:, OFF_NSA_GATE:OFF_MLA_Q], ((0, 0), (0, LANES - 3 * NSA_HEADS))).astype(bf16)
    w_mla_q = jnp.pad(w_in[:, OFF_MLA_Q:OFF_MLA_KV].reshape(D, MLA_HEADS, MLA_NOPE + MLA_ROPE),
                      ((0, 0), (0, 0), (0, MLA_SLOT - MLA_NOPE - MLA_ROPE))).reshape(D, MLA_HEADS * MLA_SLOT).astype(bf16)
    w_mla_kv = jnp.pad(w_in[:, OFF_MLA_KV:OFF_MERGE], ((0, 0), (0, LANES - MLA_ROPE))).astype(bf16)
    w_merge = w_in[:, OFF_MERGE:].astype(bf16)

    nsa_q = _mm(xb, w_nsa_q, bf16, 1024, 512, scale=HEAD_DIM ** -0.5 * LOG2E, name="nsa_q_proj")
    nsa_kv = _mm(xb, w_nsa_kv, f32, 1024, 512, name="nsa_kv_proj")
    gates = _mm(xb, w_gate, f32, 1024, LANES, act="sigmoid", name="nsa_gate_proj")
    mla_q = _mla_q(xb, w_mla_q, ropes, T, 1024, 512)
    mla_ckv = _mm(xb, w_mla_kv, f32, 1024, MLA_KV_RANK + LANES, name="mla_kv_proj")
    merge = _mm(xb, w_merge, f32, 1024, 512, act="sigmoid", name="merge_gate_proj")

    slc_tk = min(K_TILE, T)
    rev_tab, cmp_tab = _position_tables(rel_bias, T)
    bias_tab = _bias_table(rev_tab)
    pe = jnp.stack([cmp_k_pe, cmp_v_pe])
    w1 = jnp.stack([cmp_k_w1, cmp_v_w1]).reshape(2, CMP_BLOCK, HEAD_DIM, CMP_HIDDEN).astype(bf16)
    w2 = jnp.stack([cmp_k_w2, cmp_v_w2]).astype(bf16)
    kvc, kvc_t = _compress(nsa_kv, pe, w1, w2, B, T)
    nc = T // CMP_STRIDE
    n_slc = T // SLC_BLOCK
    cs = np.arange(nc)[None, :] * CMP_STRIDE
    ss = np.arange(LANES)[:, None] * SLC_BLOCK
    ovt = ((cs < ss + SLC_BLOCK) & (cs + CMP_BLOCK > ss) & (np.arange(nc)[None, :] < nc - 1)
           & (np.arange(LANES)[:, None] < n_slc)).astype(np.float32)
    o_cmp, sel = _nsa_cmp(nsa_q, kvc, kvc_t, cmp_tab, jnp.asarray(ovt), B, T)
    emat = (np.arange(T)[None, :] // SLC_BLOCK == np.arange(LANES)[:, None]).astype(np.float32)
    o_slc = _nsa_band(nsa_q, nsa_kv, bias_tab, B, T, 2, 3, slc_tk, sel=sel, emat=jnp.asarray(emat, dtype=bf16))
    o_win = _nsa_band(nsa_q, nsa_kv, bias_tab, B, T, 4, 5, WIN_TILE, tq=WIN_TILE)
    o_nsa = _gate_combine(o_cmp, o_slc, o_win, gates, 512)

    mla_k, mla_v = _mla_kv(mla_ckv, row(mla_kv_norm),
                           jnp.asarray(mla_w_uk, bf16), jnp.asarray(mla_w_uv, bf16), ropes, T, 512)
    o_mla = _mla_attn(mla_q, mla_k, mla_v, B, T)

    z = _branch_merge(o_nsa, o_mla, w_branch_nsa.astype(bf16), w_branch_mla.astype(bf16), merge, 1024, 512)
    x1 = _proj_ln(z, w_mix_out.astype(bf16), x2d, row(ln_mix_g), row(ln_mix_b), 512)

    mem2d = mem.reshape(B * N_MEM, D).astype(bf16)
    km = _mm(mem2d, mem_w_k.astype(bf16), bf16, B * N_MEM, 256, name="mem_k_proj").reshape(B, N_MEM, -1)
    vm = _mm(mem2d, mem_w_v.astype(bf16), bf16, B * N_MEM, 256, name="mem_v_proj").reshape(B, N_MEM, -1)
    x2, x2b = _mem_attn(x1, mem_w_q.astype(bf16), km, vm, mem_w_o.astype(bf16), row(ln_mem_g), row(ln_mem_b), T, 512)

    pq = _mm(x2b, peer_w_query.astype(bf16), bf16, 1024, 512, name="peer_query_proj")
    e1, e2, gt = _peer_route(pq, peer_sub_keys)
    wts = jnp.transpose(_peer_weights(e1, e2, gt), (1, 0, 2))
    y = _peer_dense(x2b, wts, peer_u, peer_v)
    x3 = _add_ln(x2, y, row(ln_ffn_g), row(ln_ffn_b), 512)
    return x3.reshape(B, T, D)


def kernel(x, mem, rel_bias, w_in, cmp_k_pe, cmp_k_w1, cmp_k_w2, cmp_v_pe, cmp_v_w1, cmp_v_w2, mla_kv_norm, mla_w_uk, mla_w_uv, w_branch_nsa, w_branch_mla, w_mix_out, ln_mix_g, ln_mix_b, mem_w_q, mem_w_k, mem_w_v, mem_w_o, ln_mem_g, ln_mem_b, peer_w_query, peer_sub_keys, peer_u, peer_v, ln_ffn_g, ln_ffn_b):
    ropes = _rope_slot_tables(x.shape[1])
    for l in range(DEPTH):
        x = _layer(x, mem, rel_bias, w_in[l], cmp_k_pe[l], cmp_k_w1[l], cmp_k_w2[l], cmp_v_pe[l], cmp_v_w1[l],
                   cmp_v_w2[l], mla_kv_norm[l], mla_w_uk[l], mla_w_uv[l], w_branch_nsa[l], w_branch_mla[l],
                   w_mix_out[l], ln_mix_g[l], ln_mix_b[l], mem_w_q[l], mem_w_k[l], mem_w_v[l], mem_w_o[l],
                   ln_mem_g[l], ln_mem_b[l], peer_w_query[l], peer_sub_keys[l], peer_u[l], peer_v[l],
                   ln_ffn_g[l], ln_ffn_b[l], ropes)
    return x
```

```python
import functools
import math

import numpy as np
import jax
import jax.numpy as jnp
from jax import lax
from jax.experimental import pallas as pl
from jax.experimental.pallas import tpu as pltpu

f32 = jnp.float32
bf16 = jnp.bfloat16
i32 = jnp.int32

D_MODEL = 2048
N_MEM = 256
NSA_HEADS = 16
NSA_GROUPS = 2
NSA_HPG = NSA_HEADS // NSA_GROUPS
HEAD_DIM = 128
CMP_BLOCK = 32
CMP_STRIDE = 16
CMP_HIDDEN = 256
SLC_BLOCK = 64
SLC_TOPK = 16
WINDOW = 512
FORCED_SCORE = 1e4
MLA_HEADS = 16
MLA_NOPE = 128
MLA_ROPE = 64
MLA_KV_RANK = 512
MLA_SLOT = 256
ROPE_THETA = 10000.0
REL_BUCKETS = 32
REL_MAX_EXACT = 16
REL_MAX_DIST = 4096
MEM_HEADS = 4
PEER_HEADS = 8
PEER_N_KEYS = 128
PEER_TOPK = 16
PEER_HK = PEER_HEADS * PEER_TOPK
DEPTH = 1
ALPHA = (2.0 * DEPTH) ** 0.25
NEG_INF = -1e30
M_INIT = -1e29
PAD_SCORE = -3e38
PAD_ID = 1e9
LOG2E = math.log2(math.e)
LN_EPS = 1e-5
RMS_EPS = 1e-6

NSA_Q_COLS = NSA_HEADS * HEAD_DIM
NSA_KV_COLS = NSA_GROUPS * HEAD_DIM
OFF_NSA_KV = NSA_Q_COLS
OFF_NSA_GATE = OFF_NSA_KV + 6 * NSA_KV_COLS
OFF_MLA_Q = OFF_NSA_GATE + 3 * NSA_HEADS
OFF_MLA_KV = OFF_MLA_Q + MLA_HEADS * (MLA_NOPE + MLA_ROPE)
OFF_MERGE = OFF_MLA_KV + MLA_KV_RANK + MLA_ROPE

LANES = 128
Q_TILE = 512
K_TILE = 512
WIN_TILE = 256
ROW_CHUNK = 32
CMP_TQ = 128
VMEM_LIMIT = 56 * 1024 * 1024

NT = (((1,), (1,)), ((), ()))


def _params(sem):
    return pltpu.CompilerParams(dimension_semantics=sem, vmem_limit_bytes=VMEM_LIMIT)


def _mm_body(a_ref, b_ref, o_ref, *, scale, act):
    acc = jnp.dot(a_ref[...].astype(bf16), b_ref[...], preferred_element_type=f32)
    if scale != 1.0:
        acc = acc * scale
    if act == "sigmoid":
        acc = jax.nn.sigmoid(acc)
    o_ref[...] = acc.astype(o_ref.dtype)


def _mm(a, b, out_dtype, tm, tn, scale=1.0, act=None, name="mm", cols=None):
    M, K = a.shape
    c0, Nc = (0, b.shape[1]) if cols is None else cols
    assert c0 % tn == 0 and Nc % tn == 0
    j0 = c0 // tn
    return pl.pallas_call(
        functools.partial(_mm_body, scale=scale, act=act),
        grid=(M // tm, Nc // tn),
        in_specs=[pl.BlockSpec((tm, K), lambda i, j: (i, 0)), pl.BlockSpec((K, tn), lambda i, j: (0, j0 + j))],
        out_specs=pl.BlockSpec((tm, tn), lambda i, j: (i, j)),
        out_shape=jax.ShapeDtypeStruct((M, Nc), out_dtype),
        compiler_params=_params(("parallel", "parallel")),
        name=name,
    )(a, b)


def _rope_slot(y, ta, tb, tc):
    return y * ta + pltpu.roll(y, LANES - 32, 1) * tb + pltpu.roll(y, 32, 1) * tc


def _mlaq_body(a_ref, b_ref, ta_ref, tb_ref, tc_ref, o_ref, *, scale, nslot):
    acc = jnp.dot(a_ref[...], b_ref[...], preferred_element_type=f32)
    ta, tb, tc = ta_ref[...], tb_ref[...], tc_ref[...]
    for s in range(nslot):
        lo = s * MLA_SLOT
        o_ref[:, lo:lo + MLA_NOPE] = (acc[:, lo:lo + MLA_NOPE] * scale).astype(o_ref.dtype)
        rot = _rope_slot(acc[:, lo + MLA_NOPE:lo + MLA_SLOT], ta, tb, tc)
        o_ref[:, lo + MLA_NOPE:lo + MLA_SLOT] = (rot * scale).astype(o_ref.dtype)


def _mla_q(xb, w, cols, ropes, T, tm, tn):
    M, K = xb.shape
    c0, Nc = cols
    assert c0 % tn == 0 and Nc % tn == 0
    j0 = c0 // tn
    nt = T // tm
    tab = pl.BlockSpec((tm, LANES), lambda i, j: (i % nt, 0))
    return pl.pallas_call(
        functools.partial(_mlaq_body, scale=(MLA_NOPE + MLA_ROPE) ** -0.5 * LOG2E, nslot=tn // MLA_SLOT),
        grid=(M // tm, Nc // tn),
        in_specs=[pl.BlockSpec((tm, K), lambda i, j: (i, 0)), pl.BlockSpec((K, tn), lambda i, j: (0, j0 + j)),
                  tab, tab, tab],
        out_specs=pl.BlockSpec((tm, tn), lambda i, j: (i, j)),
        out_shape=jax.ShapeDtypeStruct((M, Nc), bf16),
        compiler_params=_params(("parallel", "parallel")),
        name="mla_q_proj_rope",
    )(xb, w, *ropes)


def _mlakv_body(c_ref, g_ref, wuk_ref, wuv_ref, ta_ref, tb_ref, tc_ref, k_ref, v_ref):
    x = c_ref[...]
    c = x[:, :MLA_KV_RANK]
    ms = jnp.mean(c * c, axis=-1, keepdims=True)
    cn = (c * lax.rsqrt(ms + RMS_EPS) * g_ref[...]).astype(bf16)
    kpe = _rope_slot(x[:, MLA_KV_RANK:], ta_ref[...], tb_ref[...], tc_ref[...]).astype(bf16)
    kn = jnp.dot(cn, wuk_ref[...], preferred_element_type=f32).astype(bf16)
    v_ref[...] = jnp.dot(cn, wuv_ref[...], preferred_element_type=f32).astype(bf16)
    for h in range(MLA_HEADS):
        k_ref[:, h * MLA_SLOT:h * MLA_SLOT + MLA_NOPE] = kn[:, h * MLA_NOPE:(h + 1) * MLA_NOPE]
        k_ref[:, h * MLA_SLOT + MLA_NOPE:(h + 1) * MLA_SLOT] = kpe


def _mla_kv(ckv, g, wuk, wuv, ropes, T, tm):
    M, K = ckv.shape
    nt = T // tm
    tab = pl.BlockSpec((tm, LANES), lambda i: (i % nt, 0))
    full = lambda shape: pl.BlockSpec(shape, lambda i: (0,) * len(shape))
    return pl.pallas_call(
        _mlakv_body,
        grid=(M // tm,),
        in_specs=[pl.BlockSpec((tm, K), lambda i: (i, 0)), full(g.shape), full(wuk.shape), full(wuv.shape), tab, tab, tab],
        out_specs=[pl.BlockSpec((tm, MLA_HEADS * MLA_SLOT), lambda i: (i, 0)),
                   pl.BlockSpec((tm, MLA_HEADS * HEAD_DIM), lambda i: (i, 0))],
        out_shape=[jax.ShapeDtypeStruct((M, MLA_HEADS * MLA_SLOT), bf16),
                   jax.ShapeDtypeStruct((M, MLA_HEADS * HEAD_DIM), bf16)],
        compiler_params=_params(("parallel",)),
        name="mla_kv_prep",
    )(ckv, g, wuk, wuv, *ropes)


FLAG_FIRST, FLAG_LAST, FLAG_CAUSAL = 1, 2, 4


def _tile_schedule(nq, tq, tk, first_key_tile):
    qs, ks, fl = [], [], []
    for qi in range(nq):
        lo = first_key_tile(qi)
        hi = (qi * tq + tq - 1) // tk
        for ki in range(lo, hi + 1):
            qs.append(qi)
            ks.append(ki)
            fl.append((FLAG_FIRST if ki == lo else 0) | (FLAG_LAST if ki == hi else 0)
                      | (FLAG_CAUSAL if ki * tk + tk - 1 > qi * tq else 0))
    return tuple(jnp.asarray(np.array(a, np.int32)) for a in (qs, ks, fl))


def _flash_scratch(heads, tq, tk):
    stat = pltpu.VMEM((heads, tq, LANES), f32)
    return [stat, stat, pltpu.VMEM((heads, tq, HEAD_DIM), f32),
            pltpu.VMEM((3, tq, tk), f32), pltpu.VMEM((2, tq, tk), bf16), pltpu.VMEM((2, tq, LANES), f32),
            pltpu.VMEM((2, tq, LANES), f32), pltpu.VMEM((tq, tk), f32)]


def _flash_init(m_ref, l_ref, acc_ref):
    m_ref[...] = jnp.full(m_ref.shape, M_INIT, f32)
    l_ref[...] = jnp.zeros(l_ref.shape, f32)
    acc_ref[...] = jnp.zeros(acc_ref.shape, f32)


def _flash_heads(nheads, tq, tk, score_fn, add_fn, value_fn, m_ref, l_ref, acc_ref, s_scr, p_scr, mx_scr, a_scr):
    nkc = tk // LANES
    chunks = [(pl.ds(c * ROW_CHUNK, ROW_CHUNK), c * ROW_CHUNK) for c in range(tq // ROW_CHUNK)]

    def scores(h):
        s_scr[h % 3] = score_fn(h)

    def pass_a(h):
        s_ref = s_scr.at[h % 3]
        for rows, r0 in chunks:
            mx = None
            for kc in range(nkc):
                cols = slice(kc * LANES, (kc + 1) * LANES)
                s = s_ref[rows, cols]
                if add_fn is not None:
                    for term in add_fn(h, rows, r0, kc):
                        s = s + term
                    s_ref[rows, cols] = s
                mx = s if mx is None else jnp.maximum(mx, s)
            mx_scr[h % 2, rows, :] = mx
        m_prev = m_ref[h]
        m_new = jnp.maximum(m_prev, jnp.max(mx_scr[h % 2], axis=-1, keepdims=True))
        a_scr[h % 2] = jnp.exp2(m_prev - m_new)
        m_ref[h] = m_new

    def pass_b(h):
        s_ref = s_scr.at[h % 3]
        for rows, _ in chunks:
            m_rows = m_ref[h, rows, :]
            for kc in range(nkc):
                cols = slice(kc * LANES, (kc + 1) * LANES)
                p_scr[h % 2, rows, cols] = jnp.exp2(s_ref[rows, cols] - m_rows).astype(bf16)
        pv = jnp.dot(p_scr[h % 2], value_fn(h), preferred_element_type=f32)
        alpha = a_scr[h % 2]
        acc_ref[h] = alpha * acc_ref[h] + pv[:, :HEAD_DIM]
        l_ref[h] = alpha * l_ref[h] + pv[:, HEAD_DIM:]

    scores(0)
    for h in range(nheads):
        if h + 1 < nheads:
            scores(h + 1)
        pass_a(h)
        if h >= 1:
            pass_b(h - 1)
    pass_b(nheads - 1)


def _with_ones(v):
    return jnp.concatenate([v, jnp.ones((v.shape[0], LANES), v.dtype)], axis=1)


def _mla_attn_body(qi_tab, ki_tab, fl_tab, q_ref, k_ref, v_ref, o_ref, m_ref, l_ref, acc_ref,
                   s_scr, p_scr, mx_scr, a_scr, mask_scr, *, hb, tq, tk):
    st = pl.program_id(2)
    qi, ki, fl = qi_tab[st], ki_tab[st], fl_tab[st]

    pl.when((fl & FLAG_FIRST) != 0)(lambda: _flash_init(m_ref, l_ref, acc_ref))

    def score(h):
        qk = slice(h * MLA_SLOT, (h + 1) * MLA_SLOT)
        return lax.dot_general(q_ref[:, qk], k_ref[:, qk], NT, preferred_element_type=f32)

    def value(h):
        return _with_ones(v_ref[:, h * HEAD_DIM:(h + 1) * HEAD_DIM])

    def tile(causal):
        add = None
        if causal:
            row = qi * tq + lax.broadcasted_iota(i32, (tq, tk), 0)
            col = ki * tk + lax.broadcasted_iota(i32, (tq, tk), 1)
            mask_scr[...] = jnp.where(row >= col, 0.0, NEG_INF)
            add = lambda h, rows, r0, kc: (mask_scr[rows, kc * LANES:(kc + 1) * LANES],)
        _flash_heads(hb, tq, tk, score, add, value, m_ref, l_ref, acc_ref, s_scr, p_scr, mx_scr, a_scr)

    pl.when((fl & FLAG_CAUSAL) != 0)(lambda: tile(True))
    pl.when((fl & FLAG_CAUSAL) == 0)(lambda: tile(False))

    @pl.when((fl & FLAG_LAST) != 0)
    def _():
        for h in range(hb):
            o_ref[:, h * HEAD_DIM:(h + 1) * HEAD_DIM] = (acc_ref[h] / l_ref[h]).astype(o_ref.dtype)


def _mla_attn(q, k, v, B, T, hb=8, tq=Q_TILE, tk=K_TILE):
    tk = min(tk, T)
    nq, nk = T // tq, T // tk
    tabs = _tile_schedule(nq, tq, tk, lambda qi: 0)
    grid_spec = pltpu.PrefetchScalarGridSpec(
        num_scalar_prefetch=3,
        grid=(B, MLA_HEADS // hb, int(tabs[0].shape[0])),
        in_specs=[pl.BlockSpec((tq, hb * MLA_SLOT), lambda b, h, s, qt, kt, ft: (b * nq + qt[s], h)),
                  pl.BlockSpec((tk, hb * MLA_SLOT), lambda b, h, s, qt, kt, ft: (b * nk + kt[s], h)),
                  pl.BlockSpec((tk, hb * HEAD_DIM), lambda b, h, s, qt, kt, ft: (b * nk + kt[s], h))],
        out_specs=pl.BlockSpec((tq, hb * HEAD_DIM), lambda b, h, s, qt, kt, ft: (b * nq + qt[s], h)),
        scratch_shapes=_flash_scratch(hb, tq, tk))
    return pl.pallas_call(
        functools.partial(_mla_attn_body, hb=hb, tq=tq, tk=tk),
        grid_spec=grid_spec,
        out_shape=jax.ShapeDtypeStruct((B * T, MLA_HEADS * HEAD_DIM), bf16),
        compiler_params=_params(("parallel", "parallel", "arbitrary")),
        name="mla_flash_attention",
    )(*tabs, q, k, v)


def _bias_table_body(rev_ref, o_ref):
    width = rev_ref.shape[-1]
    x = jnp.broadcast_to(rev_ref[0], (LANES, width))
    rolled = pltpu.roll(x, 0, 1, stride=1, stride_axis=0)
    for kb in range(width // LANES):
        o_ref[0, kb] = rolled[:, kb * LANES:(kb + 1) * LANES]


def _bias_table(rev):
    H, _, U = rev.shape
    return pl.pallas_call(
        _bias_table_body,
        grid=(H,),
        in_specs=[pl.BlockSpec((1, 1, U), lambda h: (h, 0, 0))],
        out_specs=pl.BlockSpec((1, U // LANES, LANES, LANES), lambda h: (h, 0, 0, 0)),
        out_shape=jax.ShapeDtypeStruct((H, U // LANES, LANES, LANES), f32),
        compiler_params=_params(("parallel",)),
        name="relative_bias_toeplitz_table",
    )(rev)


def _nsa_band_body(qi_tab, ki_tab, fl_tab, q_ref, k_ref, v_ref, g_ref, *rest, tq, tk, seq, select):
    if select:
        sel_ref, e_ref, o_ref, m_ref, l_ref, acc_ref, s_scr, p_scr, mx_scr, a_scr, mask_scr = rest
    else:
        oc_ref, os_ref, gate_ref, o_ref, m_ref, l_ref, acc_ref, s_scr, p_scr, mx_scr, a_scr, mask_scr = rest
    st = pl.program_id(2)
    qi, ki, fl = qi_tab[st], ki_tab[st], fl_tab[st]

    pl.when((fl & FLAG_FIRST) != 0)(lambda: _flash_init(m_ref, l_ref, acc_ref))

    d0 = qi * tq - ki * tk
    dist = d0 + lax.broadcasted_iota(i32, (tq, tk), 0) - lax.broadcasted_iota(i32, (tq, tk), 1)
    if select:
        picked = jnp.dot(sel_ref[...], e_ref[...], preferred_element_type=f32)
        valid = (dist >= 0) & (picked > 0.5)
    else:
        valid = (dist >= 0) & (dist < WINDOW)
    mask_scr[...] = jnp.where(valid, 0.0, NEG_INF)
    k = k_ref[...].astype(bf16)
    v = _with_ones(v_ref[...].astype(bf16))
    kb0 = (seq - d0) // LANES

    def score(h):
        return lax.dot_general(q_ref[:, h * HEAD_DIM:(h + 1) * HEAD_DIM], k, NT, preferred_element_type=f32)

    def add(h, rows, r0, kc):
        bias = g_ref[h, kb0 - r0 // LANES + kc, pl.ds(r0 % LANES, ROW_CHUNK), :]
        return bias, mask_scr[rows, kc * LANES:(kc + 1) * LANES]

    _flash_heads(NSA_HPG, tq, tk, score, add, lambda h: v, m_ref, l_ref, acc_ref, s_scr, p_scr, mx_scr, a_scr)

    @pl.when((fl & FLAG_LAST) != 0)
    def _():
        for h in range(NSA_HPG):
            sl = slice(h * HEAD_DIM, (h + 1) * HEAD_DIM)
            o = acc_ref[h] / l_ref[h]
            if not select:
                gate = gate_ref[...]
                o = (gate[:, h:h + 1] * oc_ref[:, sl].astype(f32)
                     + gate[:, NSA_HPG + h:NSA_HPG + h + 1] * os_ref[:, sl].astype(f32)
                     + gate[:, 2 * NSA_HPG + h:2 * NSA_HPG + h + 1] * o)
            o_ref[:, sl] = o.astype(o_ref.dtype)


def _nsa_band(q, kv, gtab, B, T, kcol, vcol, tk, sel=None, emat=None, merge=None, tq=Q_TILE):
    select = sel is not None
    nq, nk = T // tq, T // tk
    first = (lambda qi: 0) if select else (lambda qi: max(qi * tq - (WINDOW - 1), 0) // tk)
    tabs = _tile_schedule(nq, tq, tk, first)
    gw = NSA_HPG * HEAD_DIM
    in_specs = [
        pl.BlockSpec((tq, gw), lambda b, g, s, qt, kt, ft: (b * nq + qt[s], g)),
        pl.BlockSpec((tk, HEAD_DIM), lambda b, g, s, qt, kt, ft: (b * nk + kt[s], kcol * NSA_GROUPS + g)),
        pl.BlockSpec((tk, HEAD_DIM), lambda b, g, s, qt, kt, ft: (b * nk + kt[s], vcol * NSA_GROUPS + g)),
        pl.BlockSpec((NSA_HPG,) + gtab.shape[1:], lambda b, g, s, qt, kt, ft: (g, 0, 0, 0),
                     pipeline_mode=pl.Buffered(1)),
    ]
    args = [q, kv, kv, gtab]
    if select:
        in_specs += [pl.BlockSpec((tq, LANES), lambda b, g, s, qt, kt, ft: (b * nq + qt[s], g)),
                     pl.BlockSpec((LANES, tk), lambda b, g, s, qt, kt, ft: (0, kt[s]))]
        args += [sel, emat]
    else:
        o_other = pl.BlockSpec((tq, gw), lambda b, g, s, qt, kt, ft: (b * nq + qt[s], g))
        in_specs += [o_other, o_other, pl.BlockSpec((tq, LANES), lambda b, g, s, qt, kt, ft: (b * nq + qt[s], g))]
        args += list(merge)
    grid_spec = pltpu.PrefetchScalarGridSpec(
        num_scalar_prefetch=3,
        grid=(B, NSA_GROUPS, int(tabs[0].shape[0])),
        in_specs=in_specs,
        out_specs=pl.BlockSpec((tq, gw), lambda b, g, s, qt, kt, ft: (b * nq + qt[s], g)),
        scratch_shapes=_flash_scratch(NSA_HPG, tq, tk))
    return pl.pallas_call(
        functools.partial(_nsa_band_body, tq=tq, tk=tk, seq=T, select=select),
        grid_spec=grid_spec,
        out_shape=jax.ShapeDtypeStruct((B * T, NSA_HEADS * HEAD_DIM), bf16),
        compiler_params=_params(("parallel", "parallel", "arbitrary")),
        name="nsa_selected_attention" if select else "nsa_window_attention",
    )(*tabs, *args)


def _compress_body(x_ref, pe_ref, w1_ref, w2_ref, o_ref, ot_ref, *, nc):
    half = CMP_BLOCK // 2
    h_lo = jnp.zeros((nc, CMP_HIDDEN), f32)
    h_hi = jnp.zeros((nc, CMP_HIDDEN), f32)
    for p in range(half):
        rows = x_ref[pl.ds(p, nc, stride=CMP_STRIDE), :]
        h_lo += jnp.dot((rows + pe_ref[0, p:p + 1, :]).astype(bf16), w1_ref[0, p], preferred_element_type=f32)
        h_hi += jnp.dot((rows + pe_ref[0, half + p:half + p + 1, :]).astype(bf16), w1_ref[0, half + p],
                        preferred_element_type=f32)
    hid = h_lo + pltpu.roll(h_hi, nc - 1, 0)
    out = jnp.dot(jax.nn.gelu(hid).astype(bf16), w2_ref[0], preferred_element_type=f32)
    out = jnp.where(lax.broadcasted_iota(i32, out.shape, 0) < nc - 1, out, 0.0)
    o_ref[0, 0, 0] = out.astype(bf16)
    ot_ref[0, 0, 0] = out.T.astype(bf16)


def _compress(kv, pe, w1, w2, B, T):
    nc = T // CMP_STRIDE
    return pl.pallas_call(
        functools.partial(_compress_body, nc=nc),
        grid=(2, B, NSA_GROUPS),
        in_specs=[pl.BlockSpec((T, HEAD_DIM), lambda c, b, g: (b, c * NSA_GROUPS + g)),
                  pl.BlockSpec((1, CMP_BLOCK, HEAD_DIM), lambda c, b, g: (c, 0, 0)),
                  pl.BlockSpec((1, CMP_BLOCK, HEAD_DIM, CMP_HIDDEN), lambda c, b, g: (c, 0, 0, 0)),
                  pl.BlockSpec((1, CMP_HIDDEN, HEAD_DIM), lambda c, b, g: (c, 0, 0))],
        out_specs=[pl.BlockSpec((1, 1, 1, nc, HEAD_DIM), lambda c, b, g: (c, b, g, 0, 0)),
                   pl.BlockSpec((1, 1, 1, HEAD_DIM, nc), lambda c, b, g: (c, b, g, 0, 0))],
        out_shape=[jax.ShapeDtypeStruct((2, B, NSA_GROUPS, nc, HEAD_DIM), bf16),
                   jax.ShapeDtypeStruct((2, B, NSA_GROUPS, HEAD_DIM, nc), bf16)],
        compiler_params=_params(("parallel", "parallel", "parallel")),
        name="nsa_compress_mlp",
    )(kv, pe, w1, w2)


def _cmp_body(q_ref, kc_ref, vct_ref, cw_ref, ovt_ref, o_ref, sel_ref, *, nc, nq):
    tq = CMP_TQ
    qi = pl.program_id(2)
    t0 = qi * tq
    key_row = lax.broadcasted_iota(i32, (nc, tq), 0)
    q_col = lax.broadcasted_iota(i32, (nc, tq), 1)
    valid = (t0 + q_col - CMP_STRIDE * key_row - (CMP_BLOCK - 1)) >= 0
    kc = kc_ref[0, 0, 0]
    vct = vct_ref[0, 0, 0]
    psum = jnp.zeros((nc, tq), f32)
    for h in range(NSA_HPG):
        bias = jnp.concatenate([cw_ref[h, qi - c + nq] for c in range(nc // 8)], axis=0)
        s = lax.dot_general(kc, q_ref[:, h * HEAD_DIM:(h + 1) * HEAD_DIM], NT, preferred_element_type=f32)
        s = jnp.where(valid, s + bias, NEG_INF)
        m = jnp.max(s, axis=0, keepdims=True)
        e = jnp.where(valid, jnp.exp2(s - m), 0.0)
        l = jnp.sum(e, axis=0, keepdims=True)
        p = e / jnp.where(l > 0.0, l, 1.0)
        psum += p
        o_t = jnp.dot(vct, p.astype(bf16), preferred_element_type=f32)
        o_ref[:, h * HEAD_DIM:(h + 1) * HEAD_DIM] = o_t.T.astype(o_ref.dtype)

    imp = jnp.dot(ovt_ref[...], psum, preferred_element_type=f32, precision=lax.Precision.HIGHEST)
    nrow = imp.shape[0]
    n_slc = (nc * CMP_STRIDE) // SLC_BLOCK
    j = lax.broadcasted_iota(i32, (nrow, tq), 0)
    t = t0 + lax.broadcasted_iota(i32, (nrow, tq), 1)
    cur = jnp.right_shift(t, 6)
    ok = j * SLC_BLOCK <= t
    forced = (j == 0) | (j == cur) | (j == cur - 1)
    score = jnp.where(forced, FORCED_SCORE, jnp.where(ok, imp, NEG_INF))
    score = jnp.where(j < n_slc, score, PAD_SCORE)
    chosen = jnp.zeros((nrow, tq), f32)
    for _ in range(min(SLC_TOPK, n_slc)):
        top = jnp.max(score, axis=0, keepdims=True)
        first = jnp.min(jnp.where(score == top, j, nrow), axis=0, keepdims=True)
        hit = j == first
        chosen = jnp.where(hit, 1.0, chosen)
        score = jnp.where(hit, PAD_SCORE, score)
    chosen = jnp.where(ok & (j < n_slc), chosen, 0.0)
    sel_ref[...] = chosen.T.astype(sel_ref.dtype)


def _nsa_cmp(q, kc, vct, cw, ovt, B, T):
    nc = T // CMP_STRIDE
    nq = T // CMP_TQ
    gw = NSA_HPG * HEAD_DIM
    return pl.pallas_call(
        functools.partial(_cmp_body, nc=nc, nq=nq),
        grid=(B, NSA_GROUPS, nq),
        in_specs=[pl.BlockSpec((CMP_TQ, gw), lambda b, g, qi: (b * nq + qi, g)),
                  pl.BlockSpec((1, 1, 1, nc, HEAD_DIM), lambda b, g, qi: (0, b, g, 0, 0)),
                  pl.BlockSpec((1, 1, 1, HEAD_DIM, nc), lambda b, g, qi: (1, b, g, 0, 0)),
                  pl.BlockSpec((NSA_HPG, 2 * nq, 8, CMP_TQ), lambda b, g, qi: (g, 0, 0, 0)),
                  pl.BlockSpec(ovt.shape, lambda b, g, qi: (0, 0))],
        out_specs=[pl.BlockSpec((CMP_TQ, gw), lambda b, g, qi: (b * nq + qi, g)),
                   pl.BlockSpec((CMP_TQ, LANES), lambda b, g, qi: (b * nq + qi, g))],
        out_shape=[jax.ShapeDtypeStruct((B * T, NSA_HEADS * HEAD_DIM), bf16),
                   jax.ShapeDtypeStruct((B * T, NSA_GROUPS * LANES), bf16)],
        compiler_params=_params(("parallel", "parallel", "parallel")),
        name="nsa_compressed_attention_select",
    )(q, kc, vct, cw, ovt)


def _layer_norm(y, g, b):
    mu = jnp.mean(y, axis=-1, keepdims=True)
    var = jnp.mean(jnp.square(y - mu), axis=-1, keepdims=True)
    return (y - mu) * lax.rsqrt(var + LN_EPS) * g + b


def _branch_merge_body(a_ref, b_ref, wa_ref, wb_ref, ga_ref, gb_ref, o_ref):
    ya = jnp.dot(a_ref[...], wa_ref[...], preferred_element_type=f32)
    yb = jnp.dot(b_ref[...], wb_ref[...], preferred_element_type=f32)
    o_ref[...] = (ga_ref[...] * ya + gb_ref[...] * yb).astype(o_ref.dtype)


def _branch_merge(a, b, wa, wb, mg, tm, tn):
    M, K = a.shape
    Nc = wa.shape[1]
    nj = Nc // tn
    return pl.pallas_call(
        _branch_merge_body,
        grid=(M // tm, nj),
        in_specs=[pl.BlockSpec((tm, K), lambda i, j: (i, 0)), pl.BlockSpec((tm, K), lambda i, j: (i, 0)),
                  pl.BlockSpec((K, tn), lambda i, j: (0, j)), pl.BlockSpec((K, tn), lambda i, j: (0, j)),
                  pl.BlockSpec((tm, tn), lambda i, j: (i, j)), pl.BlockSpec((tm, tn), lambda i, j: (i, nj + j))],
        out_specs=pl.BlockSpec((tm, tn), lambda i, j: (i, j)),
        out_shape=jax.ShapeDtypeStruct((M, Nc), bf16),
        compiler_params=_params(("parallel", "parallel")),
        name="branch_proj_merge",
    )(a, b, wa, wb, mg, mg)


def _proj_ln_body(z_ref, w_ref, x_ref, g_ref, b_ref, o_ref):
    y = jnp.dot(z_ref[...], w_ref[...], preferred_element_type=f32)
    o_ref[...] = _layer_norm(ALPHA * x_ref[...] + y, g_ref[...], b_ref[...])


def _proj_ln(z, w, x, g, b, tm):
    M, K = z.shape
    D = w.shape[1]
    vec = pl.BlockSpec((1, D), lambda i: (0, 0))
    return pl.pallas_call(
        _proj_ln_body,
        grid=(M // tm,),
        in_specs=[pl.BlockSpec((tm, K), lambda i: (i, 0)), pl.BlockSpec((K, D), lambda i: (0, 0)),
                  pl.BlockSpec((tm, D), lambda i: (i, 0)), vec, vec],
        out_specs=pl.BlockSpec((tm, D), lambda i: (i, 0)),
        out_shape=jax.ShapeDtypeStruct((M, D), f32),
        compiler_params=_params(("parallel",)),
        name="mix_out_proj_layernorm",
    )(z, w, x, g, b)


def _mem_attn_body(x_ref, wq_ref, k_ref, v_ref, wo_ref, g_ref, b_ref, o_ref, ob_ref):
    x = x_ref[...]
    q = (jnp.dot(x.astype(bf16), wq_ref[...], preferred_element_type=f32) * HEAD_DIM ** -0.5).astype(bf16)
    heads = []
    for h in range(MEM_HEADS):
        sl = slice(h * HEAD_DIM, (h + 1) * HEAD_DIM)
        s = lax.dot_general(q[:, sl], k_ref[0, :, sl], NT, preferred_element_type=f32)
        e = jnp.exp(s - jnp.max(s, axis=-1, keepdims=True))
        p = e / jnp.sum(e, axis=-1, keepdims=True)
        heads.append(jnp.dot(p.astype(bf16), v_ref[0, :, sl], preferred_element_type=f32).astype(bf16))
    y = jnp.dot(jnp.concatenate(heads, axis=1), wo_ref[...], preferred_element_type=f32)
    out = _layer_norm(ALPHA * x + y, g_ref[...], b_ref[...])
    o_ref[...] = out
    ob_ref[...] = out.astype(bf16)


def _mem_attn(x, wq, km, vm, wo, g, b, T, tm):
    M, D = x.shape
    nt = T // tm
    W = wq.shape[1]
    vec = pl.BlockSpec((1, D), lambda i: (0, 0))
    blk = pl.BlockSpec((tm, D), lambda i: (i, 0))
    return pl.pallas_call(
        _mem_attn_body,
        grid=(M // tm,),
        in_specs=[blk, pl.BlockSpec((D, W), lambda i: (0, 0)),
                  pl.BlockSpec((1, N_MEM, W), lambda i: (i // nt, 0, 0)),
                  pl.BlockSpec((1, N_MEM, W), lambda i: (i // nt, 0, 0)),
                  pl.BlockSpec((W, D), lambda i: (0, 0)), vec, vec],
        out_specs=[blk, blk],
        out_shape=[jax.ShapeDtypeStruct((M, D), f32), jax.ShapeDtypeStruct((M, D), bf16)],
        compiler_params=_params(("parallel",)),
        name="memory_cross_attention_layernorm",
    )(x, wq, km, vm, wo, g, b)


def _add_ln_body(x_ref, y_ref, g_ref, b_ref, o_ref):
    o_ref[...] = _layer_norm(ALPHA * x_ref[...] + y_ref[...], g_ref[...], b_ref[...])


def _add_ln(x, y, g, b, tm):
    M, D = x.shape
    vec = pl.BlockSpec((1, D), lambda i: (0, 0))
    blk = pl.BlockSpec((tm, D), lambda i: (i, 0))
    return pl.pallas_call(
        _add_ln_body,
        grid=(M // tm,),
        in_specs=[blk, blk, vec, vec],
        out_specs=blk,
        out_shape=jax.ShapeDtypeStruct((M, D), f32),
        compiler_params=_params(("parallel",)),
        name="residual_layernorm",
    )(x, y, g, b)


def _topk_rows(s, ids, k):
    n = s.shape[1]
    slot = lax.broadcasted_iota(i32, (k, n), 0)
    vals = jnp.zeros((k, n), f32)
    inds = jnp.zeros((k, n), f32)
    for r in range(k):
        top = jnp.max(s, axis=0, keepdims=True)
        first = jnp.min(jnp.where(s == top, ids, PAD_ID), axis=0, keepdims=True)
        vals = jnp.where(slot == r, top, vals)
        inds = jnp.where(slot == r, first, inds)
        s = jnp.where(ids == first, PAD_SCORE, s)
    return vals, inds


def _pick_rows(table, sel):
    out = jnp.zeros(sel.shape, table.dtype)
    for a in range(table.shape[0]):
        out = jnp.where(sel == a, table[a:a + 1, :], out)
    return out


CAND_COUNTS = tuple(PEER_TOPK // (a + 1) for a in range(PEER_TOPK))
CAND_ROWS = -(-sum(CAND_COUNTS) // 8) * 8


def _cand_ids(tn):
    ids = [a * PEER_TOPK + b for a, nb in enumerate(CAND_COUNTS) for b in range(nb)]
    ids = np.array(ids + [PAD_ID] * (CAND_ROWS - len(ids)), np.float32)
    return jnp.asarray(np.broadcast_to(ids[:, None], (CAND_ROWS, tn)))


def _peer_route_body(q_ref, keys_ref, ids_ref, e1_ref, e2_ref, gt_ref, e1_scr, e2_scr, g_scr, cand_scr):
    k1 = keys_ref[0].astype(bf16)
    k2 = keys_ref[1].astype(bf16)
    tn = q_ref.shape[0]
    key_ids = lax.broadcasted_iota(i32, (PEER_N_KEYS, tn), 0).astype(f32)
    cand_scr[...] = jnp.full(cand_scr.shape, PAD_SCORE, f32)
    for h in range(PEER_HEADS):
        q1 = q_ref[:, (2 * h) * LANES:(2 * h + 1) * LANES]
        q2 = q_ref[:, (2 * h + 1) * LANES:(2 * h + 2) * LANES]
        s1 = lax.dot_general(k1, q1, NT, preferred_element_type=f32)
        s2 = lax.dot_general(k2, q2, NT, preferred_element_type=f32)
        v1, i1 = _topk_rows(s1, key_ids, PEER_TOPK)
        v2, i2 = _topk_rows(s2, key_ids, PEER_TOPK)
        off = 0
        for a, nb in enumerate(CAND_COUNTS):
            cand_scr[off:off + nb, :] = v1[a:a + 1, :] + v2[0:nb, :]
            off += nb
        top_s, flat = _topk_rows(cand_scr[...], ids_ref[...], PEER_TOPK)
        flat = flat.astype(i32)
        e = jnp.exp(top_s - top_s[0:1, :])
        rows = slice(h * PEER_TOPK, (h + 1) * PEER_TOPK)
        g_scr[rows, :] = e / jnp.sum(e, axis=0, keepdims=True)
        e1_scr[rows, :] = _pick_rows(i1, jnp.right_shift(flat, 4)).astype(i32)
        e2_scr[rows, :] = _pick_rows(i2, jnp.bitwise_and(flat, PEER_TOPK - 1)).astype(i32)
    e1_ref[...] = e1_scr[...].T
    e2_ref[...] = e2_scr[...].T
    gt_ref[...] = g_scr[...].T


def _peer_route(q, keys, tn=LANES):
    M, W = q.shape
    out = pl.BlockSpec((tn, PEER_HK), lambda i: (i, 0))
    return pl.pallas_call(
        _peer_route_body,
        grid=(M // tn,),
        in_specs=[pl.BlockSpec((tn, W), lambda i: (i, 0)), pl.BlockSpec(keys.shape, lambda i: (0, 0, 0)),
                  pl.BlockSpec((CAND_ROWS, tn), lambda i: (0, 0))],
        out_specs=[out, out, out],
        out_shape=[jax.ShapeDtypeStruct((M, PEER_HK), i32), jax.ShapeDtypeStruct((M, PEER_HK), i32),
                   jax.ShapeDtypeStruct((M, PEER_HK), f32)],
        scratch_shapes=[pltpu.VMEM((PEER_HK, tn), i32), pltpu.VMEM((PEER_HK, tn), i32), pltpu.VMEM((PEER_HK, tn), f32),
                        pltpu.VMEM((CAND_ROWS, tn), f32)],
        compiler_params=_params(("parallel",)),
        name="peer_product_key_routing",
    )(q, keys, _cand_ids(tn))


def _peer_weights_body(e1_ref, e2_ref, g_ref, w_ref, *, tn):
    sub = lax.broadcasted_iota(i32, (PEER_N_KEYS, PEER_HK), 0)

    def one(n, carry):
        r1 = e1_ref[pl.ds(n, 1), :]
        r2 = e2_ref[pl.ds(n, 1), :]
        g = g_ref[pl.ds(n, 1), :]
        a_t = (sub == r1).astype(bf16)
        b_t = jnp.where(sub == r2, g, 0.0).astype(bf16)
        w_ref[n] = lax.dot_general(a_t, b_t, NT, preferred_element_type=f32).astype(w_ref.dtype)
        return carry

    lax.fori_loop(0, tn, one, 0, unroll=8)


def _peer_weights(e1, e2, g, tn=64):
    M = e1.shape[0]
    blk = pl.BlockSpec((tn, PEER_HK), lambda i: (i, 0))
    return pl.pallas_call(
        functools.partial(_peer_weights_body, tn=tn),
        grid=(M // tn,),
        in_specs=[blk, blk, blk],
        out_specs=pl.BlockSpec((tn, PEER_N_KEYS, PEER_N_KEYS), lambda i: (i, 0, 0)),
        out_shape=jax.ShapeDtypeStruct((M, PEER_N_KEYS, PEER_N_KEYS), bf16),
        compiler_params=_params(("parallel",)),
        name="peer_routing_weights",
    )(e1, e2, g)


def _peer_dense_body(x_ref, w_ref, u_ref, v_ref, o_ref, *, rows):
    @pl.when(pl.program_id(1) == 0)
    def _():
        o_ref[...] = jnp.zeros(o_ref.shape, f32)

    act = jax.nn.gelu(lax.dot_general(x_ref[...], u_ref[...].astype(bf16), NT, preferred_element_type=f32))
    w = jnp.concatenate([w_ref[r] for r in range(rows)], axis=1).astype(f32)
    o_ref[...] += jnp.dot((w * act).astype(bf16), v_ref[...].astype(bf16), preferred_element_type=f32)


def _peer_dense(xb, w, u, v, tm=1024, rows=4):
    M, D = xb.shape
    te = rows * PEER_N_KEYS
    return pl.pallas_call(
        functools.partial(_peer_dense_body, rows=rows),
        grid=(M // tm, PEER_N_KEYS // rows),
        in_specs=[pl.BlockSpec((tm, D), lambda i, e: (i, 0)),
                  pl.BlockSpec((rows, tm, PEER_N_KEYS), lambda i, e: (e, i, 0)),
                  pl.BlockSpec((te, D), lambda i, e: (e, 0)), pl.BlockSpec((te, D), lambda i, e: (e, 0))],
        out_specs=pl.BlockSpec((tm, D), lambda i, e: (i, 0)),
        out_shape=jax.ShapeDtypeStruct((M, D), f32),
        compiler_params=_params(("parallel", "arbitrary")),
        name="peer_dense_experts",
    )(xb, w, u, v)


def _t5_bucket(dist):
    dist = jnp.maximum(dist, 0)
    n_log = REL_BUCKETS - REL_MAX_EXACT
    large = REL_MAX_EXACT + (jnp.log(jnp.maximum(dist, 1).astype(f32) / REL_MAX_EXACT)
                             / math.log(REL_MAX_DIST / REL_MAX_EXACT) * n_log).astype(i32)
    large = jnp.minimum(large, REL_BUCKETS - 1)
    return jnp.where(dist < REL_MAX_EXACT, dist, large)


def _position_tables(rel_bias, T):
    by_dist = rel_bias[_t5_bucket(jnp.arange(T))] * LOG2E

    def lookup(d):
        return jnp.transpose(by_dist[jnp.clip(d, 0, T - 1)], (0, 2, 1)).astype(f32)

    rev = jnp.transpose(lookup((T - np.arange(T + K_TILE))[None, :]), (1, 0, 2))
    u = np.arange(2 * T) - T
    r = np.arange(8)
    cmpw = lookup(u[None, :] - (CMP_BLOCK - 1) - CMP_STRIDE * r[:, None])
    cmpw = jnp.transpose(cmpw.reshape(8, NSA_HEADS, 2 * T // CMP_TQ, CMP_TQ), (1, 2, 0, 3))
    return rev, cmpw


def _rope_slot_tables(T):
    inv = ROPE_THETA ** (-jnp.arange(0, MLA_ROPE, 2, dtype=f32) / MLA_ROPE)
    ang = jnp.arange(T, dtype=f32)[:, None] * inv[None, :]
    cos, sin = jnp.cos(ang), jnp.sin(ang)
    z = jnp.zeros_like(cos)
    ta = jnp.concatenate([cos, cos, z, z], axis=1)
    tb = jnp.concatenate([-sin, z, z, z], axis=1)
    tc = jnp.concatenate([z, sin, z, z], axis=1)
    return ta, tb, tc


def _layer(x, mem, rel_bias, w_in, cmp_k_pe, cmp_k_w1, cmp_k_w2, cmp_v_pe, cmp_v_w1, cmp_v_w2,
           mla_kv_norm, mla_w_uk, mla_w_uv, w_branch_nsa, w_branch_mla, w_mix_out, ln_mix_g, ln_mix_b,
           mem_w_q, mem_w_k, mem_w_v, mem_w_o, ln_mem_g, ln_mem_b,
           peer_w_query, peer_sub_keys, peer_u, peer_v, ln_ffn_g, ln_ffn_b, ropes):
    B, T, D = x.shape
    N = B * T
    x2d = x.reshape(N, D)
    xb = x2d.astype(bf16)
    row = lambda a: a.reshape(1, -1)

    gate_cols = np.array([[br * NSA_HEADS + g * NSA_HPG + h for br in range(3) for h in range(NSA_HPG)]
                          for g in range(NSA_GROUPS)])
    w_gate = jnp.pad(w_in[:, OFF_NSA_GATE:OFF_MLA_Q][:, gate_cols],
                     ((0, 0), (0, 0), (0, LANES - 3 * NSA_HPG))).reshape(D, NSA_GROUPS * LANES)
    w_mla_q = jnp.pad(w_in[:, OFF_MLA_Q:OFF_MLA_KV].reshape(D, MLA_HEADS, MLA_NOPE + MLA_ROPE),
                      ((0, 0), (0, 0), (0, MLA_SLOT - MLA_NOPE - MLA_ROPE))).reshape(D, MLA_HEADS * MLA_SLOT)
    w_mla_kv = jnp.pad(w_in[:, OFF_MLA_KV:OFF_MERGE], ((0, 0), (0, LANES - MLA_ROPE)))
    segs = [w_in[:, :OFF_NSA_KV], w_mla_q, w_in[:, OFF_MERGE:], w_in[:, OFF_NSA_KV:OFF_NSA_GATE], w_mla_kv, w_gate]
    starts = np.concatenate([[0], np.cumsum([s.shape[1] for s in segs])])
    col = lambda i: (int(starts[i]), int(segs[i].shape[1]))
    w_all = jnp.concatenate(segs, axis=1).astype(bf16)

    nsa_q = _mm(xb, w_all, bf16, 1024, 1024, scale=HEAD_DIM ** -0.5 * LOG2E, name="nsa_q_proj", cols=col(0))
    mla_q = _mla_q(xb, w_all, col(1), ropes, T, 1024, 1024)
    merge = _mm(xb, w_all, f32, 1024, 1024, act="sigmoid", name="merge_gate_proj", cols=col(2))
    nsa_kv = _mm(xb, w_all, f32, 1024, 512, name="nsa_kv_proj", cols=col(3))
    mla_ckv = _mm(xb, w_all, f32, 1024, LANES, name="mla_kv_proj", cols=col(4))
    gates = _mm(xb, w_all, f32, 1024, LANES, act="sigmoid", name="nsa_gate_proj", cols=col(5))

    slc_tk = min(K_TILE, T)
    rev_tab, cmp_tab = _position_tables(rel_bias, T)
    bias_tab = _bias_table(rev_tab)
    pe = jnp.stack([cmp_k_pe, cmp_v_pe])
    w1 = jnp.stack([cmp_k_w1, cmp_v_w1]).reshape(2, CMP_BLOCK, HEAD_DIM, CMP_HIDDEN).astype(bf16)
    w2 = jnp.stack([cmp_k_w2, cmp_v_w2]).astype(bf16)
    kvc, kvc_t = _compress(nsa_kv, pe, w1, w2, B, T)
    nc = T // CMP_STRIDE
    n_slc = T // SLC_BLOCK
    cs = np.arange(nc)[None, :] * CMP_STRIDE
    ss = np.arange(LANES)[:, None] * SLC_BLOCK
    ovt = ((cs < ss + SLC_BLOCK) & (cs + CMP_BLOCK > ss) & (np.arange(nc)[None, :] < nc - 1)
           & (np.arange(LANES)[:, None] < n_slc)).astype(np.float32)
    o_cmp, sel = _nsa_cmp(nsa_q, kvc, kvc_t, cmp_tab, jnp.asarray(ovt), B, T)
    emat = (np.arange(T)[None, :] // SLC_BLOCK == np.arange(LANES)[:, None]).astype(np.float32)
    o_slc = _nsa_band(nsa_q, nsa_kv, bias_tab, B, T, 2, 3, slc_tk, sel=sel, emat=jnp.asarray(emat, dtype=bf16))
    o_nsa = _nsa_band(nsa_q, nsa_kv, bias_tab, B, T, 4, 5, WIN_TILE, merge=(o_cmp, o_slc, gates), tq=WIN_TILE)

    mla_k, mla_v = _mla_kv(mla_ckv, row(mla_kv_norm),
                           jnp.asarray(mla_w_uk, bf16), jnp.asarray(mla_w_uv, bf16), ropes, T, 512)
    o_mla = _mla_attn(mla_q, mla_k, mla_v, B, T)

    z = _branch_merge(o_nsa, o_mla, w_branch_nsa.astype(bf16), w_branch_mla.astype(bf16), merge, 1024, 512)
    x1 = _proj_ln(z, w_mix_out.astype(bf16), x2d, row(ln_mix_g), row(ln_mix_b), 512)

    mem2d = mem.reshape(B * N_MEM, D).astype(bf16)
    km = _mm(mem2d, mem_w_k.astype(bf16), bf16, B * N_MEM, 256, name="mem_k_proj").reshape(B, N_MEM, -1)
    vm = _mm(mem2d, mem_w_v.astype(bf16), bf16, B * N_MEM, 256, name="mem_v_proj").reshape(B, N_MEM, -1)
    x2, x2b = _mem_attn(x1, mem_w_q.astype(bf16), km, vm, mem_w_o.astype(bf16), row(ln_mem_g), row(ln_mem_b), T, 512)

    pq = _mm(x2b, peer_w_query.astype(bf16), bf16, 1024, 512, name="peer_query_proj")
    e1, e2, gt = _peer_route(pq, peer_sub_keys)
    wts = jnp.transpose(_peer_weights(e1, e2, gt), (1, 0, 2))
    y = _peer_dense(x2b, wts, peer_u, peer_v)
    x3 = _add_ln(x2, y, row(ln_ffn_g), row(ln_ffn_b), 512)
    return x3.reshape(B, T, D)


def kernel(x, mem, rel_bias, w_in, cmp_k_pe, cmp_k_w1, cmp_k_w2, cmp_v_pe, cmp_v_w1, cmp_v_w2, mla_kv_norm, mla_w_uk, mla_w_uv, w_branch_nsa, w_branch_mla, w_mix_out, ln_mix_g, ln_mix_b, mem_w_q, mem_w_k, mem_w_v, mem_w_o, ln_mem_g, ln_mem_b, peer_w_query, peer_sub_keys, peer_u, peer_v, ln_ffn_g, ln_ffn_b):
    ropes = _rope_slot_tables(x.shape[1])
    for l in range(DEPTH):
        x = _layer(x, mem, rel_bias, w_in[l], cmp_k_pe[l], cmp_k_w1[l], cmp_k_w2[l], cmp_v_pe[l], cmp_v_w1[l],
                   cmp_v_w2[l], mla_kv_norm[l], mla_w_uk[l], mla_w_uv[l], w_branch_nsa[l], w_branch_mla[l],
                   w_mix_out[l], ln_mix_g[l], ln_mix_b[l], mem_w_q[l], mem_w_k[l], mem_w_v[l], mem_w_o[l],
                   ln_mem_g[l], ln_mem_b[l], peer_w_query[l], peer_sub_keys[l], peer_u[l], peer_v[l],
                   ln_ffn_g[l], ln_ffn_b[l], ropes)
    return x
```

```python
import functools
import math

import numpy as np
import jax
import jax.numpy as jnp
from jax import lax
from jax.experimental import pallas as pl
from jax.experimental.pallas import tpu as pltpu

f32 = jnp.float32
bf16 = jnp.bfloat16
i32 = jnp.int32

D_MODEL = 2048
N_MEM = 256
NSA_HEADS = 16
NSA_GROUPS = 2
NSA_HPG = NSA_HEADS // NSA_GROUPS
HEAD_DIM = 128
CMP_BLOCK = 32
CMP_STRIDE = 16
CMP_HIDDEN = 256
SLC_BLOCK = 64
SLC_TOPK = 16
WINDOW = 512
FORCED_SCORE = 1e4
MLA_HEADS = 16
MLA_NOPE = 128
MLA_ROPE = 64
MLA_KV_RANK = 512
MLA_SLOT = 256
ROPE_THETA = 10000.0
REL_BUCKETS = 32
REL_MAX_EXACT = 16
REL_MAX_DIST = 4096
MEM_HEADS = 4
PEER_HEADS = 8
PEER_N_KEYS = 128
PEER_TOPK = 16
PEER_HK = PEER_HEADS * PEER_TOPK
DEPTH = 1
ALPHA = (2.0 * DEPTH) ** 0.25
NEG_INF = -1e30
M_INIT = -1e29
PAD_SCORE = -3e38
PAD_ID = 1e9
LOG2E = math.log2(math.e)
LN_EPS = 1e-5
RMS_EPS = 1e-6

NSA_Q_COLS = NSA_HEADS * HEAD_DIM
NSA_KV_COLS = NSA_GROUPS * HEAD_DIM
OFF_NSA_KV = NSA_Q_COLS
OFF_NSA_GATE = OFF_NSA_KV + 6 * NSA_KV_COLS
OFF_MLA_Q = OFF_NSA_GATE + 3 * NSA_HEADS
OFF_MLA_KV = OFF_MLA_Q + MLA_HEADS * (MLA_NOPE + MLA_ROPE)
OFF_MERGE = OFF_MLA_KV + MLA_KV_RANK + MLA_ROPE

LANES = 128
Q_TILE = 512
K_TILE = 512
WIN_TILE = 256
ROW_CHUNK = 32
CMP_TQ = 128
VMEM_LIMIT = 56 * 1024 * 1024

NT = (((1,), (1,)), ((), ()))


def _params(sem):
    return pltpu.CompilerParams(dimension_semantics=sem, vmem_limit_bytes=VMEM_LIMIT)


def _mm_body(a_ref, b_ref, o_ref, *, scale, act):
    acc = jnp.dot(a_ref[...].astype(bf16), b_ref[...], preferred_element_type=f32)
    if scale != 1.0:
        acc = acc * scale
    if act == "sigmoid":
        acc = jax.nn.sigmoid(acc)
    o_ref[...] = acc.astype(o_ref.dtype)


def _mm(a, b, out_dtype, tm, tn, scale=1.0, act=None, name="mm", cols=None):
    M, K = a.shape
    c0, Nc = (0, b.shape[1]) if cols is None else cols
    assert c0 % tn == 0 and Nc % tn == 0
    j0 = c0 // tn
    return pl.pallas_call(
        functools.partial(_mm_body, scale=scale, act=act),
        grid=(M // tm, Nc // tn),
        in_specs=[pl.BlockSpec((tm, K), lambda i, j: (i, 0)), pl.BlockSpec((K, tn), lambda i, j: (0, j0 + j))],
        out_specs=pl.BlockSpec((tm, tn), lambda i, j: (i, j)),
        out_shape=jax.ShapeDtypeStruct((M, Nc), out_dtype),
        compiler_params=_params(("parallel", "parallel")),
        name=name,
    )(a, b)


def _rope_slot(y, ta, tb, tc):
    return y * ta + pltpu.roll(y, LANES - 32, 1) * tb + pltpu.roll(y, 32, 1) * tc


def _mlaq_body(a_ref, b_ref, ta_ref, tb_ref, tc_ref, o_ref, *, scale, nslot):
    acc = jnp.dot(a_ref[...], b_ref[...], preferred_element_type=f32)
    ta, tb, tc = ta_ref[...], tb_ref[...], tc_ref[...]
    for s in range(nslot):
        lo = s * MLA_SLOT
        o_ref[:, lo:lo + MLA_NOPE] = (acc[:, lo:lo + MLA_NOPE] * scale).astype(o_ref.dtype)
        rot = _rope_slot(acc[:, lo + MLA_NOPE:lo + MLA_SLOT], ta, tb, tc)
        o_ref[:, lo + MLA_NOPE:lo + MLA_SLOT] = (rot * scale).astype(o_ref.dtype)


def _mla_q(xb, w, cols, ropes, T, tm, tn):
    M, K = xb.shape
    c0, Nc = cols
    assert c0 % tn == 0 and Nc % tn == 0
    j0 = c0 // tn
    nt = T // tm
    tab = pl.BlockSpec((tm, LANES), lambda i, j: (i % nt, 0))
    return pl.pallas_call(
        functools.partial(_mlaq_body, scale=(MLA_NOPE + MLA_ROPE) ** -0.5 * LOG2E, nslot=tn // MLA_SLOT),
        grid=(M // tm, Nc // tn),
        in_specs=[pl.BlockSpec((tm, K), lambda i, j: (i, 0)), pl.BlockSpec((K, tn), lambda i, j: (0, j0 + j)),
                  tab, tab, tab],
        out_specs=pl.BlockSpec((tm, tn), lambda i, j: (i, j)),
        out_shape=jax.ShapeDtypeStruct((M, Nc), bf16),
        compiler_params=_params(("parallel", "parallel")),
        name="mla_q_proj_rope",
    )(xb, w, *ropes)


def _mlakv_body(c_ref, g_ref, wuk_ref, wuv_ref, ta_ref, tb_ref, tc_ref, k_ref, v_ref):
    x = c_ref[...]
    c = x[:, :MLA_KV_RANK]
    ms = jnp.mean(c * c, axis=-1, keepdims=True)
    cn = (c * lax.rsqrt(ms + RMS_EPS) * g_ref[...]).astype(bf16)
    kpe = _rope_slot(x[:, MLA_KV_RANK:], ta_ref[...], tb_ref[...], tc_ref[...]).astype(bf16)
    kn = jnp.dot(cn, wuk_ref[...], preferred_element_type=f32).astype(bf16)
    v_ref[...] = jnp.dot(cn, wuv_ref[...], preferred_element_type=f32).astype(bf16)
    for h in range(MLA_HEADS):
        k_ref[:, h * MLA_SLOT:h * MLA_SLOT + MLA_NOPE] = kn[:, h * MLA_NOPE:(h + 1) * MLA_NOPE]
        k_ref[:, h * MLA_SLOT + MLA_NOPE:(h + 1) * MLA_SLOT] = kpe


def _mla_kv(ckv, g, wuk, wuv, ropes, T, tm):
    M, K = ckv.shape
    nt = T // tm
    tab = pl.BlockSpec((tm, LANES), lambda i: (i % nt, 0))
    full = lambda shape: pl.BlockSpec(shape, lambda i: (0,) * len(shape))
    return pl.pallas_call(
        _mlakv_body,
        grid=(M // tm,),
        in_specs=[pl.BlockSpec((tm, K), lambda i: (i, 0)), full(g.shape), full(wuk.shape), full(wuv.shape), tab, tab, tab],
        out_specs=[pl.BlockSpec((tm, MLA_HEADS * MLA_SLOT), lambda i: (i, 0)),
                   pl.BlockSpec((tm, MLA_HEADS * HEAD_DIM), lambda i: (i, 0))],
        out_shape=[jax.ShapeDtypeStruct((M, MLA_HEADS * MLA_SLOT), bf16),
                   jax.ShapeDtypeStruct((M, MLA_HEADS * HEAD_DIM), bf16)],
        compiler_params=_params(("parallel",)),
        name="mla_kv_prep",
    )(ckv, g, wuk, wuv, *ropes)


FLAG_FIRST, FLAG_LAST, FLAG_CAUSAL = 1, 2, 4


def _tile_schedule(nq, tq, tk, first_key_tile):
    qs, ks, fl = [], [], []
    for qi in range(nq):
        lo = first_key_tile(qi)
        hi = (qi * tq + tq - 1) // tk
        for ki in range(lo, hi + 1):
            qs.append(qi)
            ks.append(ki)
            fl.append((FLAG_FIRST if ki == lo else 0) | (FLAG_LAST if ki == hi else 0)
                      | (FLAG_CAUSAL if ki * tk + tk - 1 > qi * tq else 0))
    return tuple(jnp.asarray(np.array(a, np.int32)) for a in (qs, ks, fl))


def _flash_scratch(heads, tq, tk):
    stat = pltpu.VMEM((heads, tq, LANES), f32)
    return [stat, stat, pltpu.VMEM((heads, tq, HEAD_DIM), f32),
            pltpu.VMEM((3, tq, tk), f32), pltpu.VMEM((2, tq, tk), bf16), pltpu.VMEM((2, tq, LANES), f32),
            pltpu.VMEM((2, tq, LANES), f32), pltpu.VMEM((tq, tk), f32)]


def _flash_init(m_ref, l_ref, acc_ref):
    m_ref[...] = jnp.full(m_ref.shape, M_INIT, f32)
    l_ref[...] = jnp.zeros(l_ref.shape, f32)
    acc_ref[...] = jnp.zeros(acc_ref.shape, f32)


def _flash_heads(nheads, tq, tk, score_fn, add_fn, value_fn, m_ref, l_ref, acc_ref, s_scr, p_scr, mx_scr, a_scr):
    nkc = tk // LANES
    chunks = [(pl.ds(c * ROW_CHUNK, ROW_CHUNK), c * ROW_CHUNK) for c in range(tq // ROW_CHUNK)]

    def scores(h):
        s_scr[h % 3] = score_fn(h)

    def pass_a(h):
        s_ref = s_scr.at[h % 3]
        for rows, r0 in chunks:
            mx = None
            for kc in range(nkc):
                cols = slice(kc * LANES, (kc + 1) * LANES)
                s = s_ref[rows, cols]
                if add_fn is not None:
                    for term in add_fn(h, rows, r0, kc):
                        s = s + term
                    s_ref[rows, cols] = s
                mx = s if mx is None else jnp.maximum(mx, s)
            mx_scr[h % 2, rows, :] = mx
        m_prev = m_ref[h]
        m_new = jnp.maximum(m_prev, jnp.max(mx_scr[h % 2], axis=-1, keepdims=True))
        a_scr[h % 2] = jnp.exp2(m_prev - m_new)
        m_ref[h] = m_new

    def pass_b(h):
        s_ref = s_scr.at[h % 3]
        for rows, _ in chunks:
            m_rows = m_ref[h, rows, :]
            for kc in range(nkc):
                cols = slice(kc * LANES, (kc + 1) * LANES)
                p_scr[h % 2, rows, cols] = jnp.exp2(s_ref[rows, cols] - m_rows).astype(bf16)
        pv = jnp.dot(p_scr[h % 2], value_fn(h), preferred_element_type=f32)
        alpha = a_scr[h % 2]
        acc_ref[h] = alpha * acc_ref[h] + pv[:, :HEAD_DIM]
        l_ref[h] = alpha * l_ref[h] + pv[:, HEAD_DIM:]

    scores(0)
    for h in range(nheads):
        if h + 1 < nheads:
            scores(h + 1)
        pass_a(h)
        if h >= 1:
            pass_b(h - 1)
    pass_b(nheads - 1)


def _with_ones(v):
    return jnp.concatenate([v, jnp.ones((v.shape[0], LANES), v.dtype)], axis=1)


def _mla_attn_body(qi_tab, ki_tab, fl_tab, q_ref, k_ref, v_ref, o_ref, m_ref, l_ref, acc_ref,
                   s_scr, p_scr, mx_scr, a_scr, mask_scr, *, hb, tq, tk):
    st = pl.program_id(2)
    qi, ki, fl = qi_tab[st], ki_tab[st], fl_tab[st]

    pl.when((fl & FLAG_FIRST) != 0)(lambda: _flash_init(m_ref, l_ref, acc_ref))

    def score(h):
        qk = slice(h * MLA_SLOT, (h + 1) * MLA_SLOT)
        return lax.dot_general(q_ref[:, qk], k_ref[:, qk], NT, preferred_element_type=f32)

    def value(h):
        return _with_ones(v_ref[:, h * HEAD_DIM:(h + 1) * HEAD_DIM])

    def tile(causal):
        add = None
        if causal:
            row = qi * tq + lax.broadcasted_iota(i32, (tq, tk), 0)
            col = ki * tk + lax.broadcasted_iota(i32, (tq, tk), 1)
            mask_scr[...] = jnp.where(row >= col, 0.0, NEG_INF)
            add = lambda h, rows, r0, kc: (mask_scr[rows, kc * LANES:(kc + 1) * LANES],)
        _flash_heads(hb, tq, tk, score, add, value, m_ref, l_ref, acc_ref, s_scr, p_scr, mx_scr, a_scr)

    pl.when((fl & FLAG_CAUSAL) != 0)(lambda: tile(True))
    pl.when((fl & FLAG_CAUSAL) == 0)(lambda: tile(False))

    @pl.when((fl & FLAG_LAST) != 0)
    def _():
        for h in range(hb):
            o_ref[:, h * HEAD_DIM:(h + 1) * HEAD_DIM] = (acc_ref[h] / l_ref[h]).astype(o_ref.dtype)


def _mla_attn(q, k, v, B, T, hb=8, tq=Q_TILE, tk=K_TILE):
    tk = min(tk, T)
    nq, nk = T // tq, T // tk
    tabs = _tile_schedule(nq, tq, tk, lambda qi: 0)
    grid_spec = pltpu.PrefetchScalarGridSpec(
        num_scalar_prefetch=3,
        grid=(B, MLA_HEADS // hb, int(tabs[0].shape[0])),
        in_specs=[pl.BlockSpec((tq, hb * MLA_SLOT), lambda b, h, s, qt, kt, ft: (b * nq + qt[s], h)),
                  pl.BlockSpec((tk, hb * MLA_SLOT), lambda b, h, s, qt, kt, ft: (b * nk + kt[s], h)),
                  pl.BlockSpec((tk, hb * HEAD_DIM), lambda b, h, s, qt, kt, ft: (b * nk + kt[s], h))],
        out_specs=pl.BlockSpec((tq, hb * HEAD_DIM), lambda b, h, s, qt, kt, ft: (b * nq + qt[s], h)),
        scratch_shapes=_flash_scratch(hb, tq, tk))
    return pl.pallas_call(
        functools.partial(_mla_attn_body, hb=hb, tq=tq, tk=tk),
        grid_spec=grid_spec,
        out_shape=jax.ShapeDtypeStruct((B * T, MLA_HEADS * HEAD_DIM), bf16),
        compiler_params=_params(("parallel", "parallel", "arbitrary")),
        name="mla_flash_attention",
    )(*tabs, q, k, v)


def _bias_table_body(rev_ref, o_ref):
    width = rev_ref.shape[-1]
    x = jnp.broadcast_to(rev_ref[0], (LANES, width))
    rolled = pltpu.roll(x, 0, 1, stride=1, stride_axis=0)
    for kb in range(width // LANES):
        o_ref[0, kb] = rolled[:, kb * LANES:(kb + 1) * LANES]


def _bias_table(rev):
    H, _, U = rev.shape
    return pl.pallas_call(
        _bias_table_body,
        grid=(H,),
        in_specs=[pl.BlockSpec((1, 1, U), lambda h: (h, 0, 0))],
        out_specs=pl.BlockSpec((1, U // LANES, LANES, LANES), lambda h: (h, 0, 0, 0)),
        out_shape=jax.ShapeDtypeStruct((H, U // LANES, LANES, LANES), f32),
        compiler_params=_params(("parallel",)),
        name="relative_bias_toeplitz_table",
    )(rev)


def _nsa_band_body(qi_tab, ki_tab, fl_tab, q_ref, k_ref, v_ref, g_ref, *rest, tq, tk, seq, select):
    if select:
        sel_ref, e_ref, o_ref, m_ref, l_ref, acc_ref, s_scr, p_scr, mx_scr, a_scr, mask_scr = rest
    else:
        oc_ref, os_ref, gate_ref, o_ref, m_ref, l_ref, acc_ref, s_scr, p_scr, mx_scr, a_scr, mask_scr = rest
    st = pl.program_id(2)
    qi, ki, fl = qi_tab[st], ki_tab[st], fl_tab[st]

    pl.when((fl & FLAG_FIRST) != 0)(lambda: _flash_init(m_ref, l_ref, acc_ref))

    d0 = qi * tq - ki * tk
    dist = d0 + lax.broadcasted_iota(i32, (tq, tk), 0) - lax.broadcasted_iota(i32, (tq, tk), 1)
    if select:
        picked = jnp.dot(sel_ref[...], e_ref[...], preferred_element_type=f32)
        valid = (dist >= 0) & (picked > 0.5)
    else:
        valid = (dist >= 0) & (dist < WINDOW)
    mask_scr[...] = jnp.where(valid, 0.0, NEG_INF)
    k = k_ref[...].astype(bf16)
    v = _with_ones(v_ref[...].astype(bf16))
    kb0 = (seq - d0) // LANES

    def score(h):
        return lax.dot_general(q_ref[:, h * HEAD_DIM:(h + 1) * HEAD_DIM], k, NT, preferred_element_type=f32)

    def add(h, rows, r0, kc):
        bias = g_ref[h, kb0 - r0 // LANES + kc, pl.ds(r0 % LANES, ROW_CHUNK), :]
        return bias, mask_scr[rows, kc * LANES:(kc + 1) * LANES]

    _flash_heads(NSA_HPG, tq, tk, score, add, lambda h: v, m_ref, l_ref, acc_ref, s_scr, p_scr, mx_scr, a_scr)

    @pl.when((fl & FLAG_LAST) != 0)
    def _():
        for h in range(NSA_HPG):
            sl = slice(h * HEAD_DIM, (h + 1) * HEAD_DIM)
            o = acc_ref[h] / l_ref[h]
            if not select:
                gate = gate_ref[...]
                o = (gate[:, h:h + 1] * oc_ref[:, sl].astype(f32)
                     + gate[:, NSA_HPG + h:NSA_HPG + h + 1] * os_ref[:, sl].astype(f32)
                     + gate[:, 2 * NSA_HPG + h:2 * NSA_HPG + h + 1] * o)
            o_ref[:, sl] = o.astype(o_ref.dtype)


def _nsa_band(q, kv, gtab, B, T, kcol, vcol, tk, sel=None, emat=None, merge=None, tq=Q_TILE):
    select = sel is not None
    nq, nk = T // tq, T // tk
    first = (lambda qi: 0) if select else (lambda qi: max(qi * tq - (WINDOW - 1), 0) // tk)
    tabs = _tile_schedule(nq, tq, tk, first)
    gw = NSA_HPG * HEAD_DIM
    in_specs = [
        pl.BlockSpec((tq, gw), lambda b, g, s, qt, kt, ft: (b * nq + qt[s], g)),
        pl.BlockSpec((tk, HEAD_DIM), lambda b, g, s, qt, kt, ft: (b * nk + kt[s], kcol * NSA_GROUPS + g)),
        pl.BlockSpec((tk, HEAD_DIM), lambda b, g, s, qt, kt, ft: (b * nk + kt[s], vcol * NSA_GROUPS + g)),
        pl.BlockSpec((NSA_HPG,) + gtab.shape[1:], lambda b, g, s, qt, kt, ft: (g, 0, 0, 0),
                     pipeline_mode=pl.Buffered(1)),
    ]
    args = [q, kv, kv, gtab]
    if select:
        in_specs += [pl.BlockSpec((tq, LANES), lambda b, g, s, qt, kt, ft: (b * nq + qt[s], g)),
                     pl.BlockSpec((LANES, tk), lambda b, g, s, qt, kt, ft: (0, kt[s]))]
        args += [sel, emat]
    else:
        o_other = pl.BlockSpec((tq, gw), lambda b, g, s, qt, kt, ft: (b * nq + qt[s], g))
        in_specs += [o_other, o_other, pl.BlockSpec((tq, LANES), lambda b, g, s, qt, kt, ft: (b * nq + qt[s], g))]
        args += list(merge)
    grid_spec = pltpu.PrefetchScalarGridSpec(
        num_scalar_prefetch=3,
        grid=(B, NSA_GROUPS, int(tabs[0].shape[0])),
        in_specs=in_specs,
        out_specs=pl.BlockSpec((tq, gw), lambda b, g, s, qt, kt, ft: (b * nq + qt[s], g)),
        scratch_shapes=_flash_scratch(NSA_HPG, tq, tk))
    return pl.pallas_call(
        functools.partial(_nsa_band_body, tq=tq, tk=tk, seq=T, select=select),
        grid_spec=grid_spec,
        out_shape=jax.ShapeDtypeStruct((B * T, NSA_HEADS * HEAD_DIM), bf16),
        compiler_params=_params(("parallel", "parallel", "arbitrary")),
        name="nsa_selected_attention" if select else "nsa_window_attention",
    )(*tabs, *args)


def _compress_body(x_ref, pe_ref, w1_ref, w2_ref, o_ref, ot_ref, *, nc):
    half = CMP_BLOCK // 2
    h_lo = jnp.zeros((nc, CMP_HIDDEN), f32)
    h_hi = jnp.zeros((nc, CMP_HIDDEN), f32)
    for p in range(half):
        rows = x_ref[pl.ds(p, nc, stride=CMP_STRIDE), :]
        h_lo += jnp.dot((rows + pe_ref[0, p:p + 1, :]).astype(bf16), w1_ref[0, p], preferred_element_type=f32)
        h_hi += jnp.dot((rows + pe_ref[0, half + p:half + p + 1, :]).astype(bf16), w1_ref[0, half + p],
                        preferred_element_type=f32)
    hid = h_lo + pltpu.roll(h_hi, nc - 1, 0)
    out = jnp.dot(jax.nn.gelu(hid).astype(bf16), w2_ref[0], preferred_element_type=f32)
    out = jnp.where(lax.broadcasted_iota(i32, out.shape, 0) < nc - 1, out, 0.0)
    o_ref[0, 0, 0] = out.astype(bf16)
    ot_ref[0, 0, 0] = out.T.astype(bf16)


def _compress(kv, pe, w1, w2, B, T):
    nc = T // CMP_STRIDE
    return pl.pallas_call(
        functools.partial(_compress_body, nc=nc),
        grid=(2, B, NSA_GROUPS),
        in_specs=[pl.BlockSpec((T, HEAD_DIM), lambda c, b, g: (b, c * NSA_GROUPS + g)),
                  pl.BlockSpec((1, CMP_BLOCK, HEAD_DIM), lambda c, b, g: (c, 0, 0)),
                  pl.BlockSpec((1, CMP_BLOCK, HEAD_DIM, CMP_HIDDEN), lambda c, b, g: (c, 0, 0, 0)),
                  pl.BlockSpec((1, CMP_HIDDEN, HEAD_DIM), lambda c, b, g: (c, 0, 0))],
        out_specs=[pl.BlockSpec((1, 1, 1, nc, HEAD_DIM), lambda c, b, g: (c, b, g, 0, 0)),
                   pl.BlockSpec((1, 1, 1, HEAD_DIM, nc), lambda c, b, g: (c, b, g, 0, 0))],
        out_shape=[jax.ShapeDtypeStruct((2, B, NSA_GROUPS, nc, HEAD_DIM), bf16),
                   jax.ShapeDtypeStruct((2, B, NSA_GROUPS, HEAD_DIM, nc), bf16)],
        compiler_params=_params(("parallel", "parallel", "parallel")),
        name="nsa_compress_mlp",
    )(kv, pe, w1, w2)


def _cmp_body(q_ref, kc_ref, vct_ref, cw_ref, ovt_ref, o_ref, sel_ref, *, nc, nq):
    tq = CMP_TQ
    qi = pl.program_id(2)
    t0 = qi * tq
    key_row = lax.broadcasted_iota(i32, (nc, tq), 0)
    q_col = lax.broadcasted_iota(i32, (nc, tq), 1)
    valid = (t0 + q_col - CMP_STRIDE * key_row - (CMP_BLOCK - 1)) >= 0
    kc = kc_ref[0, 0, 0]
    vct = vct_ref[0, 0, 0]
    psum = jnp.zeros((nc, tq), f32)
    for h in range(NSA_HPG):
        bias = jnp.concatenate([cw_ref[h, qi - c + nq] for c in range(nc // 8)], axis=0)
        s = lax.dot_general(kc, q_ref[:, h * HEAD_DIM:(h + 1) * HEAD_DIM], NT, preferred_element_type=f32)
        s = jnp.where(valid, s + bias, NEG_INF)
        m = jnp.max(s, axis=0, keepdims=True)
        e = jnp.where(valid, jnp.exp2(s - m), 0.0)
        l = jnp.sum(e, axis=0, keepdims=True)
        p = e / jnp.where(l > 0.0, l, 1.0)
        psum += p
        o_t = jnp.dot(vct, p.astype(bf16), preferred_element_type=f32)
        o_ref[:, h * HEAD_DIM:(h + 1) * HEAD_DIM] = o_t.T.astype(o_ref.dtype)

    imp = jnp.dot(ovt_ref[...], psum, preferred_element_type=f32, precision=lax.Precision.HIGHEST)
    nrow = imp.shape[0]
    n_slc = (nc * CMP_STRIDE) // SLC_BLOCK
    j = lax.broadcasted_iota(i32, (nrow, tq), 0)
    t = t0 + lax.broadcasted_iota(i32, (nrow, tq), 1)
    cur = jnp.right_shift(t, 6)
    ok = j * SLC_BLOCK <= t
    forced = (j == 0) | (j == cur) | (j == cur - 1)
    score = jnp.where(forced, FORCED_SCORE, jnp.where(ok, imp, NEG_INF))
    score = jnp.where(j < n_slc, score, PAD_SCORE)
    chosen = jnp.zeros((nrow, tq), f32)
    for _ in range(min(SLC_TOPK, n_slc)):
        top = jnp.max(score, axis=0, keepdims=True)
        first = jnp.min(jnp.where(score == top, j, nrow), axis=0, keepdims=True)
        hit = j == first
        chosen = jnp.where(hit, 1.0, chosen)
        score = jnp.where(hit, PAD_SCORE, score)
    chosen = jnp.where(ok & (j < n_slc), chosen, 0.0)
    sel_ref[...] = chosen.T.astype(sel_ref.dtype)


def _nsa_cmp(q, kc, vct, cw, ovt, B, T):
    nc = T // CMP_STRIDE
    nq = T // CMP_TQ
    gw = NSA_HPG * HEAD_DIM
    return pl.pallas_call(
        functools.partial(_cmp_body, nc=nc, nq=nq),
        grid=(B, NSA_GROUPS, nq),
        in_specs=[pl.BlockSpec((CMP_TQ, gw), lambda b, g, qi: (b * nq + qi, g)),
                  pl.BlockSpec((1, 1, 1, nc, HEAD_DIM), lambda b, g, qi: (0, b, g, 0, 0)),
                  pl.BlockSpec((1, 1, 1, HEAD_DIM, nc), lambda b, g, qi: (1, b, g, 0, 0)),
                  pl.BlockSpec((NSA_HPG, 2 * nq, 8, CMP_TQ), lambda b, g, qi: (g, 0, 0, 0)),
                  pl.BlockSpec(ovt.shape, lambda b, g, qi: (0, 0))],
        out_specs=[pl.BlockSpec((CMP_TQ, gw), lambda b, g, qi: (b * nq + qi, g)),
                   pl.BlockSpec((CMP_TQ, LANES), lambda b, g, qi: (b * nq + qi, g))],
        out_shape=[jax.ShapeDtypeStruct((B * T, NSA_HEADS * HEAD_DIM), bf16),
                   jax.ShapeDtypeStruct((B * T, NSA_GROUPS * LANES), bf16)],
        compiler_params=_params(("parallel", "parallel", "parallel")),
        name="nsa_compressed_attention_select",
    )(q, kc, vct, cw, ovt)


def _layer_norm(y, g, b):
    mu = jnp.mean(y, axis=-1, keepdims=True)
    var = jnp.mean(jnp.square(y - mu), axis=-1, keepdims=True)
    return (y - mu) * lax.rsqrt(var + LN_EPS) * g + b


def _branch_merge_body(a_ref, b_ref, wa_ref, wb_ref, ga_ref, gb_ref, o_ref):
    ya = jnp.dot(a_ref[...], wa_ref[...], preferred_element_type=f32)
    yb = jnp.dot(b_ref[...], wb_ref[...], preferred_element_type=f32)
    o_ref[...] = (ga_ref[...] * ya + gb_ref[...] * yb).astype(o_ref.dtype)


def _branch_merge(a, b, wa, wb, mg, tm, tn):
    M, K = a.shape
    Nc = wa.shape[1]
    nj = Nc // tn
    return pl.pallas_call(
        _branch_merge_body,
        grid=(M // tm, nj),
        in_specs=[pl.BlockSpec((tm, K), lambda i, j: (i, 0)), pl.BlockSpec((tm, K), lambda i, j: (i, 0)),
                  pl.BlockSpec((K, tn), lambda i, j: (0, j)), pl.BlockSpec((K, tn), lambda i, j: (0, j)),
                  pl.BlockSpec((tm, tn), lambda i, j: (i, j)), pl.BlockSpec((tm, tn), lambda i, j: (i, nj + j))],
        out_specs=pl.BlockSpec((tm, tn), lambda i, j: (i, j)),
        out_shape=jax.ShapeDtypeStruct((M, Nc), bf16),
        compiler_params=_params(("parallel", "parallel")),
        name="branch_proj_merge",
    )(a, b, wa, wb, mg, mg)


def _proj_ln_body(z_ref, w_ref, x_ref, g_ref, b_ref, o_ref):
    y = jnp.dot(z_ref[...], w_ref[...], preferred_element_type=f32)
    o_ref[...] = _layer_norm(ALPHA * x_ref[...] + y, g_ref[...], b_ref[...])


def _proj_ln(z, w, x, g, b, tm):
    M, K = z.shape
    D = w.shape[1]
    vec = pl.BlockSpec((1, D), lambda i: (0, 0))
    return pl.pallas_call(
        _proj_ln_body,
        grid=(M // tm,),
        in_specs=[pl.BlockSpec((tm, K), lambda i: (i, 0)), pl.BlockSpec((K, D), lambda i: (0, 0)),
                  pl.BlockSpec((tm, D), lambda i: (i, 0)), vec, vec],
        out_specs=pl.BlockSpec((tm, D), lambda i: (i, 0)),
        out_shape=jax.ShapeDtypeStruct((M, D), f32),
        compiler_params=_params(("parallel",)),
        name="mix_out_proj_layernorm",
    )(z, w, x, g, b)


def _mem_attn_body(x_ref, wq_ref, k_ref, v_ref, wo_ref, g_ref, b_ref, o_ref, ob_ref):
    x = x_ref[...]
    q = (jnp.dot(x.astype(bf16), wq_ref[...], preferred_element_type=f32) * HEAD_DIM ** -0.5).astype(bf16)
    heads = []
    for h in range(MEM_HEADS):
        sl = slice(h * HEAD_DIM, (h + 1) * HEAD_DIM)
        s = lax.dot_general(q[:, sl], k_ref[0, :, sl], NT, preferred_element_type=f32)
        e = jnp.exp(s - jnp.max(s, axis=-1, keepdims=True))
        p = e / jnp.sum(e, axis=-1, keepdims=True)
        heads.append(jnp.dot(p.astype(bf16), v_ref[0, :, sl], preferred_element_type=f32).astype(bf16))
    y = jnp.dot(jnp.concatenate(heads, axis=1), wo_ref[...], preferred_element_type=f32)
    out = _layer_norm(ALPHA * x + y, g_ref[...], b_ref[...])
    o_ref[...] = out
    ob_ref[...] = out.astype(bf16)


def _mem_attn(x, wq, km, vm, wo, g, b, T, tm):
    M, D = x.shape
    nt = T // tm
    W = wq.shape[1]
    vec = pl.BlockSpec((1, D), lambda i: (0, 0))
    blk = pl.BlockSpec((tm, D), lambda i: (i, 0))
    return pl.pallas_call(
        _mem_attn_body,
        grid=(M // tm,),
        in_specs=[blk, pl.BlockSpec((D, W), lambda i: (0, 0)),
                  pl.BlockSpec((1, N_MEM, W), lambda i: (i // nt, 0, 0)),
                  pl.BlockSpec((1, N_MEM, W), lambda i: (i // nt, 0, 0)),
                  pl.BlockSpec((W, D), lambda i: (0, 0)), vec, vec],
        out_specs=[blk, blk],
        out_shape=[jax.ShapeDtypeStruct((M, D), f32), jax.ShapeDtypeStruct((M, D), bf16)],
        compiler_params=_params(("parallel",)),
        name="memory_cross_attention_layernorm",
    )(x, wq, km, vm, wo, g, b)


def _add_ln_body(x_ref, y_ref, g_ref, b_ref, o_ref):
    o_ref[...] = _layer_norm(ALPHA * x_ref[...] + y_ref[...], g_ref[...], b_ref[...])


def _add_ln(x, y, g, b, tm):
    M, D = x.shape
    vec = pl.BlockSpec((1, D), lambda i: (0, 0))
    blk = pl.BlockSpec((tm, D), lambda i: (i, 0))
    return pl.pallas_call(
        _add_ln_body,
        grid=(M // tm,),
        in_specs=[blk, blk, vec, vec],
        out_specs=blk,
        out_shape=jax.ShapeDtypeStruct((M, D), f32),
        compiler_params=_params(("parallel",)),
        name="residual_layernorm",
    )(x, y, g, b)


def _topk_rows(s, ids, k):
    n = s.shape[1]
    slot = lax.broadcasted_iota(i32, (k, n), 0)
    vals = jnp.zeros((k, n), f32)
    inds = jnp.zeros((k, n), f32)
    for r in range(k):
        top = jnp.max(s, axis=0, keepdims=True)
        first = jnp.min(jnp.where(s == top, ids, PAD_ID), axis=0, keepdims=True)
        vals = jnp.where(slot == r, top, vals)
        inds = jnp.where(slot == r, first, inds)
        s = jnp.where(ids == first, PAD_SCORE, s)
    return vals, inds


def _pick_rows(table, sel):
    out = jnp.zeros(sel.shape, table.dtype)
    for a in range(table.shape[0]):
        out = jnp.where(sel == a, table[a:a + 1, :], out)
    return out


CAND_COUNTS = tuple(PEER_TOPK // (a + 1) for a in range(PEER_TOPK))
CAND_ROWS = -(-sum(CAND_COUNTS) // 8) * 8


def _cand_ids(tn):
    ids = [a * PEER_TOPK + b for a, nb in enumerate(CAND_COUNTS) for b in range(nb)]
    ids = np.array(ids + [PAD_ID] * (CAND_ROWS - len(ids)), np.float32)
    return jnp.asarray(np.broadcast_to(ids[:, None], (CAND_ROWS, tn)))


def _peer_route_body(q_ref, keys_ref, ids_ref, e1_ref, e2_ref, gt_ref, e1_scr, e2_scr, g_scr, cand_scr):
    k1 = keys_ref[0].astype(bf16)
    k2 = keys_ref[1].astype(bf16)
    tn = q_ref.shape[0]
    key_ids = lax.broadcasted_iota(i32, (PEER_N_KEYS, tn), 0).astype(f32)
    cand_scr[...] = jnp.full(cand_scr.shape, PAD_SCORE, f32)
    for h in range(PEER_HEADS):
        q1 = q_ref[:, (2 * h) * LANES:(2 * h + 1) * LANES]
        q2 = q_ref[:, (2 * h + 1) * LANES:(2 * h + 2) * LANES]
        s1 = lax.dot_general(k1, q1, NT, preferred_element_type=f32)
        s2 = lax.dot_general(k2, q2, NT, preferred_element_type=f32)
        v1, i1 = _topk_rows(s1, key_ids, PEER_TOPK)
        v2, i2 = _topk_rows(s2, key_ids, PEER_TOPK)
        off = 0
        for a, nb in enumerate(CAND_COUNTS):
            cand_scr[off:off + nb, :] = v1[a:a + 1, :] + v2[0:nb, :]
            off += nb
        top_s, flat = _topk_rows(cand_scr[...], ids_ref[...], PEER_TOPK)
        flat = flat.astype(i32)
        e = jnp.exp(top_s - top_s[0:1, :])
        rows = slice(h * PEER_TOPK, (h + 1) * PEER_TOPK)
        g_scr[rows, :] = e / jnp.sum(e, axis=0, keepdims=True)
        e1_scr[rows, :] = _pick_rows(i1, jnp.right_shift(flat, 4)).astype(i32)
        e2_scr[rows, :] = _pick_rows(i2, jnp.bitwise_and(flat, PEER_TOPK - 1)).astype(i32)
    e1_ref[...] = e1_scr[...].T
    e2_ref[...] = e2_scr[...].T
    gt_ref[...] = g_scr[...].T


def _peer_route(q, keys, tn=LANES):
    M, W = q.shape
    out = pl.BlockSpec((tn, PEER_HK), lambda i: (i, 0))
    return pl.pallas_call(
        _peer_route_body,
        grid=(M // tn,),
        in_specs=[pl.BlockSpec((tn, W), lambda i: (i, 0)), pl.BlockSpec(keys.shape, lambda i: (0, 0, 0)),
                  pl.BlockSpec((CAND_ROWS, tn), lambda i: (0, 0))],
        out_specs=[out, out, out],
        out_shape=[jax.ShapeDtypeStruct((M, PEER_HK), i32), jax.ShapeDtypeStruct((M, PEER_HK), i32),
                   jax.ShapeDtypeStruct((M, PEER_HK), f32)],
        scratch_shapes=[pltpu.VMEM((PEER_HK, tn), i32), pltpu.VMEM((PEER_HK, tn), i32), pltpu.VMEM((PEER_HK, tn), f32),
                        pltpu.VMEM((CAND_ROWS, tn), f32)],
        compiler_params=_params(("parallel",)),
        name="peer_product_key_routing",
    )(q, keys, _cand_ids(tn))


def _peer_weights_body(e1_ref, e2_ref, g_ref, w_ref, *, tn):
    sub = lax.broadcasted_iota(i32, (PEER_N_KEYS, PEER_HK), 0)

    def one(n, carry):
        r1 = e1_ref[pl.ds(n, 1), :]
        r2 = e2_ref[pl.ds(n, 1), :]
        g = g_ref[pl.ds(n, 1), :]
        a_t = (sub == r1).astype(bf16)
        b_t = jnp.where(sub == r2, g, 0.0).astype(bf16)
        w_ref[n] = lax.dot_general(a_t, b_t, NT, preferred_element_type=f32).astype(w_ref.dtype)
        return carry

    lax.fori_loop(0, tn, one, 0, unroll=8)


def _peer_weights(e1, e2, g, tn=64):
    M = e1.shape[0]
    blk = pl.BlockSpec((tn, PEER_HK), lambda i: (i, 0))
    return pl.pallas_call(
        functools.partial(_peer_weights_body, tn=tn),
        grid=(M // tn,),
        in_specs=[blk, blk, blk],
        out_specs=pl.BlockSpec((tn, PEER_N_KEYS, PEER_N_KEYS), lambda i: (i, 0, 0)),
        out_shape=jax.ShapeDtypeStruct((M, PEER_N_KEYS, PEER_N_KEYS), bf16),
        compiler_params=_params(("parallel",)),
        name="peer_routing_weights",
    )(e1, e2, g)


def _peer_dense_body(x_ref, w_ref, u_ref, v_ref, o_ref, *, rows):
    @pl.when(pl.program_id(1) == 0)
    def _():
        o_ref[...] = jnp.zeros(o_ref.shape, f32)

    act = jax.nn.gelu(lax.dot_general(x_ref[...], u_ref[...].astype(bf16), NT, preferred_element_type=f32))
    w = jnp.concatenate([w_ref[r] for r in range(rows)], axis=1).astype(f32)
    o_ref[...] += jnp.dot((w * act).astype(bf16), v_ref[...].astype(bf16), preferred_element_type=f32)


def _peer_dense(xb, w, u, v, tm=1024, rows=4):
    M, D = xb.shape
    te = rows * PEER_N_KEYS
    return pl.pallas_call(
        functools.partial(_peer_dense_body, rows=rows),
        grid=(M // tm, PEER_N_KEYS // rows),
        in_specs=[pl.BlockSpec((tm, D), lambda i, e: (i, 0)),
                  pl.BlockSpec((rows, tm, PEER_N_KEYS), lambda i, e: (e, i, 0)),
                  pl.BlockSpec((te, D), lambda i, e: (e, 0)), pl.BlockSpec((te, D), lambda i, e: (e, 0))],
        out_specs=pl.BlockSpec((tm, D), lambda i, e: (i, 0)),
        out_shape=jax.ShapeDtypeStruct((M, D), f32),
        compiler_params=_params(("parallel", "arbitrary")),
        name="peer_dense_experts",
    )(xb, w, u, v)


def _t5_bucket(dist):
    dist = jnp.maximum(dist, 0)
    n_log = REL_BUCKETS - REL_MAX_EXACT
    large = REL_MAX_EXACT + (jnp.log(jnp.maximum(dist, 1).astype(f32) / REL_MAX_EXACT)
                             / math.log(REL_MAX_DIST / REL_MAX_EXACT) * n_log).astype(i32)
    large = jnp.minimum(large, REL_BUCKETS - 1)
    return jnp.where(dist < REL_MAX_EXACT, dist, large)


def _position_tables(rel_bias, T):
    by_dist = (rel_bias[_t5_bucket(jnp.arange(T))] * LOG2E).T.astype(f32)
    rev = jnp.pad(by_dist[:, ::-1], ((0, 0), (1, K_TILE - 1)), mode="edge")[:, None, :]
    shift = (CMP_BLOCK - 1) + CMP_STRIDE * 7
    ext = jnp.pad(by_dist, ((0, 0), (T + shift, 0)), mode="edge")
    rows = [ext[:, CMP_STRIDE * (7 - r):CMP_STRIDE * (7 - r) + 2 * T] for r in range(8)]
    cmpw = jnp.stack(rows, axis=1).reshape(NSA_HEADS, 8, 2 * T // CMP_TQ, CMP_TQ)
    return rev, jnp.transpose(cmpw, (0, 2, 1, 3))


def _rope_slot_tables(T):
    inv = ROPE_THETA ** (-jnp.arange(0, MLA_ROPE, 2, dtype=f32) / MLA_ROPE)
    ang = jnp.arange(T, dtype=f32)[:, None] * inv[None, :]
    cos, sin = jnp.cos(ang), jnp.sin(ang)
    z = jnp.zeros_like(cos)
    ta = jnp.concatenate([cos, cos, z, z], axis=1)
    tb = jnp.concatenate([-sin, z, z, z], axis=1)
    tc = jnp.concatenate([z, sin, z, z], axis=1)
    return ta, tb, tc


def _layer(x, mem, rel_bias, w_in, cmp_k_pe, cmp_k_w1, cmp_k_w2, cmp_v_pe, cmp_v_w1, cmp_v_w2,
           mla_kv_norm, mla_w_uk, mla_w_uv, w_branch_nsa, w_branch_mla, w_mix_out, ln_mix_g, ln_mix_b,
           mem_w_q, mem_w_k, mem_w_v, mem_w_o, ln_mem_g, ln_mem_b,
           peer_w_query, peer_sub_keys, peer_u, peer_v, ln_ffn_g, ln_ffn_b, ropes):
    B, T, D = x.shape
    N = B * T
    x2d = x.reshape(N, D)
    xb = x2d.astype(bf16)
    row = lambda a: a.reshape(1, -1)

    gate_cols = np.array([[br * NSA_HEADS + g * NSA_HPG + h for br in range(3) for h in range(NSA_HPG)]
                          for g in range(NSA_GROUPS)])
    w_gate = jnp.pad(w_in[:, OFF_NSA_GATE:OFF_MLA_Q][:, gate_cols],
                     ((0, 0), (0, 0), (0, LANES - 3 * NSA_HPG))).reshape(D, NSA_GROUPS * LANES)
    w_mla_q = jnp.pad(w_in[:, OFF_MLA_Q:OFF_MLA_KV].reshape(D, MLA_HEADS, MLA_NOPE + MLA_ROPE),
                      ((0, 0), (0, 0), (0, MLA_SLOT - MLA_NOPE - MLA_ROPE))).reshape(D, MLA_HEADS * MLA_SLOT)
    w_mla_kv = jnp.pad(w_in[:, OFF_MLA_KV:OFF_MERGE], ((0, 0), (0, LANES - MLA_ROPE)))
    segs = [w_mla_kv, jnp.zeros((D, LANES), f32), w_gate,
            w_in[:, :OFF_NSA_KV], w_mla_q, w_in[:, OFF_MERGE:], w_in[:, OFF_NSA_KV:OFF_NSA_GATE]]
    starts = np.concatenate([[0], np.cumsum([s.shape[1] for s in segs])])
    col = lambda i: (int(starts[i]), int(segs[i].shape[1]))
    w_all = jnp.concatenate(segs, axis=1).astype(bf16)

    mla_ckv = _mm(xb, w_all, f32, 1024, MLA_KV_RANK + LANES, name="mla_kv_proj", cols=col(0))
    gates = _mm(xb, w_all, f32, 1024, NSA_GROUPS * LANES, act="sigmoid", name="nsa_gate_proj", cols=col(2))
    nsa_q = _mm(xb, w_all, bf16, 1024, 1024, scale=HEAD_DIM ** -0.5 * LOG2E, name="nsa_q_proj", cols=col(3))
    mla_q = _mla_q(xb, w_all, col(4), ropes, T, 1024, 1024)
    merge = _mm(xb, w_all, f32, 1024, 1024, act="sigmoid", name="merge_gate_proj", cols=col(5))
    nsa_kv = _mm(xb, w_all, f32, 1024, 512, name="nsa_kv_proj", cols=col(6))

    slc_tk = min(K_TILE, T)
    rev_tab, cmp_tab = _position_tables(rel_bias, T)
    bias_tab = _bias_table(rev_tab)
    pe = jnp.stack([cmp_k_pe, cmp_v_pe])
    w1 = jnp.stack([cmp_k_w1, cmp_v_w1]).reshape(2, CMP_BLOCK, HEAD_DIM, CMP_HIDDEN).astype(bf16)
    w2 = jnp.stack([cmp_k_w2, cmp_v_w2]).astype(bf16)
    kvc, kvc_t = _compress(nsa_kv, pe, w1, w2, B, T)
    nc = T // CMP_STRIDE
    n_slc = T // SLC_BLOCK
    cs = np.arange(nc)[None, :] * CMP_STRIDE
    ss = np.arange(LANES)[:, None] * SLC_BLOCK
    ovt = ((cs < ss + SLC_BLOCK) & (cs + CMP_BLOCK > ss) & (np.arange(nc)[None, :] < nc - 1)
           & (np.arange(LANES)[:, None] < n_slc)).astype(np.float32)
    o_cmp, sel = _nsa_cmp(nsa_q, kvc, kvc_t, cmp_tab, jnp.asarray(ovt), B, T)
    emat = (np.arange(T)[None, :] // SLC_BLOCK == np.arange(LANES)[:, None]).astype(np.float32)
    o_slc = _nsa_band(nsa_q, nsa_kv, bias_tab, B, T, 2, 3, slc_tk, sel=sel, emat=jnp.asarray(emat, dtype=bf16))
    o_nsa = _nsa_band(nsa_q, nsa_kv, bias_tab, B, T, 4, 5, WIN_TILE, merge=(o_cmp, o_slc, gates), tq=WIN_TILE)

    mla_k, mla_v = _mla_kv(mla_ckv, row(mla_kv_norm),
                           jnp.asarray(mla_w_uk, bf16), jnp.asarray(mla_w_uv, bf16), ropes, T, 512)
    o_mla = _mla_attn(mla_q, mla_k, mla_v, B, T)

    z = _branch_merge(o_nsa, o_mla, w_branch_nsa.astype(bf16), w_branch_mla.astype(bf16), merge, 1024, 512)
    x1 = _proj_ln(z, w_mix_out.astype(bf16), x2d, row(ln_mix_g), row(ln_mix_b), 512)

    mem2d = mem.reshape(B * N_MEM, D).astype(bf16)
    km = _mm(mem2d, mem_w_k.astype(bf16), bf16, B * N_MEM, 256, name="mem_k_proj").reshape(B, N_MEM, -1)
    vm = _mm(mem2d, mem_w_v.astype(bf16), bf16, B * N_MEM, 256, name="mem_v_proj").reshape(B, N_MEM, -1)
    x2, x2b = _mem_attn(x1, mem_w_q.astype(bf16), km, vm, mem_w_o.astype(bf16), row(ln_mem_g), row(ln_mem_b), T, 512)

    pq = _mm(x2b, peer_w_query.astype(bf16), bf16, 1024, 512, name="peer_query_proj")
    e1, e2, gt = _peer_route(pq, peer_sub_keys)
    wts = jnp.transpose(_peer_weights(e1, e2, gt), (1, 0, 2))
    y = _peer_dense(x2b, wts, peer_u, peer_v)
    x3 = _add_ln(x2, y, row(ln_ffn_g), row(ln_ffn_b), 512)
    return x3.reshape(B, T, D)


def kernel(x, mem, rel_bias, w_in, cmp_k_pe, cmp_k_w1, cmp_k_w2, cmp_v_pe, cmp_v_w1, cmp_v_w2, mla_kv_norm, mla_w_uk, mla_w_uv, w_branch_nsa, w_branch_mla, w_mix_out, ln_mix_g, ln_mix_b, mem_w_q, mem_w_k, mem_w_v, mem_w_o, ln_mem_g, ln_mem_b, peer_w_query, peer_sub_keys, peer_u, peer_v, ln_ffn_g, ln_ffn_b):
    ropes = _rope_slot_tables(x.shape[1])
    for l in range(DEPTH):
        x = _layer(x, mem, rel_bias, w_in[l], cmp_k_pe[l], cmp_k_w1[l], cmp_k_w2[l], cmp_v_pe[l], cmp_v_w1[l],
                   cmp_v_w2[l], mla_kv_norm[l], mla_w_uk[l], mla_w_uv[l], w_branch_nsa[l], w_branch_mla[l],
                   w_mix_out[l], ln_mix_g[l], ln_mix_b[l], mem_w_q[l], mem_w_k[l], mem_w_v[l], mem_w_o[l],
                   ln_mem_g[l], ln_mem_b[l], peer_w_query[l], peer_sub_keys[l], peer_u[l], peer_v[l],
                   ln_ffn_g[l], ln_ffn_b[l], ropes)
    return x
```

```python
import functools
import math

import numpy as np
import jax
import jax.numpy as jnp
from jax import lax
from jax.experimental import pallas as pl
from jax.experimental.pallas import tpu as pltpu

f32 = jnp.float32
bf16 = jnp.bfloat16
i32 = jnp.int32

D_MODEL = 2048
N_MEM = 256
NSA_HEADS = 16
NSA_GROUPS = 2
NSA_HPG = NSA_HEADS // NSA_GROUPS
HEAD_DIM = 128
CMP_BLOCK = 32
CMP_STRIDE = 16
CMP_HIDDEN = 256
SLC_BLOCK = 64
SLC_TOPK = 16
WINDOW = 512
FORCED_SCORE = 1e4
MLA_HEADS = 16
MLA_NOPE = 128
MLA_ROPE = 64
MLA_KV_RANK = 512
MLA_SLOT = 256
ROPE_THETA = 10000.0
REL_BUCKETS = 32
REL_MAX_EXACT = 16
REL_MAX_DIST = 4096
MEM_HEADS = 4
PEER_HEADS = 8
PEER_N_KEYS = 128
PEER_TOPK = 16
PEER_HK = PEER_HEADS * PEER_TOPK
DEPTH = 1
ALPHA = (2.0 * DEPTH) ** 0.25
NEG_INF = -1e30
M_INIT = -1e29
PAD_SCORE = -3e38
PAD_ID = 1e9
LOG2E = math.log2(math.e)
LN_EPS = 1e-5
RMS_EPS = 1e-6

NSA_Q_COLS = NSA_HEADS * HEAD_DIM
NSA_KV_COLS = NSA_GROUPS * HEAD_DIM
OFF_NSA_KV = NSA_Q_COLS
OFF_NSA_GATE = OFF_NSA_KV + 6 * NSA_KV_COLS
OFF_MLA_Q = OFF_NSA_GATE + 3 * NSA_HEADS
OFF_MLA_KV = OFF_MLA_Q + MLA_HEADS * (MLA_NOPE + MLA_ROPE)
OFF_MERGE = OFF_MLA_KV + MLA_KV_RANK + MLA_ROPE

LANES = 128
Q_TILE = 512
K_TILE = 512
WIN_TILE = 256
ROW_CHUNK = 32
CMP_TQ = 128
VMEM_LIMIT = 56 * 1024 * 1024

NT = (((1,), (1,)), ((), ()))


def _params(sem):
    return pltpu.CompilerParams(dimension_semantics=sem, vmem_limit_bytes=VMEM_LIMIT)


def _mm_body(a_ref, b_ref, o_ref, *, scale, act):
    acc = jnp.dot(a_ref[...].astype(bf16), b_ref[...], preferred_element_type=f32)
    if scale != 1.0:
        acc = acc * scale
    if act == "sigmoid":
        acc = jax.nn.sigmoid(acc)
    o_ref[...] = acc.astype(o_ref.dtype)


def _mm(a, b, out_dtype, tm, tn, scale=1.0, act=None, name="mm", cols=None):
    M, K = a.shape
    c0, Nc = (0, b.shape[1]) if cols is None else cols
    assert c0 % tn == 0 and Nc % tn == 0
    j0 = c0 // tn
    return pl.pallas_call(
        functools.partial(_mm_body, scale=scale, act=act),
        grid=(M // tm, Nc // tn),
        in_specs=[pl.BlockSpec((tm, K), lambda i, j: (i, 0)), pl.BlockSpec((K, tn), lambda i, j: (0, j0 + j))],
        out_specs=pl.BlockSpec((tm, tn), lambda i, j: (i, j)),
        out_shape=jax.ShapeDtypeStruct((M, Nc), out_dtype),
        compiler_params=_params(("parallel", "parallel")),
        name=name,
    )(a, b)


def _mm_slabs_body(a_ref, b_ref, o_ref):
    acc = jnp.dot(a_ref[...], b_ref[...], preferred_element_type=f32)
    for c in range(o_ref.shape[0]):
        o_ref[c] = acc[:, c * LANES:(c + 1) * LANES]


def _mm_slabs(a, b, cols, tm, tn, name):
    M, K = a.shape
    c0, Nc = cols
    assert c0 % tn == 0 and Nc % tn == 0
    j0 = c0 // tn
    return pl.pallas_call(
        _mm_slabs_body,
        grid=(M // tm, Nc // tn),
        in_specs=[pl.BlockSpec((tm, K), lambda i, j: (i, 0)), pl.BlockSpec((K, tn), lambda i, j: (0, j0 + j))],
        out_specs=pl.BlockSpec((tn // LANES, tm, LANES), lambda i, j: (j, i, 0)),
        out_shape=jax.ShapeDtypeStruct((Nc // LANES, M, LANES), f32),
        compiler_params=_params(("parallel", "parallel")),
        name=name,
    )(a, b)


def _rope_slot(y, ta, tb, tc):
    return y * ta + pltpu.roll(y, LANES - 32, 1) * tb + pltpu.roll(y, 32, 1) * tc


def _mlaq_body(a_ref, b_ref, ta_ref, tb_ref, tc_ref, o_ref, *, scale, nslot):
    acc = jnp.dot(a_ref[...], b_ref[...], preferred_element_type=f32)
    ta, tb, tc = ta_ref[...], tb_ref[...], tc_ref[...]
    for s in range(nslot):
        lo = s * MLA_SLOT
        o_ref[:, lo:lo + MLA_NOPE] = (acc[:, lo:lo + MLA_NOPE] * scale).astype(o_ref.dtype)
        rot = _rope_slot(acc[:, lo + MLA_NOPE:lo + MLA_SLOT], ta, tb, tc)
        o_ref[:, lo + MLA_NOPE:lo + MLA_SLOT] = (rot * scale).astype(o_ref.dtype)


def _mla_q(xb, w, cols, ropes, T, tm, tn):
    M, K = xb.shape
    c0, Nc = cols
    assert c0 % tn == 0 and Nc % tn == 0
    j0 = c0 // tn
    nt = T // tm
    tab = pl.BlockSpec((tm, LANES), lambda i, j: (i % nt, 0))
    return pl.pallas_call(
        functools.partial(_mlaq_body, scale=(MLA_NOPE + MLA_ROPE) ** -0.5 * LOG2E, nslot=tn // MLA_SLOT),
        grid=(M // tm, Nc // tn),
        in_specs=[pl.BlockSpec((tm, K), lambda i, j: (i, 0)), pl.BlockSpec((K, tn), lambda i, j: (0, j0 + j)),
                  tab, tab, tab],
        out_specs=pl.BlockSpec((tm, tn), lambda i, j: (i, j)),
        out_shape=jax.ShapeDtypeStruct((M, Nc), bf16),
        compiler_params=_params(("parallel", "parallel")),
        name="mla_q_proj_rope",
    )(xb, w, *ropes)


def _mlakv_body(c_ref, g_ref, wuk_ref, wuv_ref, ta_ref, tb_ref, tc_ref, k_ref, v_ref):
    x = c_ref[...]
    c = x[:, :MLA_KV_RANK]
    ms = jnp.mean(c * c, axis=-1, keepdims=True)
    cn = (c * lax.rsqrt(ms + RMS_EPS) * g_ref[...]).astype(bf16)
    kpe = _rope_slot(x[:, MLA_KV_RANK:], ta_ref[...], tb_ref[...], tc_ref[...]).astype(bf16)
    kn = jnp.dot(cn, wuk_ref[...], preferred_element_type=f32).astype(bf16)
    v_ref[...] = jnp.dot(cn, wuv_ref[...], preferred_element_type=f32).astype(bf16)
    for h in range(MLA_HEADS):
        k_ref[:, h * MLA_SLOT:h * MLA_SLOT + MLA_NOPE] = kn[:, h * MLA_NOPE:(h + 1) * MLA_NOPE]
        k_ref[:, h * MLA_SLOT + MLA_NOPE:(h + 1) * MLA_SLOT] = kpe


def _mla_kv(ckv, g, wuk, wuv, ropes, T, tm):
    M, K = ckv.shape
    nt = T // tm
    tab = pl.BlockSpec((tm, LANES), lambda i: (i % nt, 0))
    full = lambda shape: pl.BlockSpec(shape, lambda i: (0,) * len(shape))
    return pl.pallas_call(
        _mlakv_body,
        grid=(M // tm,),
        in_specs=[pl.BlockSpec((tm, K), lambda i: (i, 0)), full(g.shape), full(wuk.shape), full(wuv.shape), tab, tab, tab],
        out_specs=[pl.BlockSpec((tm, MLA_HEADS * MLA_SLOT), lambda i: (i, 0)),
                   pl.BlockSpec((tm, MLA_HEADS * HEAD_DIM), lambda i: (i, 0))],
        out_shape=[jax.ShapeDtypeStruct((M, MLA_HEADS * MLA_SLOT), bf16),
                   jax.ShapeDtypeStruct((M, MLA_HEADS * HEAD_DIM), bf16)],
        compiler_params=_params(("parallel",)),
        name="mla_kv_prep",
    )(ckv, g, wuk, wuv, *ropes)


FLAG_FIRST, FLAG_LAST, FLAG_CAUSAL = 1, 2, 4


def _tile_schedule(nq, tq, tk, first_key_tile):
    qs, ks, fl = [], [], []
    for qi in range(nq):
        lo = first_key_tile(qi)
        hi = (qi * tq + tq - 1) // tk
        for ki in range(lo, hi + 1):
            qs.append(qi)
            ks.append(ki)
            fl.append((FLAG_FIRST if ki == lo else 0) | (FLAG_LAST if ki == hi else 0)
                      | (FLAG_CAUSAL if ki * tk + tk - 1 > qi * tq else 0))
    return tuple(jnp.asarray(np.array(a, np.int32)) for a in (qs, ks, fl))


def _flash_scratch(heads, tq, tk):
    stat = pltpu.VMEM((heads, tq, LANES), f32)
    return [stat, stat, pltpu.VMEM((heads, tq, HEAD_DIM), f32),
            pltpu.VMEM((3, tq, tk), f32), pltpu.VMEM((2, tq, tk), bf16), pltpu.VMEM((2, tq, LANES), f32),
            pltpu.VMEM((2, tq, LANES), f32), pltpu.VMEM((tq, tk), f32)]


def _flash_init(m_ref, l_ref, acc_ref):
    m_ref[...] = jnp.full(m_ref.shape, M_INIT, f32)
    l_ref[...] = jnp.zeros(l_ref.shape, f32)
    acc_ref[...] = jnp.zeros(acc_ref.shape, f32)


def _flash_heads(nheads, tq, tk, score_fn, add_fn, value_fn, m_ref, l_ref, acc_ref, s_scr, p_scr, mx_scr, a_scr):
    nkc = tk // LANES
    chunks = [(pl.ds(c * ROW_CHUNK, ROW_CHUNK), c * ROW_CHUNK) for c in range(tq // ROW_CHUNK)]

    def scores(h):
        s_scr[h % 3] = score_fn(h)

    def pass_a(h):
        s_ref = s_scr.at[h % 3]
        for rows, r0 in chunks:
            mx = None
            for kc in range(nkc):
                cols = slice(kc * LANES, (kc + 1) * LANES)
                s = s_ref[rows, cols]
                if add_fn is not None:
                    for term in add_fn(h, rows, r0, kc):
                        s = s + term
                    s_ref[rows, cols] = s
                mx = s if mx is None else jnp.maximum(mx, s)
            mx_scr[h % 2, rows, :] = mx
        m_prev = m_ref[h]
        m_new = jnp.maximum(m_prev, jnp.max(mx_scr[h % 2], axis=-1, keepdims=True))
        a_scr[h % 2] = jnp.exp2(m_prev - m_new)
        m_ref[h] = m_new

    def pass_b(h):
        s_ref = s_scr.at[h % 3]
        for rows, _ in chunks:
            m_rows = m_ref[h, rows, :]
            for kc in range(nkc):
                cols = slice(kc * LANES, (kc + 1) * LANES)
                p_scr[h % 2, rows, cols] = jnp.exp2(s_ref[rows, cols] - m_rows).astype(bf16)
        pv = jnp.dot(p_scr[h % 2], value_fn(h), preferred_element_type=f32)
        alpha = a_scr[h % 2]
        acc_ref[h] = alpha * acc_ref[h] + pv[:, :HEAD_DIM]
        l_ref[h] = alpha * l_ref[h] + pv[:, HEAD_DIM:]

    scores(0)
    for h in range(nheads):
        if h + 1 < nheads:
            scores(h + 1)
        pass_a(h)
        if h >= 1:
            pass_b(h - 1)
    pass_b(nheads - 1)


def _with_ones(v):
    return jnp.concatenate([v, jnp.ones((v.shape[0], LANES), v.dtype)], axis=1)


def _mla_attn_body(qi_tab, ki_tab, fl_tab, q_ref, k_ref, v_ref, o_ref, m_ref, l_ref, acc_ref,
                   s_scr, p_scr, mx_scr, a_scr, mask_scr, *, hb, tq, tk):
    st = pl.program_id(2)
    qi, ki, fl = qi_tab[st], ki_tab[st], fl_tab[st]

    pl.when((fl & FLAG_FIRST) != 0)(lambda: _flash_init(m_ref, l_ref, acc_ref))

    def score(h):
        qk = slice(h * MLA_SLOT, (h + 1) * MLA_SLOT)
        return lax.dot_general(q_ref[:, qk], k_ref[:, qk], NT, preferred_element_type=f32)

    def value(h):
        return _with_ones(v_ref[:, h * HEAD_DIM:(h + 1) * HEAD_DIM])

    def tile(causal):
        add = None
        if causal:
            row = qi * tq + lax.broadcasted_iota(i32, (tq, tk), 0)
            col = ki * tk + lax.broadcasted_iota(i32, (tq, tk), 1)
            mask_scr[...] = jnp.where(row >= col, 0.0, NEG_INF)
            add = lambda h, rows, r0, kc: (mask_scr[rows, kc * LANES:(kc + 1) * LANES],)
        _flash_heads(hb, tq, tk, score, add, value, m_ref, l_ref, acc_ref, s_scr, p_scr, mx_scr, a_scr)

    pl.when((fl & FLAG_CAUSAL) != 0)(lambda: tile(True))
    pl.when((fl & FLAG_CAUSAL) == 0)(lambda: tile(False))

    @pl.when((fl & FLAG_LAST) != 0)
    def _():
        for h in range(hb):
            o_ref[:, h * HEAD_DIM:(h + 1) * HEAD_DIM] = (acc_ref[h] / l_ref[h]).astype(o_ref.dtype)


def _mla_attn(q, k, v, B, T, hb=MLA_HEADS, tq=Q_TILE, tk=K_TILE):
    tk = min(tk, T)
    nq, nk = T // tq, T // tk
    tabs = _tile_schedule(nq, tq, tk, lambda qi: 0)
    grid_spec = pltpu.PrefetchScalarGridSpec(
        num_scalar_prefetch=3,
        grid=(B, MLA_HEADS // hb, int(tabs[0].shape[0])),
        in_specs=[pl.BlockSpec((tq, hb * MLA_SLOT), lambda b, h, s, qt, kt, ft: (b * nq + qt[s], h)),
                  pl.BlockSpec((tk, hb * MLA_SLOT), lambda b, h, s, qt, kt, ft: (b * nk + kt[s], h)),
                  pl.BlockSpec((tk, hb * HEAD_DIM), lambda b, h, s, qt, kt, ft: (b * nk + kt[s], h))],
        out_specs=pl.BlockSpec((tq, hb * HEAD_DIM), lambda b, h, s, qt, kt, ft: (b * nq + qt[s], h)),
        scratch_shapes=_flash_scratch(hb, tq, tk))
    return pl.pallas_call(
        functools.partial(_mla_attn_body, hb=hb, tq=tq, tk=tk),
        grid_spec=grid_spec,
        out_shape=jax.ShapeDtypeStruct((B * T, MLA_HEADS * HEAD_DIM), bf16),
        compiler_params=_params(("parallel", "parallel", "arbitrary")),
        name="mla_flash_attention",
    )(*tabs, q, k, v)


def _bias_table_body(rev_ref, o_ref):
    width = rev_ref.shape[-1]
    x = jnp.broadcast_to(rev_ref[0], (LANES, width))
    rolled = pltpu.roll(x, 0, 1, stride=1, stride_axis=0)
    for kb in range(width // LANES):
        o_ref[0, kb] = rolled[:, kb * LANES:(kb + 1) * LANES]


def _bias_table(rev):
    H, _, U = rev.shape
    return pl.pallas_call(
        _bias_table_body,
        grid=(H,),
        in_specs=[pl.BlockSpec((1, 1, U), lambda h: (h, 0, 0))],
        out_specs=pl.BlockSpec((1, U // LANES, LANES, LANES), lambda h: (h, 0, 0, 0)),
        out_shape=jax.ShapeDtypeStruct((H, U // LANES, LANES, LANES), f32),
        compiler_params=_params(("parallel",)),
        name="relative_bias_toeplitz_table",
    )(rev)


def _nsa_band_body(qi_tab, ki_tab, fl_tab, q_ref, k_ref, v_ref, g_ref, *rest, tq, tk, seq, select):
    if select:
        sel_ref, e_ref, o_ref, m_ref, l_ref, acc_ref, s_scr, p_scr, mx_scr, a_scr, mask_scr = rest
    else:
        oc_ref, os_ref, gate_ref, o_ref, m_ref, l_ref, acc_ref, s_scr, p_scr, mx_scr, a_scr, mask_scr = rest
    st = pl.program_id(2)
    qi, ki, fl = qi_tab[st], ki_tab[st], fl_tab[st]

    pl.when((fl & FLAG_FIRST) != 0)(lambda: _flash_init(m_ref, l_ref, acc_ref))

    d0 = qi * tq - ki * tk
    dist = d0 + lax.broadcasted_iota(i32, (tq, tk), 0) - lax.broadcasted_iota(i32, (tq, tk), 1)
    if select:
        picked = jnp.dot(sel_ref[...], e_ref[...], preferred_element_type=f32)
        valid = (dist >= 0) & (picked > 0.5)
    else:
        valid = (dist >= 0) & (dist < WINDOW)
    mask_scr[...] = jnp.where(valid, 0.0, NEG_INF)
    k = k_ref[0].astype(bf16)
    v = _with_ones(v_ref[0].astype(bf16))
    kb0 = (seq - d0) // LANES

    def score(h):
        return lax.dot_general(q_ref[:, h * HEAD_DIM:(h + 1) * HEAD_DIM], k, NT, preferred_element_type=f32)

    def add(h, rows, r0, kc):
        bias = g_ref[h, kb0 - r0 // LANES + kc, pl.ds(r0 % LANES, ROW_CHUNK), :]
        return bias, mask_scr[rows, kc * LANES:(kc + 1) * LANES]

    _flash_heads(NSA_HPG, tq, tk, score, add, lambda h: v, m_ref, l_ref, acc_ref, s_scr, p_scr, mx_scr, a_scr)

    @pl.when((fl & FLAG_LAST) != 0)
    def _():
        for h in range(NSA_HPG):
            sl = slice(h * HEAD_DIM, (h + 1) * HEAD_DIM)
            o = acc_ref[h] / l_ref[h]
            if not select:
                gate = gate_ref[...]
                o = (gate[:, h:h + 1] * oc_ref[:, sl].astype(f32)
                     + gate[:, NSA_HPG + h:NSA_HPG + h + 1] * os_ref[:, sl].astype(f32)
                     + gate[:, 2 * NSA_HPG + h:2 * NSA_HPG + h + 1] * o)
            o_ref[:, sl] = o.astype(o_ref.dtype)


def _nsa_band(q, kv, gtab, B, T, kcol, vcol, tk, sel=None, emat=None, merge=None, tq=Q_TILE):
    select = sel is not None
    nq, nk = T // tq, T // tk
    first = (lambda qi: 0) if select else (lambda qi: max(qi * tq - (WINDOW - 1), 0) // tk)
    tabs = _tile_schedule(nq, tq, tk, first)
    gw = NSA_HPG * HEAD_DIM
    in_specs = [
        pl.BlockSpec((tq, gw), lambda b, g, s, qt, kt, ft: (b * nq + qt[s], g)),
        pl.BlockSpec((1, tk, HEAD_DIM), lambda b, g, s, qt, kt, ft: (kcol * NSA_GROUPS + g, b * nk + kt[s], 0)),
        pl.BlockSpec((1, tk, HEAD_DIM), lambda b, g, s, qt, kt, ft: (vcol * NSA_GROUPS + g, b * nk + kt[s], 0)),
        pl.BlockSpec((NSA_HPG,) + gtab.shape[1:], lambda b, g, s, qt, kt, ft: (g, 0, 0, 0),
                     pipeline_mode=pl.Buffered(1)),
    ]
    args = [q, kv, kv, gtab]
    if select:
        in_specs += [pl.BlockSpec((tq, LANES), lambda b, g, s, qt, kt, ft: (b * nq + qt[s], g)),
                     pl.BlockSpec((LANES, tk), lambda b, g, s, qt, kt, ft: (0, kt[s]))]
        args += [sel, emat]
    else:
        o_other = pl.BlockSpec((tq, gw), lambda b, g, s, qt, kt, ft: (b * nq + qt[s], g))
        in_specs += [o_other, o_other, pl.BlockSpec((tq, LANES), lambda b, g, s, qt, kt, ft: (b * nq + qt[s], g))]
        args += list(merge)
    grid_spec = pltpu.PrefetchScalarGridSpec(
        num_scalar_prefetch=3,
        grid=(B, NSA_GROUPS, int(tabs[0].shape[0])),
        in_specs=in_specs,
        out_specs=pl.BlockSpec((tq, gw), lambda b, g, s, qt, kt, ft: (b * nq + qt[s], g)),
        scratch_shapes=_flash_scratch(NSA_HPG, tq, tk))
    return pl.pallas_call(
        functools.partial(_nsa_band_body, tq=tq, tk=tk, seq=T, select=select),
        grid_spec=grid_spec,
        out_shape=jax.ShapeDtypeStruct((B * T, NSA_HEADS * HEAD_DIM), bf16),
        compiler_params=_params(("parallel", "parallel", "arbitrary")),
        name="nsa_selected_attention" if select else "nsa_window_attention",
    )(*tabs, *args)


def _compress_body(x_ref, pe_ref, w1_ref, w2_ref, o_ref, ot_ref, *, nc):
    half = CMP_BLOCK // 2
    h_lo = jnp.zeros((nc, CMP_HIDDEN), f32)
    h_hi = jnp.zeros((nc, CMP_HIDDEN), f32)
    for p in range(half):
        rows = x_ref[0, pl.ds(p, nc, stride=CMP_STRIDE), :]
        h_lo += jnp.dot((rows + pe_ref[0, p:p + 1, :]).astype(bf16), w1_ref[0, p], preferred_element_type=f32)
        h_hi += jnp.dot((rows + pe_ref[0, half + p:half + p + 1, :]).astype(bf16), w1_ref[0, half + p],
                        preferred_element_type=f32)
    hid = h_lo + pltpu.roll(h_hi, nc - 1, 0)
    out = jnp.dot(jax.nn.gelu(hid).astype(bf16), w2_ref[0], preferred_element_type=f32)
    out = jnp.where(lax.broadcasted_iota(i32, out.shape, 0) < nc - 1, out, 0.0)
    o_ref[0, 0, 0] = out.astype(bf16)
    ot_ref[0, 0, 0] = out.T.astype(bf16)


def _compress(kv, pe, w1, w2, B, T):
    nc = T // CMP_STRIDE
    return pl.pallas_call(
        functools.partial(_compress_body, nc=nc),
        grid=(2, B, NSA_GROUPS),
        in_specs=[pl.BlockSpec((1, T, HEAD_DIM), lambda c, b, g: (c * NSA_GROUPS + g, b, 0)),
                  pl.BlockSpec((1, CMP_BLOCK, HEAD_DIM), lambda c, b, g: (c, 0, 0)),
                  pl.BlockSpec((1, CMP_BLOCK, HEAD_DIM, CMP_HIDDEN), lambda c, b, g: (c, 0, 0, 0)),
                  pl.BlockSpec((1, CMP_HIDDEN, HEAD_DIM), lambda c, b, g: (c, 0, 0))],
        out_specs=[pl.BlockSpec((1, 1, 1, nc, HEAD_DIM), lambda c, b, g: (c, b, g, 0, 0)),
                   pl.BlockSpec((1, 1, 1, HEAD_DIM, nc), lambda c, b, g: (c, b, g, 0, 0))],
        out_shape=[jax.ShapeDtypeStruct((2, B, NSA_GROUPS, nc, HEAD_DIM), bf16),
                   jax.ShapeDtypeStruct((2, B, NSA_GROUPS, HEAD_DIM, nc), bf16)],
        compiler_params=_params(("parallel", "parallel", "parallel")),
        name="nsa_compress_mlp",
    )(kv, pe, w1, w2)


def _cmp_body(q_ref, kc_ref, vct_ref, cw_ref, ovt_ref, o_ref, sel_ref, *, nc, nq):
    tq = CMP_TQ
    qi = pl.program_id(2)
    t0 = qi * tq
    key_row = lax.broadcasted_iota(i32, (nc, tq), 0)
    q_col = lax.broadcasted_iota(i32, (nc, tq), 1)
    valid = (t0 + q_col - CMP_STRIDE * key_row - (CMP_BLOCK - 1)) >= 0
    kc = kc_ref[0, 0, 0]
    vct = vct_ref[0, 0, 0]
    psum = jnp.zeros((nc, tq), f32)
    for h in range(NSA_HPG):
        bias = jnp.concatenate([cw_ref[h, qi - c + nq] for c in range(nc // 8)], axis=0)
        s = lax.dot_general(kc, q_ref[:, h * HEAD_DIM:(h + 1) * HEAD_DIM], NT, preferred_element_type=f32)
        s = jnp.where(valid, s + bias, NEG_INF)
        m = jnp.max(s, axis=0, keepdims=True)
        e = jnp.where(valid, jnp.exp2(s - m), 0.0)
        l = jnp.sum(e, axis=0, keepdims=True)
        p = e / jnp.where(l > 0.0, l, 1.0)
        psum += p
        o_t = jnp.dot(vct, p.astype(bf16), preferred_element_type=f32)
        o_ref[:, h * HEAD_DIM:(h + 1) * HEAD_DIM] = o_t.T.astype(o_ref.dtype)

    imp = jnp.dot(ovt_ref[...], psum, preferred_element_type=f32, precision=lax.Precision.HIGHEST)
    nrow = imp.shape[0]
    n_slc = (nc * CMP_STRIDE) // SLC_BLOCK
    j = lax.broadcasted_iota(i32, (nrow, tq), 0)
    t = t0 + lax.broadcasted_iota(i32, (nrow, tq), 1)
    cur = jnp.right_shift(t, 6)
    ok = j * SLC_BLOCK <= t
    forced = (j == 0) | (j == cur) | (j == cur - 1)
    score = jnp.where(forced, FORCED_SCORE, jnp.where(ok, imp, NEG_INF))
    score = jnp.where(j < n_slc, score, PAD_SCORE)
    chosen = jnp.zeros((nrow, tq), f32)
    for _ in range(min(SLC_TOPK, n_slc)):
        top = jnp.max(score, axis=0, keepdims=True)
        first = jnp.min(jnp.where(score == top, j, nrow), axis=0, keepdims=True)
        hit = j == first
        chosen = jnp.where(hit, 1.0, chosen)
        score = jnp.where(hit, PAD_SCORE, score)
    chosen = jnp.where(ok & (j < n_slc), chosen, 0.0)
    sel_ref[...] = chosen.T.astype(sel_ref.dtype)


def _nsa_cmp(q, kc, vct, cw, ovt, B, T):
    nc = T // CMP_STRIDE
    nq = T // CMP_TQ
    gw = NSA_HPG * HEAD_DIM
    return pl.pallas_call(
        functools.partial(_cmp_body, nc=nc, nq=nq),
        grid=(B, NSA_GROUPS, nq),
        in_specs=[pl.BlockSpec((CMP_TQ, gw), lambda b, g, qi: (b * nq + qi, g)),
                  pl.BlockSpec((1, 1, 1, nc, HEAD_DIM), lambda b, g, qi: (0, b, g, 0, 0)),
                  pl.BlockSpec((1, 1, 1, HEAD_DIM, nc), lambda b, g, qi: (1, b, g, 0, 0)),
                  pl.BlockSpec((NSA_HPG, 2 * nq, 8, CMP_TQ), lambda b, g, qi: (g, 0, 0, 0)),
                  pl.BlockSpec(ovt.shape, lambda b, g, qi: (0, 0))],
        out_specs=[pl.BlockSpec((CMP_TQ, gw), lambda b, g, qi: (b * nq + qi, g)),
                   pl.BlockSpec((CMP_TQ, LANES), lambda b, g, qi: (b * nq + qi, g))],
        out_shape=[jax.ShapeDtypeStruct((B * T, NSA_HEADS * HEAD_DIM), bf16),
                   jax.ShapeDtypeStruct((B * T, NSA_GROUPS * LANES), bf16)],
        compiler_params=_params(("parallel", "parallel", "parallel")),
        name="nsa_compressed_attention_select",
    )(q, kc, vct, cw, ovt)


def _layer_norm(y, g, b):
    mu = jnp.mean(y, axis=-1, keepdims=True)
    var = jnp.mean(jnp.square(y - mu), axis=-1, keepdims=True)
    return (y - mu) * lax.rsqrt(var + LN_EPS) * g + b


def _branch_merge_body(a_ref, b_ref, wa_ref, wb_ref, ga_ref, gb_ref, o_ref):
    ya = jnp.dot(a_ref[...], wa_ref[...], preferred_element_type=f32)
    yb = jnp.dot(b_ref[...], wb_ref[...], preferred_element_type=f32)
    o_ref[...] = (ga_ref[...] * ya + gb_ref[...] * yb).astype(o_ref.dtype)


def _branch_merge(a, b, wa, wb, mg, tm, tn):
    M, K = a.shape
    Nc = wa.shape[1]
    nj = Nc // tn
    return pl.pallas_call(
        _branch_merge_body,
        grid=(M // tm, nj),
        in_specs=[pl.BlockSpec((tm, K), lambda i, j: (i, 0)), pl.BlockSpec((tm, K), lambda i, j: (i, 0)),
                  pl.BlockSpec((K, tn), lambda i, j: (0, j)), pl.BlockSpec((K, tn), lambda i, j: (0, j)),
                  pl.BlockSpec((tm, tn), lambda i, j: (i, j)), pl.BlockSpec((tm, tn), lambda i, j: (i, nj + j))],
        out_specs=pl.BlockSpec((tm, tn), lambda i, j: (i, j)),
        out_shape=jax.ShapeDtypeStruct((M, Nc), bf16),
        compiler_params=_params(("parallel", "parallel")),
        name="branch_proj_merge",
    )(a, b, wa, wb, mg, mg)


def _proj_ln_body(z_ref, w_ref, x_ref, g_ref, b_ref, o_ref):
    y = jnp.dot(z_ref[...], w_ref[...], preferred_element_type=f32)
    o_ref[...] = _layer_norm(ALPHA * x_ref[...] + y, g_ref[...], b_ref[...])


def _proj_ln(z, w, x, g, b, tm):
    M, K = z.shape
    D = w.shape[1]
    vec = pl.BlockSpec((1, D), lambda i: (0, 0))
    return pl.pallas_call(
        _proj_ln_body,
        grid=(M // tm,),
        in_specs=[pl.BlockSpec((tm, K), lambda i: (i, 0)), pl.BlockSpec((K, D), lambda i: (0, 0)),
                  pl.BlockSpec((tm, D), lambda i: (i, 0)), vec, vec],
        out_specs=pl.BlockSpec((tm, D), lambda i: (i, 0)),
        out_shape=jax.ShapeDtypeStruct((M, D), f32),
        compiler_params=_params(("parallel",)),
        name="mix_out_proj_layernorm",
    )(z, w, x, g, b)


def _mem_attn_body(x_ref, wq_ref, k_ref, v_ref, wo_ref, g_ref, b_ref, o_ref, ob_ref):
    x = x_ref[...]
    q = (jnp.dot(x.astype(bf16), wq_ref[...], preferred_element_type=f32) * HEAD_DIM ** -0.5).astype(bf16)
    heads = []
    for h in range(MEM_HEADS):
        sl = slice(h * HEAD_DIM, (h + 1) * HEAD_DIM)
        s = lax.dot_general(q[:, sl], k_ref[0, :, sl], NT, preferred_element_type=f32)
        e = jnp.exp(s - jnp.max(s, axis=-1, keepdims=True))
        p = e / jnp.sum(e, axis=-1, keepdims=True)
        heads.append(jnp.dot(p.astype(bf16), v_ref[0, :, sl], preferred_element_type=f32).astype(bf16))
    y = jnp.dot(jnp.concatenate(heads, axis=1), wo_ref[...], preferred_element_type=f32)
    out = _layer_norm(ALPHA * x + y, g_ref[...], b_ref[...])
    o_ref[...] = out
    ob_ref[...] = out.astype(bf16)


def _mem_attn(x, wq, km, vm, wo, g, b, T, tm):
    M, D = x.shape
    nt = T // tm
    W = wq.shape[1]
    vec = pl.BlockSpec((1, D), lambda i: (0, 0))
    blk = pl.BlockSpec((tm, D), lambda i: (i, 0))
    return pl.pallas_call(
        _mem_attn_body,
        grid=(M // tm,),
        in_specs=[blk, pl.BlockSpec((D, W), lambda i: (0, 0)),
                  pl.BlockSpec((1, N_MEM, W), lambda i: (i // nt, 0, 0)),
                  pl.BlockSpec((1, N_MEM, W), lambda i: (i // nt, 0, 0)),
                  pl.BlockSpec((W, D), lambda i: (0, 0)), vec, vec],
        out_specs=[blk, blk],
        out_shape=[jax.ShapeDtypeStruct((M, D), f32), jax.ShapeDtypeStruct((M, D), bf16)],
        compiler_params=_params(("parallel",)),
        name="memory_cross_attention_layernorm",
    )(x, wq, km, vm, wo, g, b)


def _add_ln_body(x_ref, y_ref, g_ref, b_ref, o_ref):
    o_ref[...] = _layer_norm(ALPHA * x_ref[...] + y_ref[...], g_ref[...], b_ref[...])


def _add_ln(x, y, g, b, tm):
    M, D = x.shape
    vec = pl.BlockSpec((1, D), lambda i: (0, 0))
    blk = pl.BlockSpec((tm, D), lambda i: (i, 0))
    return pl.pallas_call(
        _add_ln_body,
        grid=(M // tm,),
        in_specs=[blk, blk, vec, vec],
        out_specs=blk,
        out_shape=jax.ShapeDtypeStruct((M, D), f32),
        compiler_params=_params(("parallel",)),
        name="residual_layernorm",
    )(x, y, g, b)


def _topk_rows(s, ids, k):
    n = s.shape[1]
    slot = lax.broadcasted_iota(i32, (k, n), 0)
    vals = jnp.zeros((k, n), f32)
    inds = jnp.zeros((k, n), f32)
    for r in range(k):
        top = jnp.max(s, axis=0, keepdims=True)
        first = jnp.min(jnp.where(s == top, ids, PAD_ID), axis=0, keepdims=True)
        vals = jnp.where(slot == r, top, vals)
        inds = jnp.where(slot == r, first, inds)
        s = jnp.where(ids == first, PAD_SCORE, s)
    return vals, inds


def _pick_rows(table, sel):
    out = jnp.zeros(sel.shape, table.dtype)
    for a in range(table.shape[0]):
        out = jnp.where(sel == a, table[a:a + 1, :], out)
    return out


CAND_COUNTS = tuple(PEER_TOPK // (a + 1) for a in range(PEER_TOPK))
CAND_ROWS = -(-sum(CAND_COUNTS) // 8) * 8


def _cand_ids(tn):
    ids = [a * PEER_TOPK + b for a, nb in enumerate(CAND_COUNTS) for b in range(nb)]
    ids = np.array(ids + [PAD_ID] * (CAND_ROWS - len(ids)), np.float32)
    return jnp.asarray(np.broadcast_to(ids[:, None], (CAND_ROWS, tn)))


def _peer_route_body(q_ref, keys_ref, ids_ref, e1_ref, e2_ref, gt_ref, e1_scr, e2_scr, g_scr, cand_scr):
    k1 = keys_ref[0].astype(bf16)
    k2 = keys_ref[1].astype(bf16)
    tn = q_ref.shape[0]
    key_ids = lax.broadcasted_iota(i32, (PEER_N_KEYS, tn), 0).astype(f32)
    cand_scr[...] = jnp.full(cand_scr.shape, PAD_SCORE, f32)
    for h in range(PEER_HEADS):
        q1 = q_ref[:, (2 * h) * LANES:(2 * h + 1) * LANES]
        q2 = q_ref[:, (2 * h + 1) * LANES:(2 * h + 2) * LANES]
        s1 = lax.dot_general(k1, q1, NT, preferred_element_type=f32)
        s2 = lax.dot_general(k2, q2, NT, preferred_element_type=f32)
        v1, i1 = _topk_rows(s1, key_ids, PEER_TOPK)
        v2, i2 = _topk_rows(s2, key_ids, PEER_TOPK)
        off = 0
        for a, nb in enumerate(CAND_COUNTS):
            cand_scr[off:off + nb, :] = v1[a:a + 1, :] + v2[0:nb, :]
            off += nb
        top_s, flat = _topk_rows(cand_scr[...], ids_ref[...], PEER_TOPK)
        flat = flat.astype(i32)
        e = jnp.exp(top_s - top_s[0:1, :])
        rows = slice(h * PEER_TOPK, (h + 1) * PEER_TOPK)
        g_scr[rows, :] = e / jnp.sum(e, axis=0, keepdims=True)
        e1_scr[rows, :] = _pick_rows(i1, jnp.right_shift(flat, 4)).astype(i32)
        e2_scr[rows, :] = _pick_rows(i2, jnp.bitwise_and(flat, PEER_TOPK - 1)).astype(i32)
    e1_ref[...] = e1_scr[...].T
    e2_ref[...] = e2_scr[...].T
    gt_ref[...] = g_scr[...].T


def _peer_route(q, keys, tn=LANES):
    M, W = q.shape
    out = pl.BlockSpec((tn, PEER_HK), lambda i: (i, 0))
    return pl.pallas_call(
        _peer_route_body,
        grid=(M // tn,),
        in_specs=[pl.BlockSpec((tn, W), lambda i: (i, 0)), pl.BlockSpec(keys.shape, lambda i: (0, 0, 0)),
                  pl.BlockSpec((CAND_ROWS, tn), lambda i: (0, 0))],
        out_specs=[out, out, out],
        out_shape=[jax.ShapeDtypeStruct((M, PEER_HK), i32), jax.ShapeDtypeStruct((M, PEER_HK), i32),
                   jax.ShapeDtypeStruct((M, PEER_HK), f32)],
        scratch_shapes=[pltpu.VMEM((PEER_HK, tn), i32), pltpu.VMEM((PEER_HK, tn), i32), pltpu.VMEM((PEER_HK, tn), f32),
                        pltpu.VMEM((CAND_ROWS, tn), f32)],
        compiler_params=_params(("parallel",)),
        name="peer_product_key_routing",
    )(q, keys, _cand_ids(tn))


def _peer_weights_body(e1_ref, e2_ref, g_ref, w_ref, *, tn):
    sub = lax.broadcasted_iota(i32, (PEER_N_KEYS, PEER_HK), 0)

    def one(n, carry):
        r1 = e1_ref[pl.ds(n, 1), :]
        r2 = e2_ref[pl.ds(n, 1), :]
        g = g_ref[pl.ds(n, 1), :]
        a_t = (sub == r1).astype(bf16)
        b_t = jnp.where(sub == r2, g, 0.0).astype(bf16)
        w_ref[n] = lax.dot_general(a_t, b_t, NT, preferred_element_type=f32).astype(w_ref.dtype)
        return carry

    lax.fori_loop(0, tn, one, 0, unroll=8)


def _peer_weights(e1, e2, g, tn=64):
    M = e1.shape[0]
    blk = pl.BlockSpec((tn, PEER_HK), lambda i: (i, 0))
    return pl.pallas_call(
        functools.partial(_peer_weights_body, tn=tn),
        grid=(M // tn,),
        in_specs=[blk, blk, blk],
        out_specs=pl.BlockSpec((tn, PEER_N_KEYS, PEER_N_KEYS), lambda i: (i, 0, 0)),
        out_shape=jax.ShapeDtypeStruct((M, PEER_N_KEYS, PEER_N_KEYS), bf16),
        compiler_params=_params(("parallel",)),
        name="peer_routing_weights",
    )(e1, e2, g)


GELU_SLOPE = 2.0 * math.sqrt(2.0 / math.pi)


def _gelu_tanh(x):
    minus_2z = x * ((-GELU_SLOPE * 0.044715 * LOG2E) * (x * x) - GELU_SLOPE * LOG2E)
    return x / (1.0 + jnp.exp2(minus_2z))


def _peer_dense_body(x_ref, w_ref, u_ref, v_ref, o_ref, *, rows):
    @pl.when(pl.program_id(1) == 0)
    def _():
        o_ref[...] = jnp.zeros(o_ref.shape, f32)

    act = _gelu_tanh(lax.dot_general(x_ref[...], u_ref[...].astype(bf16), NT, preferred_element_type=f32))
    w = jnp.concatenate([w_ref[r] for r in range(rows)], axis=1)
    o_ref[...] += jnp.dot(w * act.astype(bf16), v_ref[...].astype(bf16), preferred_element_type=f32)


def _peer_dense(xb, w, u, v, tm=1024, rows=4):
    M, D = xb.shape
    te = rows * PEER_N_KEYS
    return pl.pallas_call(
        functools.partial(_peer_dense_body, rows=rows),
        grid=(M // tm, PEER_N_KEYS // rows),
        in_specs=[pl.BlockSpec((tm, D), lambda i, e: (i, 0)),
                  pl.BlockSpec((rows, tm, PEER_N_KEYS), lambda i, e: (e, i, 0)),
                  pl.BlockSpec((te, D), lambda i, e: (e, 0)), pl.BlockSpec((te, D), lambda i, e: (e, 0))],
        out_specs=pl.BlockSpec((tm, D), lambda i, e: (i, 0)),
        out_shape=jax.ShapeDtypeStruct((M, D), f32),
        compiler_params=_params(("parallel", "arbitrary")),
        name="peer_dense_experts",
    )(xb, w, u, v)


def _t5_bucket(dist):
    dist = jnp.maximum(dist, 0)
    n_log = REL_BUCKETS - REL_MAX_EXACT
    large = REL_MAX_EXACT + (jnp.log(jnp.maximum(dist, 1).astype(f32) / REL_MAX_EXACT)
                             / math.log(REL_MAX_DIST / REL_MAX_EXACT) * n_log).astype(i32)
    large = jnp.minimum(large, REL_BUCKETS - 1)
    return jnp.where(dist < REL_MAX_EXACT, dist, large)


def _position_tables(rel_bias, T):
    by_dist = (rel_bias[_t5_bucket(jnp.arange(T))] * LOG2E).T.astype(f32)
    rev = jnp.pad(by_dist[:, ::-1], ((0, 0), (1, K_TILE - 1)), mode="edge")[:, None, :]
    shift = (CMP_BLOCK - 1) + CMP_STRIDE * 7
    ext = jnp.pad(by_dist, ((0, 0), (T + shift, 0)), mode="edge")
    rows = [ext[:, CMP_STRIDE * (7 - r):CMP_STRIDE * (7 - r) + 2 * T] for r in range(8)]
    cmpw = jnp.stack(rows, axis=1).reshape(NSA_HEADS, 8, 2 * T // CMP_TQ, CMP_TQ)
    return rev, jnp.transpose(cmpw, (0, 2, 1, 3))


def _rope_slot_tables(T):
    inv = ROPE_THETA ** (-jnp.arange(0, MLA_ROPE, 2, dtype=f32) / MLA_ROPE)
    ang = jnp.arange(T, dtype=f32)[:, None] * inv[None, :]
    cos, sin = jnp.cos(ang), jnp.sin(ang)
    z = jnp.zeros_like(cos)
    ta = jnp.concatenate([cos, cos, z, z], axis=1)
    tb = jnp.concatenate([-sin, z, z, z], axis=1)
    tc = jnp.concatenate([z, sin, z, z], axis=1)
    return ta, tb, tc


def _layer(x, mem, rel_bias, w_in, cmp_k_pe, cmp_k_w1, cmp_k_w2, cmp_v_pe, cmp_v_w1, cmp_v_w2,
           mla_kv_norm, mla_w_uk, mla_w_uv, w_branch_nsa, w_branch_mla, w_mix_out, ln_mix_g, ln_mix_b,
           mem_w_q, mem_w_k, mem_w_v, mem_w_o, ln_mem_g, ln_mem_b,
           peer_w_query, peer_sub_keys, peer_u, peer_v, ln_ffn_g, ln_ffn_b, ropes):
    B, T, D = x.shape
    N = B * T
    x2d = x.reshape(N, D)
    xb = x2d.astype(bf16)
    row = lambda a: a.reshape(1, -1)

    gate_cols = np.array([[br * NSA_HEADS + g * NSA_HPG + h for br in range(3) for h in range(NSA_HPG)]
                          for g in range(NSA_GROUPS)])
    w_gate = jnp.pad(w_in[:, OFF_NSA_GATE:OFF_MLA_Q][:, gate_cols],
                     ((0, 0), (0, 0), (0, LANES - 3 * NSA_HPG))).reshape(D, NSA_GROUPS * LANES)
    w_mla_q = jnp.pad(w_in[:, OFF_MLA_Q:OFF_MLA_KV].reshape(D, MLA_HEADS, MLA_NOPE + MLA_ROPE),
                      ((0, 0), (0, 0), (0, MLA_SLOT - MLA_NOPE - MLA_ROPE))).reshape(D, MLA_HEADS * MLA_SLOT)
    w_mla_kv = jnp.pad(w_in[:, OFF_MLA_KV:OFF_MERGE], ((0, 0), (0, LANES - MLA_ROPE)))
    segs = [w_mla_kv, jnp.zeros((D, LANES), f32), w_gate,
            w_in[:, :OFF_NSA_KV], w_mla_q, w_in[:, OFF_MERGE:], w_in[:, OFF_NSA_KV:OFF_NSA_GATE]]
    starts = np.concatenate([[0], np.cumsum([s.shape[1] for s in segs])])
    col = lambda i: (int(starts[i]), int(segs[i].shape[1]))
    w_all = jnp.concatenate(segs, axis=1).astype(bf16)

    mla_ckv = _mm(xb, w_all, f32, 1024, MLA_KV_RANK + LANES, name="mla_kv_proj", cols=col(0))
    gates = _mm(xb, w_all, f32, 1024, NSA_GROUPS * LANES, act="sigmoid", name="nsa_gate_proj", cols=col(2))
    nsa_q = _mm(xb, w_all, bf16, 1024, 1024, scale=HEAD_DIM ** -0.5 * LOG2E, name="nsa_q_proj", cols=col(3))
    mla_q = _mla_q(xb, w_all, col(4), ropes, T, 1024, 1024)
    merge = _mm(xb, w_all, f32, 1024, 1024, act="sigmoid", name="merge_gate_proj", cols=col(5))
    nsa_kv = _mm_slabs(xb, w_all, col(6), 1024, 512, "nsa_kv_proj")

    slc_tk = min(K_TILE, T)
    rev_tab, cmp_tab = _position_tables(rel_bias, T)
    bias_tab = _bias_table(rev_tab)
    pe = jnp.stack([cmp_k_pe, cmp_v_pe])
    w1 = jnp.stack([cmp_k_w1, cmp_v_w1]).reshape(2, CMP_BLOCK, HEAD_DIM, CMP_HIDDEN).astype(bf16)
    w2 = jnp.stack([cmp_k_w2, cmp_v_w2]).astype(bf16)
    kvc, kvc_t = _compress(nsa_kv, pe, w1, w2, B, T)
    nc = T // CMP_STRIDE
    n_slc = T // SLC_BLOCK
    cs = np.arange(nc)[None, :] * CMP_STRIDE
    ss = np.arange(LANES)[:, None] * SLC_BLOCK
    ovt = ((cs < ss + SLC_BLOCK) & (cs + CMP_BLOCK > ss) & (np.arange(nc)[None, :] < nc - 1)
           & (np.arange(LANES)[:, None] < n_slc)).astype(np.float32)
    o_cmp, sel = _nsa_cmp(nsa_q, kvc, kvc_t, cmp_tab, jnp.asarray(ovt), B, T)
    emat = (np.arange(T)[None, :] // SLC_BLOCK == np.arange(LANES)[:, None]).astype(np.float32)
    o_slc = _nsa_band(nsa_q, nsa_kv, bias_tab, B, T, 2, 3, slc_tk, sel=sel, emat=jnp.asarray(emat, dtype=bf16))
    o_nsa = _nsa_band(nsa_q, nsa_kv, bias_tab, B, T, 4, 5, WIN_TILE, merge=(o_cmp, o_slc, gates), tq=WIN_TILE)

    mla_k, mla_v = _mla_kv(mla_ckv, row(mla_kv_norm),
                           jnp.asarray(mla_w_uk, bf16), jnp.asarray(mla_w_uv, bf16), ropes, T, 512)
    o_mla = _mla_attn(mla_q, mla_k, mla_v, B, T)

    z = _branch_merge(o_nsa, o_mla, w_branch_nsa.astype(bf16), w_branch_mla.astype(bf16), merge, 1024, 512)
    x1 = _proj_ln(z, w_mix_out.astype(bf16), x2d, row(ln_mix_g), row(ln_mix_b), 512)

    mem2d = mem.reshape(B * N_MEM, D).astype(bf16)
    km = _mm(mem2d, mem_w_k.astype(bf16), bf16, B * N_MEM, 256, name="mem_k_proj").reshape(B, N_MEM, -1)
    vm = _mm(mem2d, mem_w_v.astype(bf16), bf16, B * N_MEM, 256, name="mem_v_proj").reshape(B, N_MEM, -1)
    x2, x2b = _mem_attn(x1, mem_w_q.astype(bf16), km, vm, mem_w_o.astype(bf16), row(ln_mem_g), row(ln_mem_b), T, 512)

    pq = _mm(x2b, peer_w_query.astype(bf16), bf16, 1024, 512, name="peer_query_proj")
    e1, e2, gt = _peer_route(pq, peer_sub_keys)
    wts = jnp.transpose(_peer_weights(e1, e2, gt), (1, 0, 2))
    y = _peer_dense(x2b, wts, peer_u, peer_v)
    x3 = _add_ln(x2, y, row(ln_ffn_g), row(ln_ffn_b), 512)
    return x3.reshape(B, T, D)


def kernel(x, mem, rel_bias, w_in, cmp_k_pe, cmp_k_w1, cmp_k_w2, cmp_v_pe, cmp_v_w1, cmp_v_w2, mla_kv_norm, mla_w_uk, mla_w_uv, w_branch_nsa, w_branch_mla, w_mix_out, ln_mix_g, ln_mix_b, mem_w_q, mem_w_k, mem_w_v, mem_w_o, ln_mem_g, ln_mem_b, peer_w_query, peer_sub_keys, peer_u, peer_v, ln_ffn_g, ln_ffn_b):
    ropes = _rope_slot_tables(x.shape[1])
    for l in range(DEPTH):
        x = _layer(x, mem, rel_bias, w_in[l], cmp_k_pe[l], cmp_k_w1[l], cmp_k_w2[l], cmp_v_pe[l], cmp_v_w1[l],
                   cmp_v_w2[l], mla_kv_norm[l], mla_w_uk[l], mla_w_uv[l], w_branch_nsa[l], w_branch_mla[l],
                   w_mix_out[l], ln_mix_g[l], ln_mix_b[l], mem_w_q[l], mem_w_k[l], mem_w_v[l], mem_w_o[l],
                   ln_mem_g[l], ln_mem_b[l], peer_w_query[l], peer_sub_keys[l], peer_u[l], peer_v[l],
                   ln_ffn_g[l], ln_ffn_b[l], ropes)
    return x
```

```python
import functools
import math

import numpy as np
import jax
import jax.numpy as jnp
from jax import lax
from jax.experimental import pallas as pl
from jax.experimental.pallas import tpu as pltpu

f32 = jnp.float32
bf16 = jnp.bfloat16
i32 = jnp.int32

D_MODEL = 2048
N_MEM = 256
NSA_HEADS = 16
NSA_GROUPS = 2
NSA_HPG = NSA_HEADS // NSA_GROUPS
HEAD_DIM = 128
CMP_BLOCK = 32
CMP_STRIDE = 16
CMP_HIDDEN = 256
SLC_BLOCK = 64
SLC_TOPK = 16
WINDOW = 512
FORCED_SCORE = 1e4
MLA_HEADS = 16
MLA_NOPE = 128
MLA_ROPE = 64
MLA_KV_RANK = 512
MLA_SLOT = 256
ROPE_THETA = 10000.0
REL_BUCKETS = 32
REL_MAX_EXACT = 16
REL_MAX_DIST = 4096
MEM_HEADS = 4
PEER_HEADS = 8
PEER_N_KEYS = 128
PEER_TOPK = 16
PEER_HK = PEER_HEADS * PEER_TOPK
DEPTH = 1
ALPHA = (2.0 * DEPTH) ** 0.25
NEG_INF = -1e30
M_INIT = -1e29
PAD_SCORE = -3e38
PAD_ID = 1e9
LOG2E = math.log2(math.e)
LN_EPS = 1e-5
RMS_EPS = 1e-6

NSA_Q_COLS = NSA_HEADS * HEAD_DIM
NSA_KV_COLS = NSA_GROUPS * HEAD_DIM
OFF_NSA_KV = NSA_Q_COLS
OFF_NSA_GATE = OFF_NSA_KV + 6 * NSA_KV_COLS
OFF_MLA_Q = OFF_NSA_GATE + 3 * NSA_HEADS
OFF_MLA_KV = OFF_MLA_Q + MLA_HEADS * (MLA_NOPE + MLA_ROPE)
OFF_MERGE = OFF_MLA_KV + MLA_KV_RANK + MLA_ROPE

LANES = 128
Q_TILE = 512
K_TILE = 512
WIN_TILE = 256
ROW_CHUNK = 32
CMP_TQ = 128
VMEM_LIMIT = 56 * 1024 * 1024

NT = (((1,), (1,)), ((), ()))


def _params(sem):
    return pltpu.CompilerParams(dimension_semantics=sem, vmem_limit_bytes=VMEM_LIMIT)


def _mm_body(a_ref, b_ref, o_ref, *, scale, act):
    acc = jnp.dot(a_ref[...].astype(bf16), b_ref[...], preferred_element_type=f32)
    if scale != 1.0:
        acc = acc * scale
    if act == "sigmoid":
        acc = jax.nn.sigmoid(acc)
    o_ref[...] = acc.astype(o_ref.dtype)


def _mm(a, b, out_dtype, tm, tn, scale=1.0, act=None, name="mm", cols=None):
    M, K = a.shape
    c0, Nc = (0, b.shape[1]) if cols is None else cols
    assert c0 % tn == 0 and Nc % tn == 0
    j0 = c0 // tn
    return pl.pallas_call(
        functools.partial(_mm_body, scale=scale, act=act),
        grid=(M // tm, Nc // tn),
        in_specs=[pl.BlockSpec((tm, K), lambda i, j: (i, 0)), pl.BlockSpec((K, tn), lambda i, j: (0, j0 + j))],
        out_specs=pl.BlockSpec((tm, tn), lambda i, j: (i, j)),
        out_shape=jax.ShapeDtypeStruct((M, Nc), out_dtype),
        compiler_params=_params(("parallel", "parallel")),
        name=name,
    )(a, b)


def _mm_slabs_body(a_ref, b_ref, o_ref):
    acc = jnp.dot(a_ref[...], b_ref[...], preferred_element_type=f32)
    for c in range(o_ref.shape[0]):
        o_ref[c] = acc[:, c * LANES:(c + 1) * LANES]


def _mm_slabs(a, b, cols, tm, tn, name):
    M, K = a.shape
    c0, Nc = cols
    assert c0 % tn == 0 and Nc % tn == 0
    j0 = c0 // tn
    return pl.pallas_call(
        _mm_slabs_body,
        grid=(M // tm, Nc // tn),
        in_specs=[pl.BlockSpec((tm, K), lambda i, j: (i, 0)), pl.BlockSpec((K, tn), lambda i, j: (0, j0 + j))],
        out_specs=pl.BlockSpec((tn // LANES, tm, LANES), lambda i, j: (j, i, 0)),
        out_shape=jax.ShapeDtypeStruct((Nc // LANES, M, LANES), f32),
        compiler_params=_params(("parallel", "parallel")),
        name=name,
    )(a, b)


def _rope_slot(y, ta, tb, tc):
    return y * ta + pltpu.roll(y, LANES - 32, 1) * tb + pltpu.roll(y, 32, 1) * tc


def _mlaq_body(a_ref, b_ref, ta_ref, tb_ref, tc_ref, o_ref, *, scale, nslot):
    acc = jnp.dot(a_ref[...], b_ref[...], preferred_element_type=f32)
    ta, tb, tc = ta_ref[...], tb_ref[...], tc_ref[...]
    for s in range(nslot):
        lo = s * MLA_SLOT
        o_ref[:, lo:lo + MLA_NOPE] = (acc[:, lo:lo + MLA_NOPE] * scale).astype(o_ref.dtype)
        rot = _rope_slot(acc[:, lo + MLA_NOPE:lo + MLA_SLOT], ta, tb, tc)
        o_ref[:, lo + MLA_NOPE:lo + MLA_SLOT] = (rot * scale).astype(o_ref.dtype)


def _mla_q(xb, w, cols, ropes, T, tm, tn):
    M, K = xb.shape
    c0, Nc = cols
    assert c0 % tn == 0 and Nc % tn == 0
    j0 = c0 // tn
    nt = T // tm
    tab = pl.BlockSpec((tm, LANES), lambda i, j: (i % nt, 0))
    return pl.pallas_call(
        functools.partial(_mlaq_body, scale=(MLA_NOPE + MLA_ROPE) ** -0.5 * LOG2E, nslot=tn // MLA_SLOT),
        grid=(M // tm, Nc // tn),
        in_specs=[pl.BlockSpec((tm, K), lambda i, j: (i, 0)), pl.BlockSpec((K, tn), lambda i, j: (0, j0 + j)),
                  tab, tab, tab],
        out_specs=pl.BlockSpec((tm, tn), lambda i, j: (i, j)),
        out_shape=jax.ShapeDtypeStruct((M, Nc), bf16),
        compiler_params=_params(("parallel", "parallel")),
        name="mla_q_proj_rope",
    )(xb, w, *ropes)


def _mlakv_body(c_ref, g_ref, wuk_ref, wuv_ref, ta_ref, tb_ref, tc_ref, k_ref, v_ref):
    x = c_ref[...]
    c = x[:, :MLA_KV_RANK]
    ms = jnp.mean(c * c, axis=-1, keepdims=True)
    cn = (c * lax.rsqrt(ms + RMS_EPS) * g_ref[...]).astype(bf16)
    kpe = _rope_slot(x[:, MLA_KV_RANK:], ta_ref[...], tb_ref[...], tc_ref[...]).astype(bf16)
    kn = jnp.dot(cn, wuk_ref[...], preferred_element_type=f32).astype(bf16)
    v_ref[...] = jnp.dot(cn, wuv_ref[...], preferred_element_type=f32).astype(bf16)
    for h in range(MLA_HEADS):
        k_ref[:, h * MLA_SLOT:h * MLA_SLOT + MLA_NOPE] = kn[:, h * MLA_NOPE:(h + 1) * MLA_NOPE]
        k_ref[:, h * MLA_SLOT + MLA_NOPE:(h + 1) * MLA_SLOT] = kpe


def _mla_kv(ckv, g, wuk, wuv, ropes, T, tm):
    M, K = ckv.shape
    nt = T // tm
    tab = pl.BlockSpec((tm, LANES), lambda i: (i % nt, 0))
    full = lambda shape: pl.BlockSpec(shape, lambda i: (0,) * len(shape))
    return pl.pallas_call(
        _mlakv_body,
        grid=(M // tm,),
        in_specs=[pl.BlockSpec((tm, K), lambda i: (i, 0)), full(g.shape), full(wuk.shape), full(wuv.shape), tab, tab, tab],
        out_specs=[pl.BlockSpec((tm, MLA_HEADS * MLA_SLOT), lambda i: (i, 0)),
                   pl.BlockSpec((tm, MLA_HEADS * HEAD_DIM), lambda i: (i, 0))],
        out_shape=[jax.ShapeDtypeStruct((M, MLA_HEADS * MLA_SLOT), bf16),
                   jax.ShapeDtypeStruct((M, MLA_HEADS * HEAD_DIM), bf16)],
        compiler_params=_params(("parallel",)),
        name="mla_kv_prep",
    )(ckv, g, wuk, wuv, *ropes)


FLAG_FIRST, FLAG_LAST, FLAG_CAUSAL = 1, 2, 4


def _tile_schedule(nq, tq, tk, first_key_tile):
    qs, ks, fl = [], [], []
    for qi in range(nq):
        lo = first_key_tile(qi)
        hi = (qi * tq + tq - 1) // tk
        for ki in range(lo, hi + 1):
            qs.append(qi)
            ks.append(ki)
            fl.append((FLAG_FIRST if ki == lo else 0) | (FLAG_LAST if ki == hi else 0)
                      | (FLAG_CAUSAL if ki * tk + tk - 1 > qi * tq else 0))
    return tuple(jnp.asarray(np.array(a, np.int32)) for a in (qs, ks, fl))


def _flash_scratch(heads, tq, tk):
    stat = pltpu.VMEM((heads, tq, LANES), f32)
    return [stat, stat, pltpu.VMEM((heads, tq, HEAD_DIM), f32),
            pltpu.VMEM((3, tq, tk), f32), pltpu.VMEM((2, tq, tk), bf16), pltpu.VMEM((2, tq, LANES), f32),
            pltpu.VMEM((2, tq, LANES), f32), pltpu.VMEM((tq, tk), f32)]


def _flash_init(m_ref, l_ref, acc_ref):
    m_ref[...] = jnp.full(m_ref.shape, M_INIT, f32)
    l_ref[...] = jnp.zeros(l_ref.shape, f32)
    acc_ref[...] = jnp.zeros(acc_ref.shape, f32)


def _flash_heads(nheads, tq, tk, score_fn, add_fn, value_fn, m_ref, l_ref, acc_ref, s_scr, p_scr, mx_scr, a_scr):
    nkc = tk // LANES
    chunks = [(pl.ds(c * ROW_CHUNK, ROW_CHUNK), c * ROW_CHUNK) for c in range(tq // ROW_CHUNK)]

    def scores(h):
        s_scr[h % 3] = score_fn(h)

    def pass_a(h):
        s_ref = s_scr.at[h % 3]
        for rows, r0 in chunks:
            mx = None
            for kc in range(nkc):
                cols = slice(kc * LANES, (kc + 1) * LANES)
                s = s_ref[rows, cols]
                if add_fn is not None:
                    for term in add_fn(h, rows, r0, kc):
                        s = s + term
                    s_ref[rows, cols] = s
                mx = s if mx is None else jnp.maximum(mx, s)
            mx_scr[h % 2, rows, :] = mx
        m_prev = m_ref[h]
        m_new = jnp.maximum(m_prev, jnp.max(mx_scr[h % 2], axis=-1, keepdims=True))
        a_scr[h % 2] = jnp.exp2(m_prev - m_new)
        m_ref[h] = m_new

    def pass_b(h):
        s_ref = s_scr.at[h % 3]
        for rows, _ in chunks:
            m_rows = m_ref[h, rows, :]
            for kc in range(nkc):
                cols = slice(kc * LANES, (kc + 1) * LANES)
                p_scr[h % 2, rows, cols] = jnp.exp2(s_ref[rows, cols] - m_rows).astype(bf16)
        pv = jnp.dot(p_scr[h % 2], value_fn(h), preferred_element_type=f32)
        alpha = a_scr[h % 2]
        acc_ref[h] = alpha * acc_ref[h] + pv[:, :HEAD_DIM]
        l_ref[h] = alpha * l_ref[h] + pv[:, HEAD_DIM:]

    scores(0)
    for h in range(nheads):
        if h + 1 < nheads:
            scores(h + 1)
        pass_a(h)
        if h >= 1:
            pass_b(h - 1)
    pass_b(nheads - 1)


def _with_ones(v):
    return jnp.concatenate([v, jnp.ones((v.shape[0], LANES), v.dtype)], axis=1)


def _mla_attn_body(qi_tab, ki_tab, fl_tab, q_ref, k_ref, v_ref, o_ref, m_ref, l_ref, acc_ref,
                   s_scr, p_scr, mx_scr, a_scr, mask_scr, *, hb, tq, tk):
    st = pl.program_id(2)
    qi, ki, fl = qi_tab[st], ki_tab[st], fl_tab[st]

    pl.when((fl & FLAG_FIRST) != 0)(lambda: _flash_init(m_ref, l_ref, acc_ref))

    def score(h):
        qk = slice(h * MLA_SLOT, (h + 1) * MLA_SLOT)
        return lax.dot_general(q_ref[:, qk], k_ref[:, qk], NT, preferred_element_type=f32)

    def value(h):
        return _with_ones(v_ref[:, h * HEAD_DIM:(h + 1) * HEAD_DIM])

    def tile(causal):
        add = None
        if causal:
            row = qi * tq + lax.broadcasted_iota(i32, (tq, tk), 0)
            col = ki * tk + lax.broadcasted_iota(i32, (tq, tk), 1)
            mask_scr[...] = jnp.where(row >= col, 0.0, NEG_INF)
            add = lambda h, rows, r0, kc: (mask_scr[rows, kc * LANES:(kc + 1) * LANES],)
        _flash_heads(hb, tq, tk, score, add, value, m_ref, l_ref, acc_ref, s_scr, p_scr, mx_scr, a_scr)

    pl.when((fl & FLAG_CAUSAL) != 0)(lambda: tile(True))
    pl.when((fl & FLAG_CAUSAL) == 0)(lambda: tile(False))

    @pl.when((fl & FLAG_LAST) != 0)
    def _():
        for h in range(hb):
            o_ref[:, h * HEAD_DIM:(h + 1) * HEAD_DIM] = (acc_ref[h] / l_ref[h]).astype(o_ref.dtype)


def _mla_attn(q, k, v, B, T, hb=MLA_HEADS, tq=Q_TILE, tk=K_TILE):
    tk = min(tk, T)
    nq, nk = T // tq, T // tk
    tabs = _tile_schedule(nq, tq, tk, lambda qi: 0)
    grid_spec = pltpu.PrefetchScalarGridSpec(
        num_scalar_prefetch=3,
        grid=(B, MLA_HEADS // hb, int(tabs[0].shape[0])),
        in_specs=[pl.BlockSpec((tq, hb * MLA_SLOT), lambda b, h, s, qt, kt, ft: (b * nq + qt[s], h)),
                  pl.BlockSpec((tk, hb * MLA_SLOT), lambda b, h, s, qt, kt, ft: (b * nk + kt[s], h)),
                  pl.BlockSpec((tk, hb * HEAD_DIM), lambda b, h, s, qt, kt, ft: (b * nk + kt[s], h))],
        out_specs=pl.BlockSpec((tq, hb * HEAD_DIM), lambda b, h, s, qt, kt, ft: (b * nq + qt[s], h)),
        scratch_shapes=_flash_scratch(hb, tq, tk))
    return pl.pallas_call(
        functools.partial(_mla_attn_body, hb=hb, tq=tq, tk=tk),
        grid_spec=grid_spec,
        out_shape=jax.ShapeDtypeStruct((B * T, MLA_HEADS * HEAD_DIM), bf16),
        compiler_params=_params(("parallel", "parallel", "arbitrary")),
        name="mla_flash_attention",
    )(*tabs, q, k, v)


def _bias_table_body(rev_ref, o_ref):
    width = rev_ref.shape[-1]
    x = jnp.broadcast_to(rev_ref[0], (LANES, width))
    rolled = pltpu.roll(x, 0, 1, stride=1, stride_axis=0)
    for kb in range(width // LANES):
        o_ref[0, kb] = rolled[:, kb * LANES:(kb + 1) * LANES]


def _bias_table(rev):
    H, _, U = rev.shape
    return pl.pallas_call(
        _bias_table_body,
        grid=(H,),
        in_specs=[pl.BlockSpec((1, 1, U), lambda h: (h, 0, 0))],
        out_specs=pl.BlockSpec((1, U // LANES, LANES, LANES), lambda h: (h, 0, 0, 0)),
        out_shape=jax.ShapeDtypeStruct((H, U // LANES, LANES, LANES), f32),
        compiler_params=_params(("parallel",)),
        name="relative_bias_toeplitz_table",
    )(rev)


def _nsa_band_body(qi_tab, ki_tab, fl_tab, q_ref, k_ref, v_ref, g_ref, *rest, tq, tk, seq, select, kb_lo):
    if select:
        sel_ref, e_ref, o_ref, m_ref, l_ref, acc_ref, s_scr, p_scr, mx_scr, a_scr, mask_scr = rest
    else:
        oc_ref, os_ref, gate_ref, o_ref, m_ref, l_ref, acc_ref, s_scr, p_scr, mx_scr, a_scr, mask_scr = rest
    st = pl.program_id(2)
    qi, ki, fl = qi_tab[st], ki_tab[st], fl_tab[st]

    pl.when((fl & FLAG_FIRST) != 0)(lambda: _flash_init(m_ref, l_ref, acc_ref))

    d0 = qi * tq - ki * tk
    dist = d0 + lax.broadcasted_iota(i32, (tq, tk), 0) - lax.broadcasted_iota(i32, (tq, tk), 1)
    if select:
        picked = jnp.dot(sel_ref[...], e_ref[...], preferred_element_type=f32)
        valid = (dist >= 0) & (picked > 0.5)
    else:
        valid = (dist >= 0) & (dist < WINDOW)
    mask_scr[...] = jnp.where(valid, 0.0, NEG_INF)
    ng = k_ref.shape[0]
    ks = [k_ref[gi].astype(bf16) for gi in range(ng)]
    vs = [_with_ones(v_ref[gi].astype(bf16)) for gi in range(ng)]
    kb0 = (seq - d0) // LANES - kb_lo

    def score(h):
        return lax.dot_general(q_ref[:, h * HEAD_DIM:(h + 1) * HEAD_DIM], ks[h // NSA_HPG], NT,
                               preferred_element_type=f32)

    def add(h, rows, r0, kc):
        bias = g_ref[h, kb0 - r0 // LANES + kc, pl.ds(r0 % LANES, ROW_CHUNK), :]
        return bias, mask_scr[rows, kc * LANES:(kc + 1) * LANES]

    _flash_heads(ng * NSA_HPG, tq, tk, score, add, lambda h: vs[h // NSA_HPG], m_ref, l_ref, acc_ref,
                 s_scr, p_scr, mx_scr, a_scr)

    @pl.when((fl & FLAG_LAST) != 0)
    def _():
        for h in range(ng * NSA_HPG):
            sl = slice(h * HEAD_DIM, (h + 1) * HEAD_DIM)
            o = acc_ref[h] / l_ref[h]
            if not select:
                gate = gate_ref[...]
                g0 = (h // NSA_HPG) * LANES + h % NSA_HPG
                o = (gate[:, g0:g0 + 1] * oc_ref[:, sl].astype(f32)
                     + gate[:, g0 + NSA_HPG:g0 + NSA_HPG + 1] * os_ref[:, sl].astype(f32)
                     + gate[:, g0 + 2 * NSA_HPG:g0 + 2 * NSA_HPG + 1] * o)
            o_ref[:, sl] = o.astype(o_ref.dtype)


def _nsa_band(q, kv, gtab, B, T, kcol, vcol, tk, sel=None, emat=None, merge=None, tq=Q_TILE):
    select = sel is not None
    nq, nk = T // tq, T // tk
    first = (lambda qi: 0) if select else (lambda qi: max(qi * tq - (WINDOW - 1), 0) // tk)
    tabs = _tile_schedule(nq, tq, tk, first)
    if select:
        ng, kb_lo = 1, 0
    else:
        ng = NSA_GROUPS
        kb_lo = (T - (WINDOW + tk - 1) // tk * tk) // LANES - tq // LANES
        gtab = gtab[:, kb_lo:T // LANES + tk // LANES]
    gw = ng * NSA_HPG * HEAD_DIM
    in_specs = [
        pl.BlockSpec((tq, gw), lambda b, g, s, qt, kt, ft: (b * nq + qt[s], g)),
        pl.BlockSpec((ng, tk, HEAD_DIM), lambda b, g, s, qt, kt, ft: ((kcol * NSA_GROUPS) // ng + g, b * nk + kt[s], 0)),
        pl.BlockSpec((ng, tk, HEAD_DIM), lambda b, g, s, qt, kt, ft: ((vcol * NSA_GROUPS) // ng + g, b * nk + kt[s], 0)),
        pl.BlockSpec((ng * NSA_HPG,) + gtab.shape[1:], lambda b, g, s, qt, kt, ft: (g, 0, 0, 0),
                     pipeline_mode=pl.Buffered(1)),
    ]
    args = [q, kv, kv, gtab]
    if select:
        in_specs += [pl.BlockSpec((tq, LANES), lambda b, g, s, qt, kt, ft: (b * nq + qt[s], g)),
                     pl.BlockSpec((LANES, tk), lambda b, g, s, qt, kt, ft: (0, kt[s]))]
        args += [sel, emat]
    else:
        o_other = pl.BlockSpec((tq, gw), lambda b, g, s, qt, kt, ft: (b * nq + qt[s], g))
        in_specs += [o_other, o_other,
                     pl.BlockSpec((tq, ng * LANES), lambda b, g, s, qt, kt, ft: (b * nq + qt[s], g))]
        args += list(merge)
    grid_spec = pltpu.PrefetchScalarGridSpec(
        num_scalar_prefetch=3,
        grid=(B, NSA_GROUPS // ng, int(tabs[0].shape[0])),
        in_specs=in_specs,
        out_specs=pl.BlockSpec((tq, gw), lambda b, g, s, qt, kt, ft: (b * nq + qt[s], g)),
        scratch_shapes=_flash_scratch(ng * NSA_HPG, tq, tk))
    return pl.pallas_call(
        functools.partial(_nsa_band_body, tq=tq, tk=tk, seq=T, select=select, kb_lo=kb_lo),
        grid_spec=grid_spec,
        out_shape=jax.ShapeDtypeStruct((B * T, NSA_HEADS * HEAD_DIM), bf16),
        compiler_params=_params(("parallel", "parallel", "arbitrary")),
        name="nsa_selected_attention" if select else "nsa_window_attention",
    )(*tabs, *args)


def _compress_body(x_ref, pe_ref, w1_ref, w2_ref, o_ref, ot_ref, *, nc):
    half = CMP_BLOCK // 2
    h_lo = jnp.zeros((nc, CMP_HIDDEN), f32)
    h_hi = jnp.zeros((nc, CMP_HIDDEN), f32)
    for p in range(half):
        rows = x_ref[0, pl.ds(p, nc, stride=CMP_STRIDE), :]
        h_lo += jnp.dot((rows + pe_ref[0, p:p + 1, :]).astype(bf16), w1_ref[0, p], preferred_element_type=f32)
        h_hi += jnp.dot((rows + pe_ref[0, half + p:half + p + 1, :]).astype(bf16), w1_ref[0, half + p],
                        preferred_element_type=f32)
    hid = h_lo + pltpu.roll(h_hi, nc - 1, 0)
    out = jnp.dot(jax.nn.gelu(hid).astype(bf16), w2_ref[0], preferred_element_type=f32)
    out = jnp.where(lax.broadcasted_iota(i32, out.shape, 0) < nc - 1, out, 0.0)
    o_ref[0, 0, 0] = out.astype(bf16)
    ot_ref[0, 0, 0] = out.T.astype(bf16)


def _compress(kv, pe, w1, w2, B, T):
    nc = T // CMP_STRIDE
    return pl.pallas_call(
        functools.partial(_compress_body, nc=nc),
        grid=(2, B, NSA_GROUPS),
        in_specs=[pl.BlockSpec((1, T, HEAD_DIM), lambda c, b, g: (c * NSA_GROUPS + g, b, 0)),
                  pl.BlockSpec((1, CMP_BLOCK, HEAD_DIM), lambda c, b, g: (c, 0, 0)),
                  pl.BlockSpec((1, CMP_BLOCK, HEAD_DIM, CMP_HIDDEN), lambda c, b, g: (c, 0, 0, 0)),
                  pl.BlockSpec((1, CMP_HIDDEN, HEAD_DIM), lambda c, b, g: (c, 0, 0))],
        out_specs=[pl.BlockSpec((1, 1, 1, nc, HEAD_DIM), lambda c, b, g: (c, b, g, 0, 0)),
                   pl.BlockSpec((1, 1, 1, HEAD_DIM, nc), lambda c, b, g: (c, b, g, 0, 0))],
        out_shape=[jax.ShapeDtypeStruct((2, B, NSA_GROUPS, nc, HEAD_DIM), bf16),
                   jax.ShapeDtypeStruct((2, B, NSA_GROUPS, HEAD_DIM, nc), bf16)],
        compiler_params=_params(("parallel", "parallel", "parallel")),
        name="nsa_compress_mlp",
    )(kv, pe, w1, w2)


def _cmp_body(q_ref, kc_ref, vct_ref, cw_ref, ovt_ref, o_ref, sel_ref, *, nc, nq):
    tq = CMP_TQ
    qi = pl.program_id(2)
    t0 = qi * tq
    key_row = lax.broadcasted_iota(i32, (nc, tq), 0)
    q_col = lax.broadcasted_iota(i32, (nc, tq), 1)
    valid = (t0 + q_col - CMP_STRIDE * key_row - (CMP_BLOCK - 1)) >= 0
    kc = kc_ref[0, 0, 0]
    vct = vct_ref[0, 0, 0]
    psum = jnp.zeros((nc, tq), f32)
    for h in range(NSA_HPG):
        bias = jnp.concatenate([cw_ref[h, qi - c + nq] for c in range(nc // 8)], axis=0)
        s = lax.dot_general(kc, q_ref[:, h * HEAD_DIM:(h + 1) * HEAD_DIM], NT, preferred_element_type=f32)
        s = jnp.where(valid, s + bias, NEG_INF)
        m = jnp.max(s, axis=0, keepdims=True)
        e = jnp.where(valid, jnp.exp2(s - m), 0.0)
        l = jnp.sum(e, axis=0, keepdims=True)
        p = e / jnp.where(l > 0.0, l, 1.0)
        psum += p
        o_t = jnp.dot(vct, p.astype(bf16), preferred_element_type=f32)
        o_ref[:, h * HEAD_DIM:(h + 1) * HEAD_DIM] = o_t.T.astype(o_ref.dtype)

    imp = jnp.dot(ovt_ref[...], psum, preferred_element_type=f32, precision=lax.Precision.HIGHEST)
    nrow = imp.shape[0]
    n_slc = (nc * CMP_STRIDE) // SLC_BLOCK
    j = lax.broadcasted_iota(i32, (nrow, tq), 0)
    t = t0 + lax.broadcasted_iota(i32, (nrow, tq), 1)
    cur = jnp.right_shift(t, 6)
    ok = j * SLC_BLOCK <= t
    forced = (j == 0) | (j == cur) | (j == cur - 1)
    score = jnp.where(forced, FORCED_SCORE, jnp.where(ok, imp, NEG_INF))
    score = jnp.where(j < n_slc, score, PAD_SCORE)
    chosen = jnp.zeros((nrow, tq), f32)
    for _ in range(min(SLC_TOPK, n_slc)):
        top = jnp.max(score, axis=0, keepdims=True)
        first = jnp.min(jnp.where(score == top, j, nrow), axis=0, keepdims=True)
        hit = j == first
        chosen = jnp.where(hit, 1.0, chosen)
        score = jnp.where(hit, PAD_SCORE, score)
    chosen = jnp.where(ok & (j < n_slc), chosen, 0.0)
    sel_ref[...] = chosen.T.astype(sel_ref.dtype)


def _nsa_cmp(q, kc, vct, cw, ovt, B, T):
    nc = T // CMP_STRIDE
    nq = T // CMP_TQ
    gw = NSA_HPG * HEAD_DIM
    return pl.pallas_call(
        functools.partial(_cmp_body, nc=nc, nq=nq),
        grid=(B, NSA_GROUPS, nq),
        in_specs=[pl.BlockSpec((CMP_TQ, gw), lambda b, g, qi: (b * nq + qi, g)),
                  pl.BlockSpec((1, 1, 1, nc, HEAD_DIM), lambda b, g, qi: (0, b, g, 0, 0)),
                  pl.BlockSpec((1, 1, 1, HEAD_DIM, nc), lambda b, g, qi: (1, b, g, 0, 0)),
                  pl.BlockSpec((NSA_HPG, 2 * nq, 8, CMP_TQ), lambda b, g, qi: (g, 0, 0, 0)),
                  pl.BlockSpec(ovt.shape, lambda b, g, qi: (0, 0))],
        out_specs=[pl.BlockSpec((CMP_TQ, gw), lambda b, g, qi: (b * nq + qi, g)),
                   pl.BlockSpec((CMP_TQ, LANES), lambda b, g, qi: (b * nq + qi, g))],
        out_shape=[jax.ShapeDtypeStruct((B * T, NSA_HEADS * HEAD_DIM), bf16),
                   jax.ShapeDtypeStruct((B * T, NSA_GROUPS * LANES), bf16)],
        compiler_params=_params(("parallel", "parallel", "parallel")),
        name="nsa_compressed_attention_select",
    )(q, kc, vct, cw, ovt)


def _layer_norm(y, g, b):
    mu = jnp.mean(y, axis=-1, keepdims=True)
    var = jnp.mean(jnp.square(y - mu), axis=-1, keepdims=True)
    return (y - mu) * lax.rsqrt(var + LN_EPS) * g + b


def _branch_merge_body(a_ref, b_ref, wa_ref, wb_ref, ga_ref, gb_ref, o_ref):
    ya = jnp.dot(a_ref[...], wa_ref[...], preferred_element_type=f32)
    yb = jnp.dot(b_ref[...], wb_ref[...], preferred_element_type=f32)
    o_ref[...] = (ga_ref[...] * ya + gb_ref[...] * yb).astype(o_ref.dtype)


def _branch_merge(a, b, wa, wb, mg, tm, tn):
    M, K = a.shape
    Nc = wa.shape[1]
    nj = Nc // tn
    return pl.pallas_call(
        _branch_merge_body,
        grid=(M // tm, nj),
        in_specs=[pl.BlockSpec((tm, K), lambda i, j: (i, 0)), pl.BlockSpec((tm, K), lambda i, j: (i, 0)),
                  pl.BlockSpec((K, tn), lambda i, j: (0, j)), pl.BlockSpec((K, tn), lambda i, j: (0, j)),
                  pl.BlockSpec((tm, tn), lambda i, j: (i, j)), pl.BlockSpec((tm, tn), lambda i, j: (i, nj + j))],
        out_specs=pl.BlockSpec((tm, tn), lambda i, j: (i, j)),
        out_shape=jax.ShapeDtypeStruct((M, Nc), bf16),
        compiler_params=_params(("parallel", "parallel")),
        name="branch_proj_merge",
    )(a, b, wa, wb, mg, mg)


def _proj_ln_body(z_ref, w_ref, x_ref, g_ref, b_ref, o_ref):
    y = jnp.dot(z_ref[...], w_ref[...], preferred_element_type=f32)
    o_ref[...] = _layer_norm(ALPHA * x_ref[...] + y, g_ref[...], b_ref[...])


def _proj_ln(z, w, x, g, b, tm):
    M, K = z.shape
    D = w.shape[1]
    vec = pl.BlockSpec((1, D), lambda i: (0, 0))
    return pl.pallas_call(
        _proj_ln_body,
        grid=(M // tm,),
        in_specs=[pl.BlockSpec((tm, K), lambda i: (i, 0)), pl.BlockSpec((K, D), lambda i: (0, 0)),
                  pl.BlockSpec((tm, D), lambda i: (i, 0)), vec, vec],
        out_specs=pl.BlockSpec((tm, D), lambda i: (i, 0)),
        out_shape=jax.ShapeDtypeStruct((M, D), f32),
        compiler_params=_params(("parallel",)),
        name="mix_out_proj_layernorm",
    )(z, w, x, g, b)


def _mem_attn_body(x_ref, wq_ref, k_ref, v_ref, wo_ref, g_ref, b_ref, o_ref, ob_ref):
    x = x_ref[...]
    q = (jnp.dot(x.astype(bf16), wq_ref[...], preferred_element_type=f32) * HEAD_DIM ** -0.5).astype(bf16)
    heads = []
    for h in range(MEM_HEADS):
        sl = slice(h * HEAD_DIM, (h + 1) * HEAD_DIM)
        s = lax.dot_general(q[:, sl], k_ref[0, :, sl], NT, preferred_element_type=f32)
        e = jnp.exp(s - jnp.max(s, axis=-1, keepdims=True))
        p = e / jnp.sum(e, axis=-1, keepdims=True)
        heads.append(jnp.dot(p.astype(bf16), v_ref[0, :, sl], preferred_element_type=f32).astype(bf16))
    y = jnp.dot(jnp.concatenate(heads, axis=1), wo_ref[...], preferred_element_type=f32)
    out = _layer_norm(ALPHA * x + y, g_ref[...], b_ref[...])
    o_ref[...] = out
    ob_ref[...] = out.astype(bf16)


def _mem_attn(x, wq, km, vm, wo, g, b, T, tm):
    M, D = x.shape
    nt = T // tm
    W = wq.shape[1]
    vec = pl.BlockSpec((1, D), lambda i: (0, 0))
    blk = pl.BlockSpec((tm, D), lambda i: (i, 0))
    return pl.pallas_call(
        _mem_attn_body,
        grid=(M // tm,),
        in_specs=[blk, pl.BlockSpec((D, W), lambda i: (0, 0)),
                  pl.BlockSpec((1, N_MEM, W), lambda i: (i // nt, 0, 0)),
                  pl.BlockSpec((1, N_MEM, W), lambda i: (i // nt, 0, 0)),
                  pl.BlockSpec((W, D), lambda i: (0, 0)), vec, vec],
        out_specs=[blk, blk],
        out_shape=[jax.ShapeDtypeStruct((M, D), f32), jax.ShapeDtypeStruct((M, D), bf16)],
        compiler_params=_params(("parallel",)),
        name="memory_cross_attention_layernorm",
    )(x, wq, km, vm, wo, g, b)


def _add_ln_body(x_ref, y_ref, g_ref, b_ref, o_ref):
    o_ref[...] = _layer_norm(ALPHA * x_ref[...] + y_ref[...], g_ref[...], b_ref[...])


def _add_ln(x, y, g, b, tm):
    M, D = x.shape
    vec = pl.BlockSpec((1, D), lambda i: (0, 0))
    blk = pl.BlockSpec((tm, D), lambda i: (i, 0))
    return pl.pallas_call(
        _add_ln_body,
        grid=(M // tm,),
        in_specs=[blk, blk, vec, vec],
        out_specs=blk,
        out_shape=jax.ShapeDtypeStruct((M, D), f32),
        compiler_params=_params(("parallel",)),
        name="residual_layernorm",
    )(x, y, g, b)


def _topk_rows(s, ids, k):
    n = s.shape[1]
    slot = lax.broadcasted_iota(i32, (k, n), 0)
    vals = jnp.zeros((k, n), f32)
    inds = jnp.zeros((k, n), f32)
    for r in range(k):
        top = jnp.max(s, axis=0, keepdims=True)
        first = jnp.min(jnp.where(s == top, ids, PAD_ID), axis=0, keepdims=True)
        vals = jnp.where(slot == r, top, vals)
        inds = jnp.where(slot == r, first, inds)
        s = jnp.where(ids == first, PAD_SCORE, s)
    return vals, inds


def _pick_rows(table, sel):
    out = jnp.zeros(sel.shape, table.dtype)
    for a in range(table.shape[0]):
        out = jnp.where(sel == a, table[a:a + 1, :], out)
    return out


CAND_COUNTS = tuple(PEER_TOPK // (a + 1) for a in range(PEER_TOPK))
CAND_ROWS = -(-sum(CAND_COUNTS) // 8) * 8


def _cand_ids(tn):
    ids = [a * PEER_TOPK + b for a, nb in enumerate(CAND_COUNTS) for b in range(nb)]
    ids = np.array(ids + [PAD_ID] * (CAND_ROWS - len(ids)), np.float32)
    return jnp.asarray(np.broadcast_to(ids[:, None], (CAND_ROWS, tn)))


def _peer_route_body(q_ref, keys_ref, ids_ref, e1_ref, e2_ref, gt_ref, e1_scr, e2_scr, g_scr, cand_scr):
    k1 = keys_ref[0].astype(bf16)
    k2 = keys_ref[1].astype(bf16)
    tn = q_ref.shape[0]
    key_ids = lax.broadcasted_iota(i32, (PEER_N_KEYS, tn), 0).astype(f32)
    cand_scr[...] = jnp.full(cand_scr.shape, PAD_SCORE, f32)
    for h in range(PEER_HEADS):
        q1 = q_ref[:, (2 * h) * LANES:(2 * h + 1) * LANES]
        q2 = q_ref[:, (2 * h + 1) * LANES:(2 * h + 2) * LANES]
        s1 = lax.dot_general(k1, q1, NT, preferred_element_type=f32)
        s2 = lax.dot_general(k2, q2, NT, preferred_element_type=f32)
        v1, i1 = _topk_rows(s1, key_ids, PEER_TOPK)
        v2, i2 = _topk_rows(s2, key_ids, PEER_TOPK)
        off = 0
        for a, nb in enumerate(CAND_COUNTS):
            cand_scr[off:off + nb, :] = v1[a:a + 1, :] + v2[0:nb, :]
            off += nb
        top_s, flat = _topk_rows(cand_scr[...], ids_ref[...], PEER_TOPK)
        flat = flat.astype(i32)
        e = jnp.exp(top_s - top_s[0:1, :])
        rows = slice(h * PEER_TOPK, (h + 1) * PEER_TOPK)
        g_scr[rows, :] = e / jnp.sum(e, axis=0, keepdims=True)
        e1_scr[rows, :] = _pick_rows(i1, jnp.right_shift(flat, 4)).astype(i32)
        e2_scr[rows, :] = _pick_rows(i2, jnp.bitwise_and(flat, PEER_TOPK - 1)).astype(i32)
    e1_ref[...] = e1_scr[...].T
    e2_ref[...] = e2_scr[...].T
    gt_ref[...] = g_scr[...].T


def _peer_route(q, keys, tn=LANES):
    M, W = q.shape
    out = pl.BlockSpec((tn, PEER_HK), lambda i: (i, 0))
    return pl.pallas_call(
        _peer_route_body,
        grid=(M // tn,),
        in_specs=[pl.BlockSpec((tn, W), lambda i: (i, 0)), pl.BlockSpec(keys.shape, lambda i: (0, 0, 0)),
                  pl.BlockSpec((CAND_ROWS, tn), lambda i: (0, 0))],
        out_specs=[out, out, out],
        out_shape=[jax.ShapeDtypeStruct((M, PEER_HK), i32), jax.ShapeDtypeStruct((M, PEER_HK), i32),
                   jax.ShapeDtypeStruct((M, PEER_HK), f32)],
        scratch_shapes=[pltpu.VMEM((PEER_HK, tn), i32), pltpu.VMEM((PEER_HK, tn), i32), pltpu.VMEM((PEER_HK, tn), f32),
                        pltpu.VMEM((CAND_ROWS, tn), f32)],
        compiler_params=_params(("parallel",)),
        name="peer_product_key_routing",
    )(q, keys, _cand_ids(tn))


TOKEN_GROUP = 16


def _peer_weights_body(e1_ref, e2_ref, g_ref, w_ref, *, tn):
    sub = lax.broadcasted_iota(i32, (PEER_N_KEYS, PEER_HK), 0)

    def one(n):
        r1 = e1_ref[pl.ds(n, 1), :]
        r2 = e2_ref[pl.ds(n, 1), :]
        g = g_ref[pl.ds(n, 1), :]
        a_t = (sub == r1).astype(bf16)
        b_t = jnp.where(sub == r2, g, 0.0).astype(bf16)
        return lax.dot_general(a_t, b_t, NT, preferred_element_type=f32)

    def group(i, carry):
        n0 = pl.multiple_of(i * TOKEN_GROUP, TOKEN_GROUP)
        tables = jnp.stack([one(n0 + t) for t in range(TOKEN_GROUP)], axis=0)
        w_ref[:, pl.ds(n0, TOKEN_GROUP), :] = jnp.swapaxes(tables, 0, 1).astype(w_ref.dtype)
        return carry

    lax.fori_loop(0, tn // TOKEN_GROUP, group, 0)


def _peer_weights(e1, e2, g, tn=64):
    M = e1.shape[0]
    blk = pl.BlockSpec((tn, PEER_HK), lambda i: (i, 0))
    return pl.pallas_call(
        functools.partial(_peer_weights_body, tn=tn),
        grid=(M // tn,),
        in_specs=[blk, blk, blk],
        out_specs=pl.BlockSpec((PEER_N_KEYS, tn, PEER_N_KEYS), lambda i: (0, i, 0)),
        out_shape=jax.ShapeDtypeStruct((PEER_N_KEYS, M, PEER_N_KEYS), bf16),
        compiler_params=_params(("parallel",)),
        name="peer_routing_weights",
    )(e1, e2, g)


GELU_SLOPE = 2.0 * math.sqrt(2.0 / math.pi)


def _gelu_tanh(x):
    minus_2z = x * ((-GELU_SLOPE * 0.044715 * LOG2E) * (x * x) - GELU_SLOPE * LOG2E)
    return x / (1.0 + jnp.exp2(minus_2z))


def _peer_dense_body(x_ref, w_ref, u_ref, v_ref, o_ref, *, rows):
    @pl.when(pl.program_id(1) == 0)
    def _():
        o_ref[...] = jnp.zeros(o_ref.shape, f32)

    act = _gelu_tanh(lax.dot_general(x_ref[...], u_ref[...].astype(bf16), NT, preferred_element_type=f32))
    w = jnp.concatenate([w_ref[r] for r in range(rows)], axis=1)
    o_ref[...] += jnp.dot(w * act.astype(bf16), v_ref[...].astype(bf16), preferred_element_type=f32)


def _peer_dense(xb, w, u, v, tm=1024, rows=4):
    M, D = xb.shape
    te = rows * PEER_N_KEYS
    return pl.pallas_call(
        functools.partial(_peer_dense_body, rows=rows),
        grid=(M // tm, PEER_N_KEYS // rows),
        in_specs=[pl.BlockSpec((tm, D), lambda i, e: (i, 0)),
                  pl.BlockSpec((rows, tm, PEER_N_KEYS), lambda i, e: (e, i, 0)),
                  pl.BlockSpec((te, D), lambda i, e: (e, 0)), pl.BlockSpec((te, D), lambda i, e: (e, 0))],
        out_specs=pl.BlockSpec((tm, D), lambda i, e: (i, 0)),
        out_shape=jax.ShapeDtypeStruct((M, D), f32),
        compiler_params=_params(("parallel", "arbitrary")),
        name="peer_dense_experts",
    )(xb, w, u, v)


def _t5_bucket(dist):
    dist = jnp.maximum(dist, 0)
    n_log = REL_BUCKETS - REL_MAX_EXACT
    large = REL_MAX_EXACT + (jnp.log(jnp.maximum(dist, 1).astype(f32) / REL_MAX_EXACT)
                             / math.log(REL_MAX_DIST / REL_MAX_EXACT) * n_log).astype(i32)
    large = jnp.minimum(large, REL_BUCKETS - 1)
    return jnp.where(dist < REL_MAX_EXACT, dist, large)


def _position_tables(rel_bias, T):
    by_dist = (rel_bias[_t5_bucket(jnp.arange(T))] * LOG2E).T.astype(f32)
    rev = jnp.pad(by_dist[:, ::-1], ((0, 0), (1, K_TILE - 1)), mode="edge")[:, None, :]
    shift = (CMP_BLOCK - 1) + CMP_STRIDE * 7
    ext = jnp.pad(by_dist, ((0, 0), (T + shift, 0)), mode="edge")
    rows = [ext[:, CMP_STRIDE * (7 - r):CMP_STRIDE * (7 - r) + 2 * T] for r in range(8)]
    cmpw = jnp.stack(rows, axis=1).reshape(NSA_HEADS, 8, 2 * T // CMP_TQ, CMP_TQ)
    return rev, jnp.transpose(cmpw, (0, 2, 1, 3))


def _rope_slot_tables(T):
    inv = ROPE_THETA ** (-jnp.arange(0, MLA_ROPE, 2, dtype=f32) / MLA_ROPE)
    ang = jnp.arange(T, dtype=f32)[:, None] * inv[None, :]
    cos, sin = jnp.cos(ang), jnp.sin(ang)
    z = jnp.zeros_like(cos)
    ta = jnp.concatenate([cos, cos, z, z], axis=1)
    tb = jnp.concatenate([-sin, z, z, z], axis=1)
    tc = jnp.concatenate([z, sin, z, z], axis=1)
    return ta, tb, tc


def _layer(x, mem, rel_bias, w_in, cmp_k_pe, cmp_k_w1, cmp_k_w2, cmp_v_pe, cmp_v_w1, cmp_v_w2,
           mla_kv_norm, mla_w_uk, mla_w_uv, w_branch_nsa, w_branch_mla, w_mix_out, ln_mix_g, ln_mix_b,
           mem_w_q, mem_w_k, mem_w_v, mem_w_o, ln_mem_g, ln_mem_b,
           peer_w_query, peer_sub_keys, peer_u, peer_v, ln_ffn_g, ln_ffn_b, ropes):
    B, T, D = x.shape
    N = B * T
    x2d = x.reshape(N, D)
    xb = x2d.astype(bf16)
    row = lambda a: a.reshape(1, -1)

    gate_cols = np.array([[br * NSA_HEADS + g * NSA_HPG + h for br in range(3) for h in range(NSA_HPG)]
                          for g in range(NSA_GROUPS)])
    w_gate = jnp.pad(w_in[:, OFF_NSA_GATE:OFF_MLA_Q][:, gate_cols],
                     ((0, 0), (0, 0), (0, LANES - 3 * NSA_HPG))).reshape(D, NSA_GROUPS * LANES)
    w_mla_q = jnp.pad(w_in[:, OFF_MLA_Q:OFF_MLA_KV].reshape(D, MLA_HEADS, MLA_NOPE + MLA_ROPE),
                      ((0, 0), (0, 0), (0, MLA_SLOT - MLA_NOPE - MLA_ROPE))).reshape(D, MLA_HEADS * MLA_SLOT)
    w_mla_kv = jnp.pad(w_in[:, OFF_MLA_KV:OFF_MERGE], ((0, 0), (0, LANES - MLA_ROPE)))
    segs = [w_mla_kv, jnp.zeros((D, LANES), f32), w_gate,
            w_in[:, :OFF_NSA_KV], w_mla_q, w_in[:, OFF_MERGE:], w_in[:, OFF_NSA_KV:OFF_NSA_GATE]]
    starts = np.concatenate([[0], np.cumsum([s.shape[1] for s in segs])])
    col = lambda i: (int(starts[i]), int(segs[i].shape[1]))
    w_all = jnp.concatenate(segs, axis=1).astype(bf16)

    mla_ckv = _mm(xb, w_all, f32, 1024, MLA_KV_RANK + LANES, name="mla_kv_proj", cols=col(0))
    gates = _mm(xb, w_all, f32, 1024, NSA_GROUPS * LANES, act="sigmoid", name="nsa_gate_proj", cols=col(2))
    nsa_q = _mm(xb, w_all, bf16, 1024, 1024, scale=HEAD_DIM ** -0.5 * LOG2E, name="nsa_q_proj", cols=col(3))
    mla_q = _mla_q(xb, w_all, col(4), ropes, T, 1024, 1024)
    merge = _mm(xb, w_all, f32, 1024, 1024, act="sigmoid", name="merge_gate_proj", cols=col(5))
    nsa_kv = _mm_slabs(xb, w_all, col(6), 1024, 512, "nsa_kv_proj")

    slc_tk = min(K_TILE, T)
    rev_tab, cmp_tab = _position_tables(rel_bias, T)
    bias_tab = _bias_table(rev_tab)
    pe = jnp.stack([cmp_k_pe, cmp_v_pe])
    w1 = jnp.stack([cmp_k_w1, cmp_v_w1]).reshape(2, CMP_BLOCK, HEAD_DIM, CMP_HIDDEN).astype(bf16)
    w2 = jnp.stack([cmp_k_w2, cmp_v_w2]).astype(bf16)
    kvc, kvc_t = _compress(nsa_kv, pe, w1, w2, B, T)
    nc = T // CMP_STRIDE
    n_slc = T // SLC_BLOCK
    cs = np.arange(nc)[None, :] * CMP_STRIDE
    ss = np.arange(LANES)[:, None] * SLC_BLOCK
    ovt = ((cs < ss + SLC_BLOCK) & (cs + CMP_BLOCK > ss) & (np.arange(nc)[None, :] < nc - 1)
           & (np.arange(LANES)[:, None] < n_slc)).astype(np.float32)
    o_cmp, sel = _nsa_cmp(nsa_q, kvc, kvc_t, cmp_tab, jnp.asarray(ovt), B, T)
    emat = (np.arange(T)[None, :] // SLC_BLOCK == np.arange(LANES)[:, None]).astype(np.float32)
    o_slc = _nsa_band(nsa_q, nsa_kv, bias_tab, B, T, 2, 3, slc_tk, sel=sel, emat=jnp.asarray(emat, dtype=bf16))
    o_nsa = _nsa_band(nsa_q, nsa_kv, bias_tab, B, T, 4, 5, WIN_TILE, merge=(o_cmp, o_slc, gates), tq=WIN_TILE)

    mla_k, mla_v = _mla_kv(mla_ckv, row(mla_kv_norm),
                           jnp.asarray(mla_w_uk, bf16), jnp.asarray(mla_w_uv, bf16), ropes, T, 512)
    o_mla = _mla_attn(mla_q, mla_k, mla_v, B, T)

    z = _branch_merge(o_nsa, o_mla, w_branch_nsa.astype(bf16), w_branch_mla.astype(bf16), merge, 1024, 512)
    x1 = _proj_ln(z, w_mix_out.astype(bf16), x2d, row(ln_mix_g), row(ln_mix_b), 512)

    mem2d = mem.reshape(B * N_MEM, D).astype(bf16)
    km = _mm(mem2d, mem_w_k.astype(bf16), bf16, B * N_MEM, 256, name="mem_k_proj").reshape(B, N_MEM, -1)
    vm = _mm(mem2d, mem_w_v.astype(bf16), bf16, B * N_MEM, 256, name="mem_v_proj").reshape(B, N_MEM, -1)
    x2, x2b = _mem_attn(x1, mem_w_q.astype(bf16), km, vm, mem_w_o.astype(bf16), row(ln_mem_g), row(ln_mem_b), T, 512)

    pq = _mm(x2b, peer_w_query.astype(bf16), bf16, 1024, 512, name="peer_query_proj")
    e1, e2, gt = _peer_route(pq, peer_sub_keys)
    wts = _peer_weights(e1, e2, gt)
    y = _peer_dense(x2b, wts, peer_u, peer_v)
    x3 = _add_ln(x2, y, row(ln_ffn_g), row(ln_ffn_b), 512)
    return x3.reshape(B, T, D)


def kernel(x, mem, rel_bias, w_in, cmp_k_pe, cmp_k_w1, cmp_k_w2, cmp_v_pe, cmp_v_w1, cmp_v_w2, mla_kv_norm, mla_w_uk, mla_w_uv, w_branch_nsa, w_branch_mla, w_mix_out, ln_mix_g, ln_mix_b, mem_w_q, mem_w_k, mem_w_v, mem_w_o, ln_mem_g, ln_mem_b, peer_w_query, peer_sub_keys, peer_u, peer_v, ln_ffn_g, ln_ffn_b):
    ropes = _rope_slot_tables(x.shape[1])
    for l in range(DEPTH):
        x = _layer(x, mem, rel_bias, w_in[l], cmp_k_pe[l], cmp_k_w1[l], cmp_k_w2[l], cmp_v_pe[l], cmp_v_w1[l],
                   cmp_v_w2[l], mla_kv_norm[l], mla_w_uk[l], mla_w_uv[l], w_branch_nsa[l], w_branch_mla[l],
                   w_mix_out[l], ln_mix_g[l], ln_mix_b[l], mem_w_q[l], mem_w_k[l], mem_w_v[l], mem_w_o[l],
                   ln_mem_g[l], ln_mem_b[l], peer_w_query[l], peer_sub_keys[l], peer_u[l], peer_v[l],
                   ln_ffn_g[l], ln_ffn_b[l], ropes)
    return x
```

```python
import functools
import math

import numpy as np
import jax
import jax.numpy as jnp
from jax import lax
from jax.experimental import pallas as pl
from jax.experimental.pallas import tpu as pltpu

f32 = jnp.float32
bf16 = jnp.bfloat16
i32 = jnp.int32

D_MODEL = 2048
N_MEM = 256
NSA_HEADS = 16
NSA_GROUPS = 2
NSA_HPG = NSA_HEADS // NSA_GROUPS
HEAD_DIM = 128
CMP_BLOCK = 32
CMP_STRIDE = 16
CMP_HIDDEN = 256
SLC_BLOCK = 64
SLC_TOPK = 16
WINDOW = 512
FORCED_SCORE = 1e4
MLA_HEADS = 16
MLA_NOPE = 128
MLA_ROPE = 64
MLA_KV_RANK = 512
MLA_SLOT = 256
ROPE_THETA = 10000.0
REL_BUCKETS = 32
REL_MAX_EXACT = 16
REL_MAX_DIST = 4096
MEM_HEADS = 4
PEER_HEADS = 8
PEER_N_KEYS = 128
PEER_TOPK = 16
PEER_HK = PEER_HEADS * PEER_TOPK
DEPTH = 1
ALPHA = (2.0 * DEPTH) ** 0.25
NEG_INF = -1e30
M_INIT = -1e29
PAD_SCORE = -3e38
PAD_ID = 1e9
LOG2E = math.log2(math.e)
LN_EPS = 1e-5
RMS_EPS = 1e-6

NSA_Q_COLS = NSA_HEADS * HEAD_DIM
NSA_KV_COLS = NSA_GROUPS * HEAD_DIM
OFF_NSA_KV = NSA_Q_COLS
OFF_NSA_GATE = OFF_NSA_KV + 6 * NSA_KV_COLS
OFF_MLA_Q = OFF_NSA_GATE + 3 * NSA_HEADS
OFF_MLA_KV = OFF_MLA_Q + MLA_HEADS * (MLA_NOPE + MLA_ROPE)
OFF_MERGE = OFF_MLA_KV + MLA_KV_RANK + MLA_ROPE

LANES = 128
Q_TILE = 512
K_TILE = 512
WIN_TILE = 256
ROW_CHUNK = 32
CMP_TQ = 256
VMEM_LIMIT = 56 * 1024 * 1024

NT = (((1,), (1,)), ((), ()))


def _params(sem):
    return pltpu.CompilerParams(dimension_semantics=sem, vmem_limit_bytes=VMEM_LIMIT)


def _mm_body(a_ref, b_ref, o_ref, *, scale, act):
    acc = jnp.dot(a_ref[...].astype(bf16), b_ref[...], preferred_element_type=f32)
    if scale != 1.0:
        acc = acc * scale
    if act == "sigmoid":
        acc = jax.nn.sigmoid(acc)
    o_ref[...] = acc.astype(o_ref.dtype)


def _mm(a, b, out_dtype, tm, tn, scale=1.0, act=None, name="mm", cols=None):
    M, K = a.shape
    c0, Nc = (0, b.shape[1]) if cols is None else cols
    assert c0 % tn == 0 and Nc % tn == 0
    j0 = c0 // tn
    return pl.pallas_call(
        functools.partial(_mm_body, scale=scale, act=act),
        grid=(M // tm, Nc // tn),
        in_specs=[pl.BlockSpec((tm, K), lambda i, j: (i, 0)), pl.BlockSpec((K, tn), lambda i, j: (0, j0 + j))],
        out_specs=pl.BlockSpec((tm, tn), lambda i, j: (i, j)),
        out_shape=jax.ShapeDtypeStruct((M, Nc), out_dtype),
        compiler_params=_params(("parallel", "parallel")),
        name=name,
    )(a, b)


def _mm_slabs_body(a_ref, b_ref, o_ref):
    acc = jnp.dot(a_ref[...], b_ref[...], preferred_element_type=f32)
    for c in range(o_ref.shape[0]):
        o_ref[c] = acc[:, c * LANES:(c + 1) * LANES]


def _mm_slabs(a, b, cols, tm, tn, name):
    M, K = a.shape
    c0, Nc = cols
    assert c0 % tn == 0 and Nc % tn == 0
    j0 = c0 // tn
    return pl.pallas_call(
        _mm_slabs_body,
        grid=(M // tm, Nc // tn),
        in_specs=[pl.BlockSpec((tm, K), lambda i, j: (i, 0)), pl.BlockSpec((K, tn), lambda i, j: (0, j0 + j))],
        out_specs=pl.BlockSpec((tn // LANES, tm, LANES), lambda i, j: (j, i, 0)),
        out_shape=jax.ShapeDtypeStruct((Nc // LANES, M, LANES), f32),
        compiler_params=_params(("parallel", "parallel")),
        name=name,
    )(a, b)


def _rope_slot(y, ta, tb, tc):
    return y * ta + pltpu.roll(y, LANES - 32, 1) * tb + pltpu.roll(y, 32, 1) * tc


def _mlaq_body(a_ref, b_ref, ta_ref, tb_ref, tc_ref, o_ref, *, scale, nslot):
    acc = jnp.dot(a_ref[...], b_ref[...], preferred_element_type=f32)
    ta, tb, tc = ta_ref[...], tb_ref[...], tc_ref[...]
    for s in range(nslot):
        lo = s * MLA_SLOT
        o_ref[:, lo:lo + MLA_NOPE] = (acc[:, lo:lo + MLA_NOPE] * scale).astype(o_ref.dtype)
        rot = _rope_slot(acc[:, lo + MLA_NOPE:lo + MLA_SLOT], ta, tb, tc)
        o_ref[:, lo + MLA_NOPE:lo + MLA_SLOT] = (rot * scale).astype(o_ref.dtype)


def _mla_q(xb, w, cols, ropes, T, tm, tn):
    M, K = xb.shape
    c0, Nc = cols
    assert c0 % tn == 0 and Nc % tn == 0
    j0 = c0 // tn
    nt = T // tm
    tab = pl.BlockSpec((tm, LANES), lambda i, j: (i % nt, 0))
    return pl.pallas_call(
        functools.partial(_mlaq_body, scale=(MLA_NOPE + MLA_ROPE) ** -0.5 * LOG2E, nslot=tn // MLA_SLOT),
        grid=(M // tm, Nc // tn),
        in_specs=[pl.BlockSpec((tm, K), lambda i, j: (i, 0)), pl.BlockSpec((K, tn), lambda i, j: (0, j0 + j)),
                  tab, tab, tab],
        out_specs=pl.BlockSpec((tm, tn), lambda i, j: (i, j)),
        out_shape=jax.ShapeDtypeStruct((M, Nc), bf16),
        compiler_params=_params(("parallel", "parallel")),
        name="mla_q_proj_rope",
    )(xb, w, *ropes)


def _mlakv_body(c_ref, g_ref, wuk_ref, wuv_ref, ta_ref, tb_ref, tc_ref, k_ref, v_ref):
    x = c_ref[...]
    c = x[:, :MLA_KV_RANK]
    ms = jnp.mean(c * c, axis=-1, keepdims=True)
    cn = (c * lax.rsqrt(ms + RMS_EPS) * g_ref[...]).astype(bf16)
    kpe = _rope_slot(x[:, MLA_KV_RANK:], ta_ref[...], tb_ref[...], tc_ref[...]).astype(bf16)
    kn = jnp.dot(cn, wuk_ref[...], preferred_element_type=f32).astype(bf16)
    v_ref[...] = jnp.dot(cn, wuv_ref[...], preferred_element_type=f32).astype(bf16)
    for h in range(MLA_HEADS):
        k_ref[:, h * MLA_SLOT:h * MLA_SLOT + MLA_NOPE] = kn[:, h * MLA_NOPE:(h + 1) * MLA_NOPE]
        k_ref[:, h * MLA_SLOT + MLA_NOPE:(h + 1) * MLA_SLOT] = kpe


def _mla_kv(ckv, g, wuk, wuv, ropes, T, tm):
    M, K = ckv.shape
    nt = T // tm
    tab = pl.BlockSpec((tm, LANES), lambda i: (i % nt, 0))
    full = lambda shape: pl.BlockSpec(shape, lambda i: (0,) * len(shape))
    return pl.pallas_call(
        _mlakv_body,
        grid=(M // tm,),
        in_specs=[pl.BlockSpec((tm, K), lambda i: (i, 0)), full(g.shape), full(wuk.shape), full(wuv.shape), tab, tab, tab],
        out_specs=[pl.BlockSpec((tm, MLA_HEADS * MLA_SLOT), lambda i: (i, 0)),
                   pl.BlockSpec((tm, MLA_HEADS * HEAD_DIM), lambda i: (i, 0))],
        out_shape=[jax.ShapeDtypeStruct((M, MLA_HEADS * MLA_SLOT), bf16),
                   jax.ShapeDtypeStruct((M, MLA_HEADS * HEAD_DIM), bf16)],
        compiler_params=_params(("parallel",)),
        name="mla_kv_prep",
    )(ckv, g, wuk, wuv, *ropes)


FLAG_FIRST, FLAG_LAST, FLAG_CAUSAL = 1, 2, 4


def _tile_schedule(nq, tq, tk, first_key_tile):
    qs, ks, fl = [], [], []
    for qi in range(nq):
        lo = first_key_tile(qi)
        hi = (qi * tq + tq - 1) // tk
        for ki in range(lo, hi + 1):
            qs.append(qi)
            ks.append(ki)
            fl.append((FLAG_FIRST if ki == lo else 0) | (FLAG_LAST if ki == hi else 0)
                      | (FLAG_CAUSAL if ki * tk + tk - 1 > qi * tq else 0))
    return tuple(jnp.asarray(np.array(a, np.int32)) for a in (qs, ks, fl))


def _flash_scratch(heads, tq, tk):
    stat = pltpu.VMEM((heads, tq, LANES), f32)
    return [stat, stat, pltpu.VMEM((heads, tq, HEAD_DIM), f32),
            pltpu.VMEM((3, tq, tk), f32), pltpu.VMEM((2, tq, tk), bf16), pltpu.VMEM((2, tq, LANES), f32),
            pltpu.VMEM((2, tq, LANES), f32), pltpu.VMEM((tq, tk), f32)]


def _flash_init(m_ref, l_ref, acc_ref):
    m_ref[...] = jnp.full(m_ref.shape, M_INIT, f32)
    l_ref[...] = jnp.zeros(l_ref.shape, f32)
    acc_ref[...] = jnp.zeros(acc_ref.shape, f32)


def _flash_heads(nheads, tq, tk, score_fn, add_fn, value_fn, m_ref, l_ref, acc_ref, s_scr, p_scr, mx_scr, a_scr):
    nkc = tk // LANES
    chunks = [(pl.ds(c * ROW_CHUNK, ROW_CHUNK), c * ROW_CHUNK) for c in range(tq // ROW_CHUNK)]

    def scores(h):
        s_scr[h % 3] = score_fn(h)

    def pass_a(h):
        s_ref = s_scr.at[h % 3]
        for rows, r0 in chunks:
            mx = None
            for kc in range(nkc):
                cols = slice(kc * LANES, (kc + 1) * LANES)
                s = s_ref[rows, cols]
                if add_fn is not None:
                    for term in add_fn(h, rows, r0, kc):
                        s = s + term
                    s_ref[rows, cols] = s
                mx = s if mx is None else jnp.maximum(mx, s)
            mx_scr[h % 2, rows, :] = mx
        m_prev = m_ref[h]
        m_new = jnp.maximum(m_prev, jnp.max(mx_scr[h % 2], axis=-1, keepdims=True))
        a_scr[h % 2] = jnp.exp2(m_prev - m_new)
        m_ref[h] = m_new

    def pass_b(h):
        s_ref = s_scr.at[h % 3]
        for rows, _ in chunks:
            m_rows = m_ref[h, rows, :]
            for kc in range(nkc):
                cols = slice(kc * LANES, (kc + 1) * LANES)
                p_scr[h % 2, rows, cols] = jnp.exp2(s_ref[rows, cols] - m_rows).astype(bf16)
        pv = jnp.dot(p_scr[h % 2], value_fn(h), preferred_element_type=f32)
        alpha = a_scr[h % 2]
        acc_ref[h] = alpha * acc_ref[h] + pv[:, :HEAD_DIM]
        l_ref[h] = alpha * l_ref[h] + pv[:, HEAD_DIM:]

    scores(0)
    for h in range(nheads):
        if h + 1 < nheads:
            scores(h + 1)
        pass_a(h)
        if h >= 1:
            pass_b(h - 1)
    pass_b(nheads - 1)


def _with_ones(v):
    return jnp.concatenate([v, jnp.ones((v.shape[0], LANES), v.dtype)], axis=1)


def _mla_attn_body(qi_tab, ki_tab, fl_tab, q_ref, k_ref, v_ref, o_ref, m_ref, l_ref, acc_ref,
                   s_scr, p_scr, mx_scr, a_scr, mask_scr, *, hb, tq, tk):
    st = pl.program_id(2)
    qi, ki, fl = qi_tab[st], ki_tab[st], fl_tab[st]

    pl.when((fl & FLAG_FIRST) != 0)(lambda: _flash_init(m_ref, l_ref, acc_ref))

    def score(h):
        qk = slice(h * MLA_SLOT, (h + 1) * MLA_SLOT)
        return lax.dot_general(q_ref[:, qk], k_ref[:, qk], NT, preferred_element_type=f32)

    def value(h):
        return _with_ones(v_ref[:, h * HEAD_DIM:(h + 1) * HEAD_DIM])

    def tile(causal):
        add = None
        if causal:
            row = qi * tq + lax.broadcasted_iota(i32, (tq, tk), 0)
            col = ki * tk + lax.broadcasted_iota(i32, (tq, tk), 1)
            mask_scr[...] = jnp.where(row >= col, 0.0, NEG_INF)
            add = lambda h, rows, r0, kc: (mask_scr[rows, kc * LANES:(kc + 1) * LANES],)
        _flash_heads(hb, tq, tk, score, add, value, m_ref, l_ref, acc_ref, s_scr, p_scr, mx_scr, a_scr)

    pl.when((fl & FLAG_CAUSAL) != 0)(lambda: tile(True))
    pl.when((fl & FLAG_CAUSAL) == 0)(lambda: tile(False))

    @pl.when((fl & FLAG_LAST) != 0)
    def _():
        for h in range(hb):
            o_ref[:, h * HEAD_DIM:(h + 1) * HEAD_DIM] = (acc_ref[h] / l_ref[h]).astype(o_ref.dtype)


def _mla_attn(q, k, v, B, T, hb=MLA_HEADS, tq=Q_TILE, tk=K_TILE):
    tk = min(tk, T)
    nq, nk = T // tq, T // tk
    tabs = _tile_schedule(nq, tq, tk, lambda qi: 0)
    grid_spec = pltpu.PrefetchScalarGridSpec(
        num_scalar_prefetch=3,
        grid=(B, MLA_HEADS // hb, int(tabs[0].shape[0])),
        in_specs=[pl.BlockSpec((tq, hb * MLA_SLOT), lambda b, h, s, qt, kt, ft: (b * nq + qt[s], h)),
                  pl.BlockSpec((tk, hb * MLA_SLOT), lambda b, h, s, qt, kt, ft: (b * nk + kt[s], h)),
                  pl.BlockSpec((tk, hb * HEAD_DIM), lambda b, h, s, qt, kt, ft: (b * nk + kt[s], h))],
        out_specs=pl.BlockSpec((tq, hb * HEAD_DIM), lambda b, h, s, qt, kt, ft: (b * nq + qt[s], h)),
        scratch_shapes=_flash_scratch(hb, tq, tk))
    return pl.pallas_call(
        functools.partial(_mla_attn_body, hb=hb, tq=tq, tk=tk),
        grid_spec=grid_spec,
        out_shape=jax.ShapeDtypeStruct((B * T, MLA_HEADS * HEAD_DIM), bf16),
        compiler_params=_params(("parallel", "parallel", "arbitrary")),
        name="mla_flash_attention",
    )(*tabs, q, k, v)


def _bias_table_body(rev_ref, o_ref):
    width = rev_ref.shape[-1]
    x = jnp.broadcast_to(rev_ref[0], (LANES, width))
    rolled = pltpu.roll(x, 0, 1, stride=1, stride_axis=0)
    for kb in range(width // LANES):
        o_ref[0, kb] = rolled[:, kb * LANES:(kb + 1) * LANES]


def _bias_table(rev):
    H, _, U = rev.shape
    return pl.pallas_call(
        _bias_table_body,
        grid=(H,),
        in_specs=[pl.BlockSpec((1, 1, U), lambda h: (h, 0, 0))],
        out_specs=pl.BlockSpec((1, U // LANES, LANES, LANES), lambda h: (h, 0, 0, 0)),
        out_shape=jax.ShapeDtypeStruct((H, U // LANES, LANES, LANES), f32),
        compiler_params=_params(("parallel",)),
        name="relative_bias_toeplitz_table",
    )(rev)


def _nsa_band_body(qi_tab, ki_tab, fl_tab, q_ref, k_ref, v_ref, g_ref, *rest, tq, tk, seq, select, kb_lo):
    if select:
        sel_ref, e_ref, o_ref, m_ref, l_ref, acc_ref, s_scr, p_scr, mx_scr, a_scr, mask_scr = rest
    else:
        oc_ref, os_ref, gate_ref, o_ref, m_ref, l_ref, acc_ref, s_scr, p_scr, mx_scr, a_scr, mask_scr = rest
    st = pl.program_id(2)
    qi, ki, fl = qi_tab[st], ki_tab[st], fl_tab[st]

    pl.when((fl & FLAG_FIRST) != 0)(lambda: _flash_init(m_ref, l_ref, acc_ref))

    d0 = qi * tq - ki * tk
    dist = d0 + lax.broadcasted_iota(i32, (tq, tk), 0) - lax.broadcasted_iota(i32, (tq, tk), 1)
    if select:
        picked = jnp.dot(sel_ref[...], e_ref[...], preferred_element_type=f32)
        valid = (dist >= 0) & (picked > 0.5)
    else:
        valid = (dist >= 0) & (dist < WINDOW)
    mask_scr[...] = jnp.where(valid, 0.0, NEG_INF)
    ng = k_ref.shape[0]
    ks = [k_ref[gi].astype(bf16) for gi in range(ng)]
    vs = [_with_ones(v_ref[gi].astype(bf16)) for gi in range(ng)]
    kb0 = (seq - d0) // LANES - kb_lo

    def score(h):
        return lax.dot_general(q_ref[:, h * HEAD_DIM:(h + 1) * HEAD_DIM], ks[h // NSA_HPG], NT,
                               preferred_element_type=f32)

    def add(h, rows, r0, kc):
        bias = g_ref[h, kb0 - r0 // LANES + kc, pl.ds(r0 % LANES, ROW_CHUNK), :]
        return bias, mask_scr[rows, kc * LANES:(kc + 1) * LANES]

    _flash_heads(ng * NSA_HPG, tq, tk, score, add, lambda h: vs[h // NSA_HPG], m_ref, l_ref, acc_ref,
                 s_scr, p_scr, mx_scr, a_scr)

    @pl.when((fl & FLAG_LAST) != 0)
    def _():
        for h in range(ng * NSA_HPG):
            sl = slice(h * HEAD_DIM, (h + 1) * HEAD_DIM)
            o = acc_ref[h] / l_ref[h]
            if not select:
                gate = gate_ref[...]
                g0 = (h // NSA_HPG) * LANES + h % NSA_HPG
                o = (gate[:, g0:g0 + 1] * oc_ref[:, sl].astype(f32)
                     + gate[:, g0 + NSA_HPG:g0 + NSA_HPG + 1] * os_ref[:, sl].astype(f32)
                     + gate[:, g0 + 2 * NSA_HPG:g0 + 2 * NSA_HPG + 1] * o)
            o_ref[:, sl] = o.astype(o_ref.dtype)


def _nsa_band(q, kv, gtab, B, T, kcol, vcol, tk, sel=None, emat=None, merge=None, tq=Q_TILE):
    select = sel is not None
    nq, nk = T // tq, T // tk
    first = (lambda qi: 0) if select else (lambda qi: max(qi * tq - (WINDOW - 1), 0) // tk)
    tabs = _tile_schedule(nq, tq, tk, first)
    if select:
        ng, kb_lo = 1, 0
    else:
        ng = NSA_GROUPS
        kb_lo = (T - (WINDOW + tk - 1) // tk * tk) // LANES - tq // LANES
        gtab = gtab[:, kb_lo:T // LANES + tk // LANES]
    gw = ng * NSA_HPG * HEAD_DIM
    in_specs = [
        pl.BlockSpec((tq, gw), lambda b, g, s, qt, kt, ft: (b * nq + qt[s], g)),
        pl.BlockSpec((ng, tk, HEAD_DIM), lambda b, g, s, qt, kt, ft: ((kcol * NSA_GROUPS) // ng + g, b * nk + kt[s], 0)),
        pl.BlockSpec((ng, tk, HEAD_DIM), lambda b, g, s, qt, kt, ft: ((vcol * NSA_GROUPS) // ng + g, b * nk + kt[s], 0)),
        pl.BlockSpec((ng * NSA_HPG,) + gtab.shape[1:], lambda b, g, s, qt, kt, ft: (g, 0, 0, 0),
                     pipeline_mode=pl.Buffered(1)),
    ]
    args = [q, kv, kv, gtab]
    if select:
        in_specs += [pl.BlockSpec((tq, LANES), lambda b, g, s, qt, kt, ft: (b * nq + qt[s], g)),
                     pl.BlockSpec((LANES, tk), lambda b, g, s, qt, kt, ft: (0, kt[s]))]
        args += [sel, emat]
    else:
        o_other = pl.BlockSpec((tq, gw), lambda b, g, s, qt, kt, ft: (b * nq + qt[s], g))
        in_specs += [o_other, o_other,
                     pl.BlockSpec((tq, ng * LANES), lambda b, g, s, qt, kt, ft: (b * nq + qt[s], g))]
        args += list(merge)
    grid_spec = pltpu.PrefetchScalarGridSpec(
        num_scalar_prefetch=3,
        grid=(B, NSA_GROUPS // ng, int(tabs[0].shape[0])),
        in_specs=in_specs,
        out_specs=pl.BlockSpec((tq, gw), lambda b, g, s, qt, kt, ft: (b * nq + qt[s], g)),
        scratch_shapes=_flash_scratch(ng * NSA_HPG, tq, tk))
    return pl.pallas_call(
        functools.partial(_nsa_band_body, tq=tq, tk=tk, seq=T, select=select, kb_lo=kb_lo),
        grid_spec=grid_spec,
        out_shape=jax.ShapeDtypeStruct((B * T, NSA_HEADS * HEAD_DIM), bf16),
        compiler_params=_params(("parallel", "parallel", "arbitrary")),
        name="nsa_selected_attention" if select else "nsa_window_attention",
    )(*tabs, *args)


def _compress_body(x_ref, pe_ref, w1_ref, w2_ref, o_ref, ot_ref, *, nc):
    half = CMP_BLOCK // 2
    h_lo = jnp.zeros((nc, CMP_HIDDEN), f32)
    h_hi = jnp.zeros((nc, CMP_HIDDEN), f32)
    for p in range(half):
        rows = x_ref[0, pl.ds(p, nc, stride=CMP_STRIDE), :]
        h_lo += jnp.dot((rows + pe_ref[0, p:p + 1, :]).astype(bf16), w1_ref[0, p], preferred_element_type=f32)
        h_hi += jnp.dot((rows + pe_ref[0, half + p:half + p + 1, :]).astype(bf16), w1_ref[0, half + p],
                        preferred_element_type=f32)
    hid = h_lo + pltpu.roll(h_hi, nc - 1, 0)
    out = jnp.dot(jax.nn.gelu(hid).astype(bf16), w2_ref[0], preferred_element_type=f32)
    out = jnp.where(lax.broadcasted_iota(i32, out.shape, 0) < nc - 1, out, 0.0)
    o_ref[0, 0, 0] = out.astype(bf16)
    ot_ref[0, 0, 0] = out.T.astype(bf16)


def _compress(kv, pe, w1, w2, B, T):
    nc = T // CMP_STRIDE
    return pl.pallas_call(
        functools.partial(_compress_body, nc=nc),
        grid=(2, B, NSA_GROUPS),
        in_specs=[pl.BlockSpec((1, T, HEAD_DIM), lambda c, b, g: (c * NSA_GROUPS + g, b, 0)),
                  pl.BlockSpec((1, CMP_BLOCK, HEAD_DIM), lambda c, b, g: (c, 0, 0)),
                  pl.BlockSpec((1, CMP_BLOCK, HEAD_DIM, CMP_HIDDEN), lambda c, b, g: (c, 0, 0, 0)),
                  pl.BlockSpec((1, CMP_HIDDEN, HEAD_DIM), lambda c, b, g: (c, 0, 0))],
        out_specs=[pl.BlockSpec((1, 1, 1, nc, HEAD_DIM), lambda c, b, g: (c, b, g, 0, 0)),
                   pl.BlockSpec((1, 1, 1, HEAD_DIM, nc), lambda c, b, g: (c, b, g, 0, 0))],
        out_shape=[jax.ShapeDtypeStruct((2, B, NSA_GROUPS, nc, HEAD_DIM), bf16),
                   jax.ShapeDtypeStruct((2, B, NSA_GROUPS, HEAD_DIM, nc), bf16)],
        compiler_params=_params(("parallel", "parallel", "parallel")),
        name="nsa_compress_mlp",
    )(kv, pe, w1, w2)


def _cmp_body(q_ref, kc_ref, vct_ref, cw_ref, ovt_ref, o_ref, sel_ref, *, nc, nq):
    tq = CMP_TQ
    qi = pl.program_id(2)
    t0 = qi * tq
    key_row = lax.broadcasted_iota(i32, (nc, tq), 0)
    q_col = lax.broadcasted_iota(i32, (nc, tq), 1)
    valid = (t0 + q_col - CMP_STRIDE * key_row - (CMP_BLOCK - 1)) >= 0
    kc = kc_ref[0, 0, 0]
    vct = vct_ref[0, 0, 0]
    psum = jnp.zeros((nc, tq), f32)
    for h in range(NSA_HPG):
        bias = jnp.concatenate(
            [jnp.concatenate([cw_ref[h, qi * (tq // LANES) + b - c + nq] for b in range(tq // LANES)], axis=1)
             for c in range(nc // 8)], axis=0)
        s = lax.dot_general(kc, q_ref[:, h * HEAD_DIM:(h + 1) * HEAD_DIM], NT, preferred_element_type=f32)
        s = jnp.where(valid, s + bias, NEG_INF)
        m = jnp.max(s, axis=0, keepdims=True)
        e = jnp.where(valid, jnp.exp2(s - m), 0.0)
        l = jnp.sum(e, axis=0, keepdims=True)
        p = e / jnp.where(l > 0.0, l, 1.0)
        psum += p
        o_t = jnp.dot(vct, p.astype(bf16), preferred_element_type=f32)
        o_ref[:, h * HEAD_DIM:(h + 1) * HEAD_DIM] = o_t.T.astype(o_ref.dtype)

    imp = jnp.dot(ovt_ref[...], psum, preferred_element_type=f32, precision=lax.Precision.HIGHEST)
    nrow = imp.shape[0]
    n_slc = (nc * CMP_STRIDE) // SLC_BLOCK
    j = lax.broadcasted_iota(i32, (nrow, tq), 0)
    t = t0 + lax.broadcasted_iota(i32, (nrow, tq), 1)
    cur = jnp.right_shift(t, 6)
    ok = j * SLC_BLOCK <= t
    forced = (j == 0) | (j == cur) | (j == cur - 1)
    score = jnp.where(forced, FORCED_SCORE, jnp.where(ok, imp, NEG_INF))
    score = jnp.where(j < n_slc, score, PAD_SCORE)
    chosen = jnp.zeros((nrow, tq), f32)
    row_id = j.astype(f32)
    for _ in range(min(SLC_TOPK, n_slc)):
        top = jnp.max(score, axis=0, keepdims=True)
        first = jnp.min(jnp.where(score == top, row_id, PAD_ID), axis=0, keepdims=True)
        hit = row_id == first
        chosen = jnp.where(hit, 1.0, chosen)
        score = jnp.where(hit, PAD_SCORE, score)
    chosen = jnp.where(ok & (j < n_slc), chosen, 0.0)
    sel_ref[...] = chosen.T.astype(sel_ref.dtype)


def _nsa_cmp(q, kc, vct, cw, ovt, B, T):
    nc = T // CMP_STRIDE
    nq = T // CMP_TQ
    gw = NSA_HPG * HEAD_DIM
    return pl.pallas_call(
        functools.partial(_cmp_body, nc=nc, nq=T // LANES),
        grid=(B, NSA_GROUPS, nq),
        in_specs=[pl.BlockSpec((CMP_TQ, gw), lambda b, g, qi: (b * nq + qi, g)),
                  pl.BlockSpec((1, 1, 1, nc, HEAD_DIM), lambda b, g, qi: (0, b, g, 0, 0)),
                  pl.BlockSpec((1, 1, 1, HEAD_DIM, nc), lambda b, g, qi: (1, b, g, 0, 0)),
                  pl.BlockSpec((NSA_HPG,) + cw.shape[1:], lambda b, g, qi: (g, 0, 0, 0)),
                  pl.BlockSpec(ovt.shape, lambda b, g, qi: (0, 0))],
        out_specs=[pl.BlockSpec((CMP_TQ, gw), lambda b, g, qi: (b * nq + qi, g)),
                   pl.BlockSpec((CMP_TQ, LANES), lambda b, g, qi: (b * nq + qi, g))],
        out_shape=[jax.ShapeDtypeStruct((B * T, NSA_HEADS * HEAD_DIM), bf16),
                   jax.ShapeDtypeStruct((B * T, NSA_GROUPS * LANES), bf16)],
        compiler_params=_params(("parallel", "parallel", "parallel")),
        name="nsa_compressed_attention_select",
    )(q, kc, vct, cw, ovt)


def _layer_norm(y, g, b):
    mu = jnp.mean(y, axis=-1, keepdims=True)
    var = jnp.mean(jnp.square(y - mu), axis=-1, keepdims=True)
    return (y - mu) * lax.rsqrt(var + LN_EPS) * g + b


def _branch_merge_body(a_ref, b_ref, wa_ref, wb_ref, ga_ref, gb_ref, o_ref):
    ya = jnp.dot(a_ref[...], wa_ref[...], preferred_element_type=f32)
    yb = jnp.dot(b_ref[...], wb_ref[...], preferred_element_type=f32)
    o_ref[...] = (ga_ref[...] * ya + gb_ref[...] * yb).astype(o_ref.dtype)


def _branch_merge(a, b, wa, wb, mg, tm, tn):
    M, K = a.shape
    Nc = wa.shape[1]
    nj = Nc // tn
    return pl.pallas_call(
        _branch_merge_body,
        grid=(M // tm, nj),
        in_specs=[pl.BlockSpec((tm, K), lambda i, j: (i, 0)), pl.BlockSpec((tm, K), lambda i, j: (i, 0)),
                  pl.BlockSpec((K, tn), lambda i, j: (0, j)), pl.BlockSpec((K, tn), lambda i, j: (0, j)),
                  pl.BlockSpec((tm, tn), lambda i, j: (i, j)), pl.BlockSpec((tm, tn), lambda i, j: (i, nj + j))],
        out_specs=pl.BlockSpec((tm, tn), lambda i, j: (i, j)),
        out_shape=jax.ShapeDtypeStruct((M, Nc), bf16),
        compiler_params=_params(("parallel", "parallel")),
        name="branch_proj_merge",
    )(a, b, wa, wb, mg, mg)


def _proj_ln_body(z_ref, w_ref, x_ref, g_ref, b_ref, o_ref):
    y = jnp.dot(z_ref[...], w_ref[...], preferred_element_type=f32)
    o_ref[...] = _layer_norm(ALPHA * x_ref[...] + y, g_ref[...], b_ref[...])


def _proj_ln(z, w, x, g, b, tm):
    M, K = z.shape
    D = w.shape[1]
    vec = pl.BlockSpec((1, D), lambda i: (0, 0))
    return pl.pallas_call(
        _proj_ln_body,
        grid=(M // tm,),
        in_specs=[pl.BlockSpec((tm, K), lambda i: (i, 0)), pl.BlockSpec((K, D), lambda i: (0, 0)),
                  pl.BlockSpec((tm, D), lambda i: (i, 0)), vec, vec],
        out_specs=pl.BlockSpec((tm, D), lambda i: (i, 0)),
        out_shape=jax.ShapeDtypeStruct((M, D), f32),
        compiler_params=_params(("parallel",)),
        name="mix_out_proj_layernorm",
    )(z, w, x, g, b)


def _mem_attn_body(x_ref, wq_ref, k_ref, v_ref, wo_ref, g_ref, b_ref, o_ref, ob_ref):
    x = x_ref[...]
    q = (jnp.dot(x.astype(bf16), wq_ref[...], preferred_element_type=f32) * HEAD_DIM ** -0.5).astype(bf16)
    heads = []
    for h in range(MEM_HEADS):
        sl = slice(h * HEAD_DIM, (h + 1) * HEAD_DIM)
        s = lax.dot_general(q[:, sl], k_ref[0, :, sl], NT, preferred_element_type=f32)
        e = jnp.exp(s - jnp.max(s, axis=-1, keepdims=True))
        p = e / jnp.sum(e, axis=-1, keepdims=True)
        heads.append(jnp.dot(p.astype(bf16), v_ref[0, :, sl], preferred_element_type=f32).astype(bf16))
    y = jnp.dot(jnp.concatenate(heads, axis=1), wo_ref[...], preferred_element_type=f32)
    out = _layer_norm(ALPHA * x + y, g_ref[...], b_ref[...])
    o_ref[...] = out
    ob_ref[...] = out.astype(bf16)


def _mem_attn(x, wq, km, vm, wo, g, b, T, tm):
    M, D = x.shape
    nt = T // tm
    W = wq.shape[1]
    vec = pl.BlockSpec((1, D), lambda i: (0, 0))
    blk = pl.BlockSpec((tm, D), lambda i: (i, 0))
    return pl.pallas_call(
        _mem_attn_body,
        grid=(M // tm,),
        in_specs=[blk, pl.BlockSpec((D, W), lambda i: (0, 0)),
                  pl.BlockSpec((1, N_MEM, W), lambda i: (i // nt, 0, 0)),
                  pl.BlockSpec((1, N_MEM, W), lambda i: (i // nt, 0, 0)),
                  pl.BlockSpec((W, D), lambda i: (0, 0)), vec, vec],
        out_specs=[blk, blk],
        out_shape=[jax.ShapeDtypeStruct((M, D), f32), jax.ShapeDtypeStruct((M, D), bf16)],
        compiler_params=_params(("parallel",)),
        name="memory_cross_attention_layernorm",
    )(x, wq, km, vm, wo, g, b)


def _add_ln_body(x_ref, y_ref, g_ref, b_ref, o_ref):
    o_ref[...] = _layer_norm(ALPHA * x_ref[...] + y_ref[...], g_ref[...], b_ref[...])


def _add_ln(x, y, g, b, tm):
    M, D = x.shape
    vec = pl.BlockSpec((1, D), lambda i: (0, 0))
    blk = pl.BlockSpec((tm, D), lambda i: (i, 0))
    return pl.pallas_call(
        _add_ln_body,
        grid=(M // tm,),
        in_specs=[blk, blk, vec, vec],
        out_specs=blk,
        out_shape=jax.ShapeDtypeStruct((M, D), f32),
        compiler_params=_params(("parallel",)),
        name="residual_layernorm",
    )(x, y, g, b)


def _topk_rows(s, ids, k):
    n = s.shape[1]
    slot = lax.broadcasted_iota(i32, (k, n), 0)
    vals = jnp.zeros((k, n), f32)
    inds = jnp.zeros((k, n), f32)
    for r in range(k):
        top = jnp.max(s, axis=0, keepdims=True)
        first = jnp.min(jnp.where(s == top, ids, PAD_ID), axis=0, keepdims=True)
        vals = jnp.where(slot == r, top, vals)
        inds = jnp.where(slot == r, first, inds)
        s = jnp.where(ids == first, PAD_SCORE, s)
    return vals, inds


def _pick_rows(table, sel):
    out = jnp.zeros(sel.shape, table.dtype)
    for a in range(table.shape[0]):
        out = jnp.where(sel == a, table[a:a + 1, :], out)
    return out


CAND_COUNTS = tuple(PEER_TOPK // (a + 1) for a in range(PEER_TOPK))
CAND_ROWS = -(-sum(CAND_COUNTS) // 8) * 8


def _cand_ids(tn):
    ids = [a * PEER_TOPK + b for a, nb in enumerate(CAND_COUNTS) for b in range(nb)]
    ids = np.array(ids + [PAD_ID] * (CAND_ROWS - len(ids)), np.float32)
    return jnp.asarray(np.broadcast_to(ids[:, None], (CAND_ROWS, tn)))


def _peer_route_body(q_ref, keys_ref, ids_ref, e1_ref, e2_ref, gt_ref, e1_scr, e2_scr, g_scr, cand_scr):
    k1 = keys_ref[0].astype(bf16)
    k2 = keys_ref[1].astype(bf16)
    tn = q_ref.shape[0]
    key_ids = lax.broadcasted_iota(i32, (PEER_N_KEYS, tn), 0).astype(f32)
    cand_scr[...] = jnp.full(cand_scr.shape, PAD_SCORE, f32)
    for h in range(PEER_HEADS):
        q1 = q_ref[:, (2 * h) * LANES:(2 * h + 1) * LANES]
        q2 = q_ref[:, (2 * h + 1) * LANES:(2 * h + 2) * LANES]
        s1 = lax.dot_general(k1, q1, NT, preferred_element_type=f32)
        s2 = lax.dot_general(k2, q2, NT, preferred_element_type=f32)
        v1, i1 = _topk_rows(s1, key_ids, PEER_TOPK)
        v2, i2 = _topk_rows(s2, key_ids, PEER_TOPK)
        off = 0
        for a, nb in enumerate(CAND_COUNTS):
            cand_scr[off:off + nb, :] = v1[a:a + 1, :] + v2[0:nb, :]
            off += nb
        top_s, flat = _topk_rows(cand_scr[...], ids_ref[...], PEER_TOPK)
        flat = flat.astype(i32)
        e = jnp.exp(top_s - top_s[0:1, :])
        rows = slice(h * PEER_TOPK, (h + 1) * PEER_TOPK)
        g_scr[rows, :] = e / jnp.sum(e, axis=0, keepdims=True)
        e1_scr[rows, :] = _pick_rows(i1, jnp.right_shift(flat, 4)).astype(i32)
        e2_scr[rows, :] = _pick_rows(i2, jnp.bitwise_and(flat, PEER_TOPK - 1)).astype(i32)
    e1_ref[...] = e1_scr[...].T
    e2_ref[...] = e2_scr[...].T
    gt_ref[...] = g_scr[...].T


def _peer_route(q, keys, tn=LANES):
    M, W = q.shape
    out = pl.BlockSpec((tn, PEER_HK), lambda i: (i, 0))
    return pl.pallas_call(
        _peer_route_body,
        grid=(M // tn,),
        in_specs=[pl.BlockSpec((tn, W), lambda i: (i, 0)), pl.BlockSpec(keys.shape, lambda i: (0, 0, 0)),
                  pl.BlockSpec((CAND_ROWS, tn), lambda i: (0, 0))],
        out_specs=[out, out, out],
        out_shape=[jax.ShapeDtypeStruct((M, PEER_HK), i32), jax.ShapeDtypeStruct((M, PEER_HK), i32),
                   jax.ShapeDtypeStruct((M, PEER_HK), f32)],
        scratch_shapes=[pltpu.VMEM((PEER_HK, tn), i32), pltpu.VMEM((PEER_HK, tn), i32), pltpu.VMEM((PEER_HK, tn), f32),
                        pltpu.VMEM((CAND_ROWS, tn), f32)],
        compiler_params=_params(("parallel",)),
        name="peer_product_key_routing",
    )(q, keys, _cand_ids(tn))


TOKEN_GROUP = 16


def _peer_weights_body(e1_ref, e2_ref, g_ref, w_ref, *, tn):
    sub = lax.broadcasted_iota(i32, (PEER_N_KEYS, PEER_HK), 0)

    def one(n):
        r1 = e1_ref[pl.ds(n, 1), :]
        r2 = e2_ref[pl.ds(n, 1), :]
        g = g_ref[pl.ds(n, 1), :]
        a_t = (sub == r1).astype(bf16)
        b_t = jnp.where(sub == r2, g, 0.0).astype(bf16)
        return lax.dot_general(a_t, b_t, NT, preferred_element_type=f32)

    def group(i, carry):
        n0 = pl.multiple_of(i * TOKEN_GROUP, TOKEN_GROUP)
        tables = jnp.stack([one(n0 + t) for t in range(TOKEN_GROUP)], axis=0)
        w_ref[:, pl.ds(n0, TOKEN_GROUP), :] = jnp.swapaxes(tables, 0, 1).astype(w_ref.dtype)
        return carry

    lax.fori_loop(0, tn // TOKEN_GROUP, group, 0)


def _peer_weights(e1, e2, g, tn=256):
    M = e1.shape[0]
    blk = pl.BlockSpec((tn, PEER_HK), lambda i: (i, 0))
    return pl.pallas_call(
        functools.partial(_peer_weights_body, tn=tn),
        grid=(M // tn,),
        in_specs=[blk, blk, blk],
        out_specs=pl.BlockSpec((PEER_N_KEYS, tn, PEER_N_KEYS), lambda i: (0, i, 0)),
        out_shape=jax.ShapeDtypeStruct((PEER_N_KEYS, M, PEER_N_KEYS), bf16),
        compiler_params=_params(("parallel",)),
        name="peer_routing_weights",
    )(e1, e2, g)


GELU_SLOPE = 2.0 * math.sqrt(2.0 / math.pi)


def _gelu_tanh(x):
    minus_2z = x * ((-GELU_SLOPE * 0.044715 * LOG2E) * (x * x) - GELU_SLOPE * LOG2E)
    return x / (1.0 + jnp.exp2(minus_2z))


def _peer_dense_body(x_ref, w_ref, u_ref, v_ref, o_ref, *, rows):
    @pl.when(pl.program_id(1) == 0)
    def _():
        o_ref[...] = jnp.zeros(o_ref.shape, f32)

    act = _gelu_tanh(lax.dot_general(x_ref[...], u_ref[...].astype(bf16), NT, preferred_element_type=f32))
    w = jnp.concatenate([w_ref[r] for r in range(rows)], axis=1)
    o_ref[...] += jnp.dot(w * act.astype(bf16), v_ref[...].astype(bf16), preferred_element_type=f32)


def _peer_dense(xb, w, u, v, tm=1024, rows=4):
    M, D = xb.shape
    te = rows * PEER_N_KEYS
    return pl.pallas_call(
        functools.partial(_peer_dense_body, rows=rows),
        grid=(M // tm, PEER_N_KEYS // rows),
        in_specs=[pl.BlockSpec((tm, D), lambda i, e: (i, 0)),
                  pl.BlockSpec((rows, tm, PEER_N_KEYS), lambda i, e: (e, i, 0)),
                  pl.BlockSpec((te, D), lambda i, e: (e, 0)), pl.BlockSpec((te, D), lambda i, e: (e, 0))],
        out_specs=pl.BlockSpec((tm, D), lambda i, e: (i, 0)),
        out_shape=jax.ShapeDtypeStruct((M, D), f32),
        compiler_params=_params(("parallel", "arbitrary")),
        name="peer_dense_experts",
    )(xb, w, u, v)


def _t5_bucket(dist):
    dist = jnp.maximum(dist, 0)
    n_log = REL_BUCKETS - REL_MAX_EXACT
    large = REL_MAX_EXACT + (jnp.log(jnp.maximum(dist, 1).astype(f32) / REL_MAX_EXACT)
                             / math.log(REL_MAX_DIST / REL_MAX_EXACT) * n_log).astype(i32)
    large = jnp.minimum(large, REL_BUCKETS - 1)
    return jnp.where(dist < REL_MAX_EXACT, dist, large)


def _position_tables(rel_bias, T):
    by_dist = (rel_bias[_t5_bucket(jnp.arange(T))] * LOG2E).T.astype(f32)
    rev = jnp.pad(by_dist[:, ::-1], ((0, 0), (1, K_TILE - 1)), mode="edge")[:, None, :]
    shift = (CMP_BLOCK - 1) + CMP_STRIDE * 7
    ext = jnp.pad(by_dist, ((0, 0), (T + shift, 0)), mode="edge")
    rows = [ext[:, CMP_STRIDE * (7 - r):CMP_STRIDE * (7 - r) + 2 * T] for r in range(8)]
    cmpw = jnp.stack(rows, axis=1).reshape(NSA_HEADS, 8, 2 * T // LANES, LANES)
    return rev, jnp.transpose(cmpw, (0, 2, 1, 3))


def _rope_slot_tables(T):
    inv = ROPE_THETA ** (-jnp.arange(0, MLA_ROPE, 2, dtype=f32) / MLA_ROPE)
    ang = jnp.arange(T, dtype=f32)[:, None] * inv[None, :]
    cos, sin = jnp.cos(ang), jnp.sin(ang)
    z = jnp.zeros_like(cos)
    ta = jnp.concatenate([cos, cos, z, z], axis=1)
    tb = jnp.concatenate([-sin, z, z, z], axis=1)
    tc = jnp.concatenate([z, sin, z, z], axis=1)
    return ta, tb, tc


def _layer(x, mem, rel_bias, w_in, cmp_k_pe, cmp_k_w1, cmp_k_w2, cmp_v_pe, cmp_v_w1, cmp_v_w2,
           mla_kv_norm, mla_w_uk, mla_w_uv, w_branch_nsa, w_branch_mla, w_mix_out, ln_mix_g, ln_mix_b,
           mem_w_q, mem_w_k, mem_w_v, mem_w_o, ln_mem_g, ln_mem_b,
           peer_w_query, peer_sub_keys, peer_u, peer_v, ln_ffn_g, ln_ffn_b, ropes):
    B, T, D = x.shape
    N = B * T
    x2d = x.reshape(N, D)
    xb = x2d.astype(bf16)
    row = lambda a: a.reshape(1, -1)

    gate_cols = np.array([[br * NSA_HEADS + g * NSA_HPG + h for br in range(3) for h in range(NSA_HPG)]
                          for g in range(NSA_GROUPS)])
    w_gate = jnp.pad(w_in[:, OFF_NSA_GATE:OFF_MLA_Q][:, gate_cols],
                     ((0, 0), (0, 0), (0, LANES - 3 * NSA_HPG))).reshape(D, NSA_GROUPS * LANES)
    w_mla_q = jnp.pad(w_in[:, OFF_MLA_Q:OFF_MLA_KV].reshape(D, MLA_HEADS, MLA_NOPE + MLA_ROPE),
                      ((0, 0), (0, 0), (0, MLA_SLOT - MLA_NOPE - MLA_ROPE))).reshape(D, MLA_HEADS * MLA_SLOT)
    w_mla_kv = jnp.pad(w_in[:, OFF_MLA_KV:OFF_MERGE], ((0, 0), (0, LANES - MLA_ROPE)))
    segs = [w_mla_kv, jnp.zeros((D, LANES), f32), w_gate,
            w_in[:, :OFF_NSA_KV], w_mla_q, w_in[:, OFF_MERGE:], w_in[:, OFF_NSA_KV:OFF_NSA_GATE]]
    starts = np.concatenate([[0], np.cumsum([s.shape[1] for s in segs])])
    col = lambda i: (int(starts[i]), int(segs[i].shape[1]))
    w_all = jnp.concatenate(segs, axis=1).astype(bf16)

    mla_ckv = _mm(xb, w_all, f32, 1024, MLA_KV_RANK + LANES, name="mla_kv_proj", cols=col(0))
    gates = _mm(xb, w_all, f32, 1024, NSA_GROUPS * LANES, act="sigmoid", name="nsa_gate_proj", cols=col(2))
    nsa_q = _mm(xb, w_all, bf16, 1024, 1024, scale=HEAD_DIM ** -0.5 * LOG2E, name="nsa_q_proj", cols=col(3))
    mla_q = _mla_q(xb, w_all, col(4), ropes, T, 1024, 1024)
    merge = _mm(xb, w_all, f32, 1024, 1024, act="sigmoid", name="merge_gate_proj", cols=col(5))
    nsa_kv = _mm_slabs(xb, w_all, col(6), 1024, 512, "nsa_kv_proj")

    slc_tk = min(K_TILE, T)
    rev_tab, cmp_tab = _position_tables(rel_bias, T)
    bias_tab = _bias_table(rev_tab)
    pe = jnp.stack([cmp_k_pe, cmp_v_pe])
    w1 = jnp.stack([cmp_k_w1, cmp_v_w1]).reshape(2, CMP_BLOCK, HEAD_DIM, CMP_HIDDEN).astype(bf16)
    w2 = jnp.stack([cmp_k_w2, cmp_v_w2]).astype(bf16)
    kvc, kvc_t = _compress(nsa_kv, pe, w1, w2, B, T)
    nc = T // CMP_STRIDE
    n_slc = T // SLC_BLOCK
    cs = np.arange(nc)[None, :] * CMP_STRIDE
    ss = np.arange(LANES)[:, None] * SLC_BLOCK
    ovt = ((cs < ss + SLC_BLOCK) & (cs + CMP_BLOCK > ss) & (np.arange(nc)[None, :] < nc - 1)
           & (np.arange(LANES)[:, None] < n_slc)).astype(np.float32)
    o_cmp, sel = _nsa_cmp(nsa_q, kvc, kvc_t, cmp_tab, jnp.asarray(ovt), B, T)
    emat = (np.arange(T)[None, :] // SLC_BLOCK == np.arange(LANES)[:, None]).astype(np.float32)
    o_slc = _nsa_band(nsa_q, nsa_kv, bias_tab, B, T, 2, 3, slc_tk, sel=sel, emat=jnp.asarray(emat, dtype=bf16))
    o_nsa = _nsa_band(nsa_q, nsa_kv, bias_tab, B, T, 4, 5, WIN_TILE, merge=(o_cmp, o_slc, gates), tq=WIN_TILE)

    mla_k, mla_v = _mla_kv(mla_ckv, row(mla_kv_norm),
                           jnp.asarray(mla_w_uk, bf16), jnp.asarray(mla_w_uv, bf16), ropes, T, 512)
    o_mla = _mla_attn(mla_q, mla_k, mla_v, B, T)

    z = _branch_merge(o_nsa, o_mla, w_branch_nsa.astype(bf16), w_branch_mla.astype(bf16), merge, 1024, 512)
    x1 = _proj_ln(z, w_mix_out.astype(bf16), x2d, row(ln_mix_g), row(ln_mix_b), 512)

    mem2d = mem.reshape(B * N_MEM, D).astype(bf16)
    km = _mm(mem2d, mem_w_k.astype(bf16), bf16, B * N_MEM, 256, name="mem_k_proj").reshape(B, N_MEM, -1)
    vm = _mm(mem2d, mem_w_v.astype(bf16), bf16, B * N_MEM, 256, name="mem_v_proj").reshape(B, N_MEM, -1)
    x2, x2b = _mem_attn(x1, mem_w_q.astype(bf16), km, vm, mem_w_o.astype(bf16), row(ln_mem_g), row(ln_mem_b), T, 512)

    pq = _mm(x2b, peer_w_query.astype(bf16), bf16, 1024, 512, name="peer_query_proj")
    e1, e2, gt = _peer_route(pq, peer_sub_keys)
    wts = _peer_weights(e1, e2, gt)
    y = _peer_dense(x2b, wts, peer_u, peer_v)
    x3 = _add_ln(x2, y, row(ln_ffn_g), row(ln_ffn_b), 512)
    return x3.reshape(B, T, D)


def kernel(x, mem, rel_bias, w_in, cmp_k_pe, cmp_k_w1, cmp_k_w2, cmp_v_pe, cmp_v_w1, cmp_v_w2, mla_kv_norm, mla_w_uk, mla_w_uv, w_branch_nsa, w_branch_mla, w_mix_out, ln_mix_g, ln_mix_b, mem_w_q, mem_w_k, mem_w_v, mem_w_o, ln_mem_g, ln_mem_b, peer_w_query, peer_sub_keys, peer_u, peer_v, ln_ffn_g, ln_ffn_b):
    ropes = _rope_slot_tables(x.shape[1])
    for l in range(DEPTH):
        x = _layer(x, mem, rel_bias, w_in[l], cmp_k_pe[l], cmp_k_w1[l], cmp_k_w2[l], cmp_v_pe[l], cmp_v_w1[l],
                   cmp_v_w2[l], mla_kv_norm[l], mla_w_uk[l], mla_w_uv[l], w_branch_nsa[l], w_branch_mla[l],
                   w_mix_out[l], ln_mix_g[l], ln_mix_b[l], mem_w_q[l], mem_w_k[l], mem_w_v[l], mem_w_o[l],
                   ln_mem_g[l], ln_mem_b[l], peer_w_query[l], peer_sub_keys[l], peer_u[l], peer_v[l],
                   ln_ffn_g[l], ln_ffn_b[l], ropes)
    return x
```

```python
import functools
import math

import numpy as np
import jax
import jax.numpy as jnp
from jax import lax
from jax.experimental import pallas as pl
from jax.experimental.pallas import tpu as pltpu

f32 = jnp.float32
bf16 = jnp.bfloat16
i32 = jnp.int32

D_MODEL = 2048
N_MEM = 256
NSA_HEADS = 16
NSA_GROUPS = 2
NSA_HPG = NSA_HEADS // NSA_GROUPS
HEAD_DIM = 128
CMP_BLOCK = 32
CMP_STRIDE = 16
CMP_HIDDEN = 256
SLC_BLOCK = 64
SLC_TOPK = 16
WINDOW = 512
FORCED_SCORE = 1e4
MLA_HEADS = 16
MLA_NOPE = 128
MLA_ROPE = 64
MLA_KV_RANK = 512
MLA_SLOT = 256
ROPE_THETA = 10000.0
REL_BUCKETS = 32
REL_MAX_EXACT = 16
REL_MAX_DIST = 4096
MEM_HEADS = 4
PEER_HEADS = 8
PEER_N_KEYS = 128
PEER_TOPK = 16
PEER_HK = PEER_HEADS * PEER_TOPK
DEPTH = 1
ALPHA = (2.0 * DEPTH) ** 0.25
NEG_INF = -1e30
M_INIT = -1e29
PAD_SCORE = -3e38
PAD_ID = 1e9
LOG2E = math.log2(math.e)
LN_EPS = 1e-5
RMS_EPS = 1e-6

NSA_Q_COLS = NSA_HEADS * HEAD_DIM
NSA_KV_COLS = NSA_GROUPS * HEAD_DIM
OFF_NSA_KV = NSA_Q_COLS
OFF_NSA_GATE = OFF_NSA_KV + 6 * NSA_KV_COLS
OFF_MLA_Q = OFF_NSA_GATE + 3 * NSA_HEADS
OFF_MLA_KV = OFF_MLA_Q + MLA_HEADS * (MLA_NOPE + MLA_ROPE)
OFF_MERGE = OFF_MLA_KV + MLA_KV_RANK + MLA_ROPE

LANES = 128
Q_TILE = 512
K_TILE = 512
WIN_TILE = 256
ROW_CHUNK = 32
CMP_TQ = 128
VMEM_LIMIT = 56 * 1024 * 1024

NT = (((1,), (1,)), ((), ()))


def _params(sem):
    return pltpu.CompilerParams(dimension_semantics=sem, vmem_limit_bytes=VMEM_LIMIT)


def _mm_body(a_ref, b_ref, o_ref, *, scale, act):
    acc = jnp.dot(a_ref[...].astype(bf16), b_ref[...], preferred_element_type=f32)
    if scale != 1.0:
        acc = acc * scale
    if act == "sigmoid":
        acc = jax.nn.sigmoid(acc)
    o_ref[...] = acc.astype(o_ref.dtype)


def _mm(a, b, out_dtype, tm, tn, scale=1.0, act=None, name="mm", cols=None):
    M, K = a.shape
    c0, Nc = (0, b.shape[1]) if cols is None else cols
    assert c0 % tn == 0 and Nc % tn == 0
    j0 = c0 // tn
    return pl.pallas_call(
        functools.partial(_mm_body, scale=scale, act=act),
        grid=(M // tm, Nc // tn),
        in_specs=[pl.BlockSpec((tm, K), lambda i, j: (i, 0)), pl.BlockSpec((K, tn), lambda i, j: (0, j0 + j))],
        out_specs=pl.BlockSpec((tm, tn), lambda i, j: (i, j)),
        out_shape=jax.ShapeDtypeStruct((M, Nc), out_dtype),
        compiler_params=_params(("parallel", "parallel")),
        name=name,
    )(a, b)


def _mm_slabs_body(a_ref, b_ref, o_ref):
    acc = jnp.dot(a_ref[...], b_ref[...], preferred_element_type=f32)
    for c in range(o_ref.shape[0]):
        o_ref[c] = acc[:, c * LANES:(c + 1) * LANES]


def _mm_slabs(a, b, cols, tm, tn, name):
    M, K = a.shape
    c0, Nc = cols
    assert c0 % tn == 0 and Nc % tn == 0
    j0 = c0 // tn
    return pl.pallas_call(
        _mm_slabs_body,
        grid=(M // tm, Nc // tn),
        in_specs=[pl.BlockSpec((tm, K), lambda i, j: (i, 0)), pl.BlockSpec((K, tn), lambda i, j: (0, j0 + j))],
        out_specs=pl.BlockSpec((tn // LANES, tm, LANES), lambda i, j: (j, i, 0)),
        out_shape=jax.ShapeDtypeStruct((Nc // LANES, M, LANES), f32),
        compiler_params=_params(("parallel", "parallel")),
        name=name,
    )(a, b)


def _rope_slot(y, ta, tb, tc):
    return y * ta + pltpu.roll(y, LANES - 32, 1) * tb + pltpu.roll(y, 32, 1) * tc


def _mlaq_body(a_ref, b_ref, ta_ref, tb_ref, tc_ref, o_ref, *, scale, nslot):
    acc = jnp.dot(a_ref[...], b_ref[...], preferred_element_type=f32)
    ta, tb, tc = ta_ref[...], tb_ref[...], tc_ref[...]
    for s in range(nslot):
        lo = s * MLA_SLOT
        o_ref[:, lo:lo + MLA_NOPE] = (acc[:, lo:lo + MLA_NOPE] * scale).astype(o_ref.dtype)
        rot = _rope_slot(acc[:, lo + MLA_NOPE:lo + MLA_SLOT], ta, tb, tc)
        o_ref[:, lo + MLA_NOPE:lo + MLA_SLOT] = (rot * scale).astype(o_ref.dtype)


def _mla_q(xb, w, cols, ropes, T, tm, tn):
    M, K = xb.shape
    c0, Nc = cols
    assert c0 % tn == 0 and Nc % tn == 0
    j0 = c0 // tn
    nt = T // tm
    tab = pl.BlockSpec((tm, LANES), lambda i, j: (i % nt, 0))
    return pl.pallas_call(
        functools.partial(_mlaq_body, scale=(MLA_NOPE + MLA_ROPE) ** -0.5 * LOG2E, nslot=tn // MLA_SLOT),
        grid=(M // tm, Nc // tn),
        in_specs=[pl.BlockSpec((tm, K), lambda i, j: (i, 0)), pl.BlockSpec((K, tn), lambda i, j: (0, j0 + j)),
                  tab, tab, tab],
        out_specs=pl.BlockSpec((tm, tn), lambda i, j: (i, j)),
        out_shape=jax.ShapeDtypeStruct((M, Nc), bf16),
        compiler_params=_params(("parallel", "parallel")),
        name="mla_q_proj_rope",
    )(xb, w, *ropes)


def _mlakv_body(c_ref, g_ref, wuk_ref, wuv_ref, ta_ref, tb_ref, tc_ref, k_ref, v_ref):
    x = c_ref[...]
    c = x[:, :MLA_KV_RANK]
    ms = jnp.mean(c * c, axis=-1, keepdims=True)
    cn = (c * lax.rsqrt(ms + RMS_EPS) * g_ref[...]).astype(bf16)
    kpe = _rope_slot(x[:, MLA_KV_RANK:], ta_ref[...], tb_ref[...], tc_ref[...]).astype(bf16)
    kn = jnp.dot(cn, wuk_ref[...], preferred_element_type=f32).astype(bf16)
    v_ref[...] = jnp.dot(cn, wuv_ref[...], preferred_element_type=f32).astype(bf16)
    for h in range(MLA_HEADS):
        k_ref[:, h * MLA_SLOT:h * MLA_SLOT + MLA_NOPE] = kn[:, h * MLA_NOPE:(h + 1) * MLA_NOPE]
        k_ref[:, h * MLA_SLOT + MLA_NOPE:(h + 1) * MLA_SLOT] = kpe


def _mla_kv(ckv, g, wuk, wuv, ropes, T, tm):
    M, K = ckv.shape
    nt = T // tm
    tab = pl.BlockSpec((tm, LANES), lambda i: (i % nt, 0))
    full = lambda shape: pl.BlockSpec(shape, lambda i: (0,) * len(shape))
    return pl.pallas_call(
        _mlakv_body,
        grid=(M // tm,),
        in_specs=[pl.BlockSpec((tm, K), lambda i: (i, 0)), full(g.shape), full(wuk.shape), full(wuv.shape), tab, tab, tab],
        out_specs=[pl.BlockSpec((tm, MLA_HEADS * MLA_SLOT), lambda i: (i, 0)),
                   pl.BlockSpec((tm, MLA_HEADS * HEAD_DIM), lambda i: (i, 0))],
        out_shape=[jax.ShapeDtypeStruct((M, MLA_HEADS * MLA_SLOT), bf16),
                   jax.ShapeDtypeStruct((M, MLA_HEADS * HEAD_DIM), bf16)],
        compiler_params=_params(("parallel",)),
        name="mla_kv_prep",
    )(ckv, g, wuk, wuv, *ropes)


FLAG_FIRST, FLAG_LAST, FLAG_CAUSAL = 1, 2, 4


def _tile_schedule(nq, tq, tk, first_key_tile):
    qs, ks, fl = [], [], []
    for qi in range(nq):
        lo = first_key_tile(qi)
        hi = (qi * tq + tq - 1) // tk
        for ki in range(lo, hi + 1):
            qs.append(qi)
            ks.append(ki)
            fl.append((FLAG_FIRST if ki == lo else 0) | (FLAG_LAST if ki == hi else 0)
                      | (FLAG_CAUSAL if ki * tk + tk - 1 > qi * tq else 0))
    return tuple(jnp.asarray(np.array(a, np.int32)) for a in (qs, ks, fl))


def _flash_scratch(heads, tq, tk):
    stat = pltpu.VMEM((heads, tq, LANES), f32)
    return [stat, stat, pltpu.VMEM((heads, tq, HEAD_DIM), f32),
            pltpu.VMEM((3, tq, tk), f32), pltpu.VMEM((2, tq, tk), bf16), pltpu.VMEM((2, tq, LANES), f32),
            pltpu.VMEM((2, tq, LANES), f32), pltpu.VMEM((tq, tk), f32)]


def _flash_init(m_ref, l_ref, acc_ref):
    m_ref[...] = jnp.full(m_ref.shape, M_INIT, f32)
    l_ref[...] = jnp.zeros(l_ref.shape, f32)
    acc_ref[...] = jnp.zeros(acc_ref.shape, f32)


def _flash_heads(nheads, tq, tk, score_fn, add_fn, value_fn, m_ref, l_ref, acc_ref, s_scr, p_scr, mx_scr, a_scr):
    nkc = tk // LANES
    chunks = [(pl.ds(c * ROW_CHUNK, ROW_CHUNK), c * ROW_CHUNK) for c in range(tq // ROW_CHUNK)]

    def scores(h):
        s_scr[h % 3] = score_fn(h)

    def pass_a(h):
        s_ref = s_scr.at[h % 3]
        for rows, r0 in chunks:
            mx = None
            for kc in range(nkc):
                cols = slice(kc * LANES, (kc + 1) * LANES)
                s = s_ref[rows, cols]
                if add_fn is not None:
                    for term in add_fn(h, rows, r0, kc):
                        s = s + term
                    s_ref[rows, cols] = s
                mx = s if mx is None else jnp.maximum(mx, s)
            mx_scr[h % 2, rows, :] = mx
        m_prev = m_ref[h]
        m_new = jnp.maximum(m_prev, jnp.max(mx_scr[h % 2], axis=-1, keepdims=True))
        a_scr[h % 2] = jnp.exp2(m_prev - m_new)
        m_ref[h] = m_new

    def pass_b(h):
        s_ref = s_scr.at[h % 3]
        for rows, _ in chunks:
            m_rows = m_ref[h, rows, :]
            for kc in range(nkc):
                cols = slice(kc * LANES, (kc + 1) * LANES)
                p_scr[h % 2, rows, cols] = jnp.exp2(s_ref[rows, cols] - m_rows).astype(bf16)
        pv = jnp.dot(p_scr[h % 2], value_fn(h), preferred_element_type=f32)
        alpha = a_scr[h % 2]
        acc_ref[h] = alpha * acc_ref[h] + pv[:, :HEAD_DIM]
        l_ref[h] = alpha * l_ref[h] + pv[:, HEAD_DIM:]

    scores(0)
    for h in range(nheads):
        if h + 1 < nheads:
            scores(h + 1)
        pass_a(h)
        if h >= 1:
            pass_b(h - 1)
    pass_b(nheads - 1)


def _with_ones(v):
    return jnp.concatenate([v, jnp.ones((v.shape[0], LANES), v.dtype)], axis=1)


def _mla_attn_body(qi_tab, ki_tab, fl_tab, q_ref, k_ref, v_ref, o_ref, m_ref, l_ref, acc_ref,
                   s_scr, p_scr, mx_scr, a_scr, mask_scr, *, hb, tq, tk):
    st = pl.program_id(2)
    qi, ki, fl = qi_tab[st], ki_tab[st], fl_tab[st]

    pl.when((fl & FLAG_FIRST) != 0)(lambda: _flash_init(m_ref, l_ref, acc_ref))

    def score(h):
        qk = slice(h * MLA_SLOT, (h + 1) * MLA_SLOT)
        return lax.dot_general(q_ref[:, qk], k_ref[:, qk], NT, preferred_element_type=f32)

    def value(h):
        return _with_ones(v_ref[:, h * HEAD_DIM:(h + 1) * HEAD_DIM])

    def tile(causal):
        add = None
        if causal:
            row = qi * tq + lax.broadcasted_iota(i32, (tq, tk), 0)
            col = ki * tk + lax.broadcasted_iota(i32, (tq, tk), 1)
            mask_scr[...] = jnp.where(row >= col, 0.0, NEG_INF)
            add = lambda h, rows, r0, kc: (mask_scr[rows, kc * LANES:(kc + 1) * LANES],)
        _flash_heads(hb, tq, tk, score, add, value, m_ref, l_ref, acc_ref, s_scr, p_scr, mx_scr, a_scr)

    pl.when((fl & FLAG_CAUSAL) != 0)(lambda: tile(True))
    pl.when((fl & FLAG_CAUSAL) == 0)(lambda: tile(False))

    @pl.when((fl & FLAG_LAST) != 0)
    def _():
        for h in range(hb):
            o_ref[:, h * HEAD_DIM:(h + 1) * HEAD_DIM] = (acc_ref[h] / l_ref[h]).astype(o_ref.dtype)


def _mla_attn(q, k, v, B, T, hb=MLA_HEADS, tq=Q_TILE, tk=K_TILE):
    tk = min(tk, T)
    nq, nk = T // tq, T // tk
    tabs = _tile_schedule(nq, tq, tk, lambda qi: 0)
    grid_spec = pltpu.PrefetchScalarGridSpec(
        num_scalar_prefetch=3,
        grid=(B, MLA_HEADS // hb, int(tabs[0].shape[0])),
        in_specs=[pl.BlockSpec((tq, hb * MLA_SLOT), lambda b, h, s, qt, kt, ft: (b * nq + qt[s], h)),
                  pl.BlockSpec((tk, hb * MLA_SLOT), lambda b, h, s, qt, kt, ft: (b * nk + kt[s], h)),
                  pl.BlockSpec((tk, hb * HEAD_DIM), lambda b, h, s, qt, kt, ft: (b * nk + kt[s], h))],
        out_specs=pl.BlockSpec((tq, hb * HEAD_DIM), lambda b, h, s, qt, kt, ft: (b * nq + qt[s], h)),
        scratch_shapes=_flash_scratch(hb, tq, tk))
    return pl.pallas_call(
        functools.partial(_mla_attn_body, hb=hb, tq=tq, tk=tk),
        grid_spec=grid_spec,
        out_shape=jax.ShapeDtypeStruct((B * T, MLA_HEADS * HEAD_DIM), bf16),
        compiler_params=_params(("parallel", "parallel", "arbitrary")),
        name="mla_flash_attention",
    )(*tabs, q, k, v)


def _bias_table_body(rev_ref, o_ref):
    width = rev_ref.shape[-1]
    x = jnp.broadcast_to(rev_ref[0], (LANES, width))
    rolled = pltpu.roll(x, 0, 1, stride=1, stride_axis=0)
    for kb in range(width // LANES):
        o_ref[0, kb] = rolled[:, kb * LANES:(kb + 1) * LANES]


def _bias_table(rev):
    H, _, U = rev.shape
    return pl.pallas_call(
        _bias_table_body,
        grid=(H,),
        in_specs=[pl.BlockSpec((1, 1, U), lambda h: (h, 0, 0))],
        out_specs=pl.BlockSpec((1, U // LANES, LANES, LANES), lambda h: (h, 0, 0, 0)),
        out_shape=jax.ShapeDtypeStruct((H, U // LANES, LANES, LANES), f32),
        compiler_params=_params(("parallel",)),
        name="relative_bias_toeplitz_table",
    )(rev)


def _nsa_band_body(qi_tab, ki_tab, fl_tab, q_ref, k_ref, v_ref, g_ref, *rest, tq, tk, seq, select, kb_lo):
    if select:
        sel_ref, e_ref, o_ref, m_ref, l_ref, acc_ref, s_scr, p_scr, mx_scr, a_scr, mask_scr = rest
    else:
        oc_ref, os_ref, gate_ref, o_ref, m_ref, l_ref, acc_ref, s_scr, p_scr, mx_scr, a_scr, mask_scr = rest
    st = pl.program_id(2)
    qi, ki, fl = qi_tab[st], ki_tab[st], fl_tab[st]

    pl.when((fl & FLAG_FIRST) != 0)(lambda: _flash_init(m_ref, l_ref, acc_ref))

    d0 = qi * tq - ki * tk
    dist = d0 + lax.broadcasted_iota(i32, (tq, tk), 0) - lax.broadcasted_iota(i32, (tq, tk), 1)
    if select:
        picked = jnp.dot(sel_ref[...], e_ref[...], preferred_element_type=f32)
        valid = (dist >= 0) & (picked > 0.5)
    else:
        valid = (dist >= 0) & (dist < WINDOW)
    mask_scr[...] = jnp.where(valid, 0.0, NEG_INF)
    ng = k_ref.shape[0]
    ks = [k_ref[gi].astype(bf16) for gi in range(ng)]
    vs = [_with_ones(v_ref[gi].astype(bf16)) for gi in range(ng)]
    kb0 = (seq - d0) // LANES - kb_lo

    def score(h):
        return lax.dot_general(q_ref[:, h * HEAD_DIM:(h + 1) * HEAD_DIM], ks[h // NSA_HPG], NT,
                               preferred_element_type=f32)

    def add(h, rows, r0, kc):
        bias = g_ref[h, kb0 - r0 // LANES + kc, pl.ds(r0 % LANES, ROW_CHUNK), :]
        return bias, mask_scr[rows, kc * LANES:(kc + 1) * LANES]

    _flash_heads(ng * NSA_HPG, tq, tk, score, add, lambda h: vs[h // NSA_HPG], m_ref, l_ref, acc_ref,
                 s_scr, p_scr, mx_scr, a_scr)

    @pl.when((fl & FLAG_LAST) != 0)
    def _():
        for h in range(ng * NSA_HPG):
            sl = slice(h * HEAD_DIM, (h + 1) * HEAD_DIM)
            o = acc_ref[h] / l_ref[h]
            if not select:
                gate = gate_ref[...]
                g0 = (h // NSA_HPG) * LANES + h % NSA_HPG
                o = (gate[:, g0:g0 + 1] * oc_ref[:, sl].astype(f32)
                     + gate[:, g0 + NSA_HPG:g0 + NSA_HPG + 1] * os_ref[:, sl].astype(f32)
                     + gate[:, g0 + 2 * NSA_HPG:g0 + 2 * NSA_HPG + 1] * o)
            o_ref[:, sl] = o.astype(o_ref.dtype)


def _nsa_band(q, kv, gtab, B, T, kcol, vcol, tk, sel=None, emat=None, merge=None, tq=Q_TILE):
    select = sel is not None
    nq, nk = T // tq, T // tk
    first = (lambda qi: 0) if select else (lambda qi: max(qi * tq - (WINDOW - 1), 0) // tk)
    tabs = _tile_schedule(nq, tq, tk, first)
    if select:
        ng, kb_lo = 1, 0
    else:
        ng = NSA_GROUPS
        kb_lo = (T - (WINDOW + tk - 1) // tk * tk) // LANES - tq // LANES
        gtab = gtab[:, kb_lo:T // LANES + tk // LANES]
    gw = ng * NSA_HPG * HEAD_DIM
    in_specs = [
        pl.BlockSpec((tq, gw), lambda b, g, s, qt, kt, ft: (b * nq + qt[s], g)),
        pl.BlockSpec((ng, tk, HEAD_DIM), lambda b, g, s, qt, kt, ft: ((kcol * NSA_GROUPS) // ng + g, b * nk + kt[s], 0)),
        pl.BlockSpec((ng, tk, HEAD_DIM), lambda b, g, s, qt, kt, ft: ((vcol * NSA_GROUPS) // ng + g, b * nk + kt[s], 0)),
        pl.BlockSpec((ng * NSA_HPG,) + gtab.shape[1:], lambda b, g, s, qt, kt, ft: (g, 0, 0, 0),
                     pipeline_mode=pl.Buffered(1)),
    ]
    args = [q, kv, kv, gtab]
    if select:
        in_specs += [pl.BlockSpec((tq, LANES), lambda b, g, s, qt, kt, ft: (b * nq + qt[s], g)),
                     pl.BlockSpec((LANES, tk), lambda b, g, s, qt, kt, ft: (0, kt[s]))]
        args += [sel, emat]
    else:
        o_other = pl.BlockSpec((tq, gw), lambda b, g, s, qt, kt, ft: (b * nq + qt[s], g))
        in_specs += [o_other, o_other,
                     pl.BlockSpec((tq, ng * LANES), lambda b, g, s, qt, kt, ft: (b * nq + qt[s], g))]
        args += list(merge)
    grid_spec = pltpu.PrefetchScalarGridSpec(
        num_scalar_prefetch=3,
        grid=(B, NSA_GROUPS // ng, int(tabs[0].shape[0])),
        in_specs=in_specs,
        out_specs=pl.BlockSpec((tq, gw), lambda b, g, s, qt, kt, ft: (b * nq + qt[s], g)),
        scratch_shapes=_flash_scratch(ng * NSA_HPG, tq, tk))
    return pl.pallas_call(
        functools.partial(_nsa_band_body, tq=tq, tk=tk, seq=T, select=select, kb_lo=kb_lo),
        grid_spec=grid_spec,
        out_shape=jax.ShapeDtypeStruct((B * T, NSA_HEADS * HEAD_DIM), bf16),
        compiler_params=_params(("parallel", "parallel", "arbitrary")),
        name="nsa_selected_attention" if select else "nsa_window_attention",
    )(*tabs, *args)


def _compress_body(x_ref, pe_ref, w1_ref, w2_ref, o_ref, ot_ref, *, nc):
    half = CMP_BLOCK // 2
    h_lo = jnp.zeros((nc, CMP_HIDDEN), f32)
    h_hi = jnp.zeros((nc, CMP_HIDDEN), f32)
    for p in range(half):
        rows = x_ref[0, pl.ds(p, nc, stride=CMP_STRIDE), :]
        h_lo += jnp.dot((rows + pe_ref[0, p:p + 1, :]).astype(bf16), w1_ref[0, p], preferred_element_type=f32)
        h_hi += jnp.dot((rows + pe_ref[0, half + p:half + p + 1, :]).astype(bf16), w1_ref[0, half + p],
                        preferred_element_type=f32)
    hid = h_lo + pltpu.roll(h_hi, nc - 1, 0)
    out = jnp.dot(jax.nn.gelu(hid).astype(bf16), w2_ref[0], preferred_element_type=f32)
    out = jnp.where(lax.broadcasted_iota(i32, out.shape, 0) < nc - 1, out, 0.0)
    o_ref[0, 0, 0] = out.astype(bf16)
    ot_ref[0, 0, 0] = out.T.astype(bf16)


def _compress(kv, pe, w1, w2, B, T):
    nc = T // CMP_STRIDE
    return pl.pallas_call(
        functools.partial(_compress_body, nc=nc),
        grid=(2, B, NSA_GROUPS),
        in_specs=[pl.BlockSpec((1, T, HEAD_DIM), lambda c, b, g: (c * NSA_GROUPS + g, b, 0)),
                  pl.BlockSpec((1, CMP_BLOCK, HEAD_DIM), lambda c, b, g: (c, 0, 0)),
                  pl.BlockSpec((1, CMP_BLOCK, HEAD_DIM, CMP_HIDDEN), lambda c, b, g: (c, 0, 0, 0)),
                  pl.BlockSpec((1, CMP_HIDDEN, HEAD_DIM), lambda c, b, g: (c, 0, 0))],
        out_specs=[pl.BlockSpec((1, 1, 1, nc, HEAD_DIM), lambda c, b, g: (c, b, g, 0, 0)),
                   pl.BlockSpec((1, 1, 1, HEAD_DIM, nc), lambda c, b, g: (c, b, g, 0, 0))],
        out_shape=[jax.ShapeDtypeStruct((2, B, NSA_GROUPS, nc, HEAD_DIM), bf16),
                   jax.ShapeDtypeStruct((2, B, NSA_GROUPS, HEAD_DIM, nc), bf16)],
        compiler_params=_params(("parallel", "parallel", "parallel")),
        name="nsa_compress_mlp",
    )(kv, pe, w1, w2)


def _cmp_body(q_ref, kc_ref, vct_ref, cw_ref, ovt_ref, o_ref, sel_ref, *, nc, nq):
    tq = CMP_TQ
    qi = pl.program_id(2)
    t0 = qi * tq
    key_row = lax.broadcasted_iota(i32, (nc, tq), 0)
    q_col = lax.broadcasted_iota(i32, (nc, tq), 1)
    valid = (t0 + q_col - CMP_STRIDE * key_row - (CMP_BLOCK - 1)) >= 0
    kc = kc_ref[0, 0, 0]
    vct = vct_ref[0, 0, 0]
    psum = jnp.zeros((nc, tq), f32)
    for h in range(NSA_HPG):
        bias = jnp.concatenate(
            [jnp.concatenate([cw_ref[h, qi * (tq // LANES) + b - c + nq] for b in range(tq // LANES)], axis=1)
             for c in range(nc // 8)], axis=0)
        s = lax.dot_general(kc, q_ref[:, h * HEAD_DIM:(h + 1) * HEAD_DIM], NT, preferred_element_type=f32)
        s = jnp.where(valid, s + bias, NEG_INF)
        m = jnp.max(s, axis=0, keepdims=True)
        e = jnp.where(valid, jnp.exp2(s - m), 0.0)
        l = jnp.sum(e, axis=0, keepdims=True)
        p = e / jnp.where(l > 0.0, l, 1.0)
        psum += p
        o_t = jnp.dot(vct, p.astype(bf16), preferred_element_type=f32)
        o_ref[:, h * HEAD_DIM:(h + 1) * HEAD_DIM] = o_t.T.astype(o_ref.dtype)

    imp = jnp.dot(ovt_ref[...], psum, preferred_element_type=f32, precision=lax.Precision.HIGHEST)
    nrow = imp.shape[0]
    n_slc = (nc * CMP_STRIDE) // SLC_BLOCK
    j = lax.broadcasted_iota(i32, (nrow, tq), 0)
    t = t0 + lax.broadcasted_iota(i32, (nrow, tq), 1)
    cur = jnp.right_shift(t, 6)
    ok = j * SLC_BLOCK <= t
    forced = (j == 0) | (j == cur) | (j == cur - 1)
    score = jnp.where(forced, FORCED_SCORE, jnp.where(ok, imp, NEG_INF))
    score = jnp.where(j < n_slc, score, PAD_SCORE)
    chosen = jnp.zeros((nrow, tq), f32)
    row_id = j.astype(f32)
    for _ in range(min(SLC_TOPK, n_slc)):
        top = jnp.max(score, axis=0, keepdims=True)
        first = jnp.min(jnp.where(score == top, row_id, PAD_ID), axis=0, keepdims=True)
        hit = row_id == first
        chosen = jnp.where(hit, 1.0, chosen)
        score = jnp.where(hit, PAD_SCORE, score)
    chosen = jnp.where(ok & (j < n_slc), chosen, 0.0)
    sel_ref[...] = chosen.T.astype(sel_ref.dtype)


def _nsa_cmp(q, kc, vct, cw, ovt, B, T):
    nc = T // CMP_STRIDE
    nq = T // CMP_TQ
    gw = NSA_HPG * HEAD_DIM
    return pl.pallas_call(
        functools.partial(_cmp_body, nc=nc, nq=T // LANES),
        grid=(B, NSA_GROUPS, nq),
        in_specs=[pl.BlockSpec((CMP_TQ, gw), lambda b, g, qi: (b * nq + qi, g)),
                  pl.BlockSpec((1, 1, 1, nc, HEAD_DIM), lambda b, g, qi: (0, b, g, 0, 0)),
                  pl.BlockSpec((1, 1, 1, HEAD_DIM, nc), lambda b, g, qi: (1, b, g, 0, 0)),
                  pl.BlockSpec((NSA_HPG,) + cw.shape[1:], lambda b, g, qi: (g, 0, 0, 0)),
                  pl.BlockSpec(ovt.shape, lambda b, g, qi: (0, 0))],
        out_specs=[pl.BlockSpec((CMP_TQ, gw), lambda b, g, qi: (b * nq + qi, g)),
                   pl.BlockSpec((CMP_TQ, LANES), lambda b, g, qi: (b * nq + qi, g))],
        out_shape=[jax.ShapeDtypeStruct((B * T, NSA_HEADS * HEAD_DIM), bf16),
                   jax.ShapeDtypeStruct((B * T, NSA_GROUPS * LANES), bf16)],
        compiler_params=_params(("parallel", "parallel", "parallel")),
        name="nsa_compressed_attention_select",
    )(q, kc, vct, cw, ovt)


def _layer_norm(y, g, b):
    mu = jnp.mean(y, axis=-1, keepdims=True)
    var = jnp.mean(jnp.square(y - mu), axis=-1, keepdims=True)
    return (y - mu) * lax.rsqrt(var + LN_EPS) * g + b


def _branch_merge_body(a_ref, b_ref, wa_ref, wb_ref, ga_ref, gb_ref, o_ref):
    ya = jnp.dot(a_ref[...], wa_ref[...], preferred_element_type=f32)
    yb = jnp.dot(b_ref[...], wb_ref[...], preferred_element_type=f32)
    o_ref[...] = (ga_ref[...] * ya + gb_ref[...] * yb).astype(o_ref.dtype)


def _branch_merge(a, b, wa, wb, mg, tm, tn):
    M, K = a.shape
    Nc = wa.shape[1]
    nj = Nc // tn
    return pl.pallas_call(
        _branch_merge_body,
        grid=(M // tm, nj),
        in_specs=[pl.BlockSpec((tm, K), lambda i, j: (i, 0)), pl.BlockSpec((tm, K), lambda i, j: (i, 0)),
                  pl.BlockSpec((K, tn), lambda i, j: (0, j)), pl.BlockSpec((K, tn), lambda i, j: (0, j)),
                  pl.BlockSpec((tm, tn), lambda i, j: (i, j)), pl.BlockSpec((tm, tn), lambda i, j: (i, nj + j))],
        out_specs=pl.BlockSpec((tm, tn), lambda i, j: (i, j)),
        out_shape=jax.ShapeDtypeStruct((M, Nc), bf16),
        compiler_params=_params(("parallel", "parallel")),
        name="branch_proj_merge",
    )(a, b, wa, wb, mg, mg)


def _proj_ln_body(z_ref, w_ref, x_ref, g_ref, b_ref, o_ref):
    y = jnp.dot(z_ref[...], w_ref[...], preferred_element_type=f32)
    o_ref[...] = _layer_norm(ALPHA * x_ref[...] + y, g_ref[...], b_ref[...])


def _proj_ln(z, w, x, g, b, tm):
    M, K = z.shape
    D = w.shape[1]
    vec = pl.BlockSpec((1, D), lambda i: (0, 0))
    return pl.pallas_call(
        _proj_ln_body,
        grid=(M // tm,),
        in_specs=[pl.BlockSpec((tm, K), lambda i: (i, 0)), pl.BlockSpec((K, D), lambda i: (0, 0)),
                  pl.BlockSpec((tm, D), lambda i: (i, 0)), vec, vec],
        out_specs=pl.BlockSpec((tm, D), lambda i: (i, 0)),
        out_shape=jax.ShapeDtypeStruct((M, D), f32),
        compiler_params=_params(("parallel",)),
        name="mix_out_proj_layernorm",
    )(z, w, x, g, b)


def _mem_attn_body(x_ref, wq_ref, k_ref, v_ref, wo_ref, g_ref, b_ref, o_ref, ob_ref):
    x = x_ref[...]
    q = (jnp.dot(x.astype(bf16), wq_ref[...], preferred_element_type=f32) * HEAD_DIM ** -0.5).astype(bf16)
    heads = []
    for h in range(MEM_HEADS):
        sl = slice(h * HEAD_DIM, (h + 1) * HEAD_DIM)
        s = lax.dot_general(q[:, sl], k_ref[0, :, sl], NT, preferred_element_type=f32)
        e = jnp.exp(s - jnp.max(s, axis=-1, keepdims=True))
        p = e / jnp.sum(e, axis=-1, keepdims=True)
        heads.append(jnp.dot(p.astype(bf16), v_ref[0, :, sl], preferred_element_type=f32).astype(bf16))
    y = jnp.dot(jnp.concatenate(heads, axis=1), wo_ref[...], preferred_element_type=f32)
    out = _layer_norm(ALPHA * x + y, g_ref[...], b_ref[...])
    o_ref[...] = out
    ob_ref[...] = out.astype(bf16)


def _mem_attn(x, wq, km, vm, wo, g, b, T, tm):
    M, D = x.shape
    nt = T // tm
    W = wq.shape[1]
    vec = pl.BlockSpec((1, D), lambda i: (0, 0))
    blk = pl.BlockSpec((tm, D), lambda i: (i, 0))
    return pl.pallas_call(
        _mem_attn_body,
        grid=(M // tm,),
        in_specs=[blk, pl.BlockSpec((D, W), lambda i: (0, 0)),
                  pl.BlockSpec((1, N_MEM, W), lambda i: (i // nt, 0, 0)),
                  pl.BlockSpec((1, N_MEM, W), lambda i: (i // nt, 0, 0)),
                  pl.BlockSpec((W, D), lambda i: (0, 0)), vec, vec],
        out_specs=[blk, blk],
        out_shape=[jax.ShapeDtypeStruct((M, D), f32), jax.ShapeDtypeStruct((M, D), bf16)],
        compiler_params=_params(("parallel",)),
        name="memory_cross_attention_layernorm",
    )(x, wq, km, vm, wo, g, b)


def _add_ln_body(x_ref, y_ref, g_ref, b_ref, o_ref):
    o_ref[...] = _layer_norm(ALPHA * x_ref[...] + y_ref[...], g_ref[...], b_ref[...])


def _add_ln(x, y, g, b, tm):
    M, D = x.shape
    vec = pl.BlockSpec((1, D), lambda i: (0, 0))
    blk = pl.BlockSpec((tm, D), lambda i: (i, 0))
    return pl.pallas_call(
        _add_ln_body,
        grid=(M // tm,),
        in_specs=[blk, blk, vec, vec],
        out_specs=blk,
        out_shape=jax.ShapeDtypeStruct((M, D), f32),
        compiler_params=_params(("parallel",)),
        name="residual_layernorm",
    )(x, y, g, b)


def _topk_rows(s, ids, k):
    n = s.shape[1]
    slot = lax.broadcasted_iota(i32, (k, n), 0)
    vals = jnp.zeros((k, n), f32)
    inds = jnp.zeros((k, n), f32)
    for r in range(k):
        top = jnp.max(s, axis=0, keepdims=True)
        first = jnp.min(jnp.where(s == top, ids, PAD_ID), axis=0, keepdims=True)
        vals = jnp.where(slot == r, top, vals)
        inds = jnp.where(slot == r, first, inds)
        s = jnp.where(ids == first, PAD_SCORE, s)
    return vals, inds


def _pick_rows(table, sel):
    out = jnp.zeros(sel.shape, table.dtype)
    for a in range(table.shape[0]):
        out = jnp.where(sel == a, table[a:a + 1, :], out)
    return out


CAND_COUNTS = tuple(PEER_TOPK // (a + 1) for a in range(PEER_TOPK))
CAND_ROWS = -(-sum(CAND_COUNTS) // 8) * 8


def _cand_ids(tn):
    ids = [a * PEER_TOPK + b for a, nb in enumerate(CAND_COUNTS) for b in range(nb)]
    ids = np.array(ids + [PAD_ID] * (CAND_ROWS - len(ids)), np.float32)
    return jnp.asarray(np.broadcast_to(ids[:, None], (CAND_ROWS, tn)))


def _peer_route_body(q_ref, keys_ref, ids_ref, e1_ref, e2_ref, gt_ref, e1_scr, e2_scr, g_scr, cand_scr):
    k1 = keys_ref[0].astype(bf16)
    k2 = keys_ref[1].astype(bf16)
    tn = q_ref.shape[0]
    key_ids = lax.broadcasted_iota(i32, (PEER_N_KEYS, tn), 0).astype(f32)
    cand_scr[...] = jnp.full(cand_scr.shape, PAD_SCORE, f32)
    for h in range(PEER_HEADS):
        q1 = q_ref[:, (2 * h) * LANES:(2 * h + 1) * LANES]
        q2 = q_ref[:, (2 * h + 1) * LANES:(2 * h + 2) * LANES]
        s1 = lax.dot_general(k1, q1, NT, preferred_element_type=f32)
        s2 = lax.dot_general(k2, q2, NT, preferred_element_type=f32)
        v1, i1 = _topk_rows(s1, key_ids, PEER_TOPK)
        v2, i2 = _topk_rows(s2, key_ids, PEER_TOPK)
        off = 0
        for a, nb in enumerate(CAND_COUNTS):
            cand_scr[off:off + nb, :] = v1[a:a + 1, :] + v2[0:nb, :]
            off += nb
        top_s, flat = _topk_rows(cand_scr[...], ids_ref[...], PEER_TOPK)
        flat = flat.astype(i32)
        e = jnp.exp(top_s - top_s[0:1, :])
        rows = slice(h * PEER_TOPK, (h + 1) * PEER_TOPK)
        g_scr[rows, :] = e / jnp.sum(e, axis=0, keepdims=True)
        e1_scr[rows, :] = _pick_rows(i1, jnp.right_shift(flat, 4)).astype(i32)
        e2_scr[rows, :] = _pick_rows(i2, jnp.bitwise_and(flat, PEER_TOPK - 1)).astype(i32)
    e1_ref[...] = e1_scr[...].T
    e2_ref[...] = e2_scr[...].T
    gt_ref[...] = g_scr[...].T


def _peer_route(q, keys, tn=LANES):
    M, W = q.shape
    out = pl.BlockSpec((tn, PEER_HK), lambda i: (i, 0))
    return pl.pallas_call(
        _peer_route_body,
        grid=(M // tn,),
        in_specs=[pl.BlockSpec((tn, W), lambda i: (i, 0)), pl.BlockSpec(keys.shape, lambda i: (0, 0, 0)),
                  pl.BlockSpec((CAND_ROWS, tn), lambda i: (0, 0))],
        out_specs=[out, out, out],
        out_shape=[jax.ShapeDtypeStruct((M, PEER_HK), i32), jax.ShapeDtypeStruct((M, PEER_HK), i32),
                   jax.ShapeDtypeStruct((M, PEER_HK), f32)],
        scratch_shapes=[pltpu.VMEM((PEER_HK, tn), i32), pltpu.VMEM((PEER_HK, tn), i32), pltpu.VMEM((PEER_HK, tn), f32),
                        pltpu.VMEM((CAND_ROWS, tn), f32)],
        compiler_params=_params(("parallel",)),
        name="peer_product_key_routing",
    )(q, keys, _cand_ids(tn))


TOKEN_GROUP = 16


def _peer_weights_body(e1_ref, e2_ref, g_ref, w_ref, *, tn):
    sub = lax.broadcasted_iota(i32, (PEER_N_KEYS, PEER_HK), 0)

    def one(n):
        r1 = e1_ref[pl.ds(n, 1), :]
        r2 = e2_ref[pl.ds(n, 1), :]
        g = g_ref[pl.ds(n, 1), :]
        a_t = (sub == r1).astype(bf16)
        b_t = jnp.where(sub == r2, g, 0.0).astype(bf16)
        return lax.dot_general(a_t, b_t, NT, preferred_element_type=f32)

    def group(i, carry):
        n0 = pl.multiple_of(i * TOKEN_GROUP, TOKEN_GROUP)
        tables = jnp.stack([one(n0 + t) for t in range(TOKEN_GROUP)], axis=0)
        w_ref[:, pl.ds(n0, TOKEN_GROUP), :] = jnp.swapaxes(tables, 0, 1).astype(w_ref.dtype)
        return carry

    lax.fori_loop(0, tn // TOKEN_GROUP, group, 0)


def _peer_weights(e1, e2, g, tn=256):
    M = e1.shape[0]
    blk = pl.BlockSpec((tn, PEER_HK), lambda i: (i, 0))
    return pl.pallas_call(
        functools.partial(_peer_weights_body, tn=tn),
        grid=(M // tn,),
        in_specs=[blk, blk, blk],
        out_specs=pl.BlockSpec((PEER_N_KEYS, tn, PEER_N_KEYS), lambda i: (0, i, 0)),
        out_shape=jax.ShapeDtypeStruct((PEER_N_KEYS, M, PEER_N_KEYS), bf16),
        compiler_params=_params(("parallel",)),
        name="peer_routing_weights",
    )(e1, e2, g)


GELU_SLOPE = 2.0 * math.sqrt(2.0 / math.pi)


def _gelu_tanh(x):
    minus_2z = x * ((-GELU_SLOPE * 0.044715 * LOG2E) * (x * x) - GELU_SLOPE * LOG2E)
    return x / (1.0 + jnp.exp2(minus_2z))


def _peer_dense_body(x_ref, w_ref, u_ref, v_ref, o_ref, *, rows):
    @pl.when(pl.program_id(1) == 0)
    def _():
        o_ref[...] = jnp.zeros(o_ref.shape, f32)

    act = _gelu_tanh(lax.dot_general(x_ref[...], u_ref[...].astype(bf16), NT, preferred_element_type=f32))
    w = jnp.concatenate([w_ref[r] for r in range(rows)], axis=1)
    o_ref[...] += jnp.dot(w * act.astype(bf16), v_ref[...].astype(bf16), preferred_element_type=f32)


def _peer_dense(xb, w, u, v, tm=1024, rows=8):
    M, D = xb.shape
    te = rows * PEER_N_KEYS
    once = dict(pipeline_mode=pl.Buffered(1))
    return pl.pallas_call(
        functools.partial(_peer_dense_body, rows=rows),
        grid=(M // tm, PEER_N_KEYS // rows),
        in_specs=[pl.BlockSpec((tm, D), lambda i, e: (i, 0), **once),
                  pl.BlockSpec((rows, tm, PEER_N_KEYS), lambda i, e: (e, i, 0)),
                  pl.BlockSpec((te, D), lambda i, e: (e, 0)), pl.BlockSpec((te, D), lambda i, e: (e, 0))],
        out_specs=pl.BlockSpec((tm, D), lambda i, e: (i, 0), **once),
        out_shape=jax.ShapeDtypeStruct((M, D), f32),
        compiler_params=_params(("parallel", "arbitrary")),
        name="peer_dense_experts",
    )(xb, w, u, v)


def _t5_bucket(dist):
    dist = jnp.maximum(dist, 0)
    n_log = REL_BUCKETS - REL_MAX_EXACT
    large = REL_MAX_EXACT + (jnp.log(jnp.maximum(dist, 1).astype(f32) / REL_MAX_EXACT)
                             / math.log(REL_MAX_DIST / REL_MAX_EXACT) * n_log).astype(i32)
    large = jnp.minimum(large, REL_BUCKETS - 1)
    return jnp.where(dist < REL_MAX_EXACT, dist, large)


def _position_tables(rel_bias, T):
    by_dist = (rel_bias[_t5_bucket(jnp.arange(T))] * LOG2E).T.astype(f32)
    rev = jnp.pad(by_dist[:, ::-1], ((0, 0), (1, K_TILE - 1)), mode="edge")[:, None, :]
    shift = (CMP_BLOCK - 1) + CMP_STRIDE * 7
    ext = jnp.pad(by_dist, ((0, 0), (T + shift, 0)), mode="edge")
    rows = [ext[:, CMP_STRIDE * (7 - r):CMP_STRIDE * (7 - r) + 2 * T] for r in range(8)]
    cmpw = jnp.stack(rows, axis=1).reshape(NSA_HEADS, 8, 2 * T // LANES, LANES)
    return rev, jnp.transpose(cmpw, (0, 2, 1, 3))


def _rope_slot_tables(T):
    inv = ROPE_THETA ** (-jnp.arange(0, MLA_ROPE, 2, dtype=f32) / MLA_ROPE)
    ang = jnp.arange(T, dtype=f32)[:, None] * inv[None, :]
    cos, sin = jnp.cos(ang), jnp.sin(ang)
    z = jnp.zeros_like(cos)
    ta = jnp.concatenate([cos, cos, z, z], axis=1)
    tb = jnp.concatenate([-sin, z, z, z], axis=1)
    tc = jnp.concatenate([z, sin, z, z], axis=1)
    return ta, tb, tc


def _layer(x, mem, rel_bias, w_in, cmp_k_pe, cmp_k_w1, cmp_k_w2, cmp_v_pe, cmp_v_w1, cmp_v_w2,
           mla_kv_norm, mla_w_uk, mla_w_uv, w_branch_nsa, w_branch_mla, w_mix_out, ln_mix_g, ln_mix_b,
           mem_w_q, mem_w_k, mem_w_v, mem_w_o, ln_mem_g, ln_mem_b,
           peer_w_query, peer_sub_keys, peer_u, peer_v, ln_ffn_g, ln_ffn_b, ropes):
    B, T, D = x.shape
    N = B * T
    x2d = x.reshape(N, D)
    xb = x2d.astype(bf16)
    row = lambda a: a.reshape(1, -1)

    gate_cols = np.array([[br * NSA_HEADS + g * NSA_HPG + h for br in range(3) for h in range(NSA_HPG)]
                          for g in range(NSA_GROUPS)])
    w_gate = jnp.pad(w_in[:, OFF_NSA_GATE:OFF_MLA_Q][:, gate_cols],
                     ((0, 0), (0, 0), (0, LANES - 3 * NSA_HPG))).reshape(D, NSA_GROUPS * LANES)
    w_mla_q = jnp.pad(w_in[:, OFF_MLA_Q:OFF_MLA_KV].reshape(D, MLA_HEADS, MLA_NOPE + MLA_ROPE),
                      ((0, 0), (0, 0), (0, MLA_SLOT - MLA_NOPE - MLA_ROPE))).reshape(D, MLA_HEADS * MLA_SLOT)
    w_mla_kv = jnp.pad(w_in[:, OFF_MLA_KV:OFF_MERGE], ((0, 0), (0, LANES - MLA_ROPE)))
    segs = [w_mla_kv, jnp.zeros((D, LANES), f32), w_gate,
            w_in[:, :OFF_NSA_KV], w_mla_q, w_in[:, OFF_MERGE:], w_in[:, OFF_NSA_KV:OFF_NSA_GATE]]
    starts = np.concatenate([[0], np.cumsum([s.shape[1] for s in segs])])
    col = lambda i: (int(starts[i]), int(segs[i].shape[1]))
    w_all = jnp.concatenate(segs, axis=1).astype(bf16)

    mla_ckv = _mm(xb, w_all, f32, 1024, MLA_KV_RANK + LANES, name="mla_kv_proj", cols=col(0))
    gates = _mm(xb, w_all, f32, 1024, NSA_GROUPS * LANES, act="sigmoid", name="nsa_gate_proj", cols=col(2))
    nsa_q = _mm(xb, w_all, bf16, 1024, 1024, scale=HEAD_DIM ** -0.5 * LOG2E, name="nsa_q_proj", cols=col(3))
    mla_q = _mla_q(xb, w_all, col(4), ropes, T, 1024, 1024)
    merge = _mm(xb, w_all, f32, 1024, 1024, act="sigmoid", name="merge_gate_proj", cols=col(5))
    nsa_kv = _mm_slabs(xb, w_all, col(6), 1024, 512, "nsa_kv_proj")

    slc_tk = min(K_TILE, T)
    rev_tab, cmp_tab = _position_tables(rel_bias, T)
    bias_tab = _bias_table(rev_tab)
    pe = jnp.stack([cmp_k_pe, cmp_v_pe])
    w1 = jnp.stack([cmp_k_w1, cmp_v_w1]).reshape(2, CMP_BLOCK, HEAD_DIM, CMP_HIDDEN).astype(bf16)
    w2 = jnp.stack([cmp_k_w2, cmp_v_w2]).astype(bf16)
    kvc, kvc_t = _compress(nsa_kv, pe, w1, w2, B, T)
    nc = T // CMP_STRIDE
    n_slc = T // SLC_BLOCK
    cs = np.arange(nc)[None, :] * CMP_STRIDE
    ss = np.arange(LANES)[:, None] * SLC_BLOCK
    ovt = ((cs < ss + SLC_BLOCK) & (cs + CMP_BLOCK > ss) & (np.arange(nc)[None, :] < nc - 1)
           & (np.arange(LANES)[:, None] < n_slc)).astype(np.float32)
    o_cmp, sel = _nsa_cmp(nsa_q, kvc, kvc_t, cmp_tab, jnp.asarray(ovt), B, T)
    emat = (np.arange(T)[None, :] // SLC_BLOCK == np.arange(LANES)[:, None]).astype(np.float32)
    o_slc = _nsa_band(nsa_q, nsa_kv, bias_tab, B, T, 2, 3, slc_tk, sel=sel, emat=jnp.asarray(emat, dtype=bf16))
    o_nsa = _nsa_band(nsa_q, nsa_kv, bias_tab, B, T, 4, 5, WIN_TILE, merge=(o_cmp, o_slc, gates), tq=WIN_TILE)

    mla_k, mla_v = _mla_kv(mla_ckv, row(mla_kv_norm),
                           jnp.asarray(mla_w_uk, bf16), jnp.asarray(mla_w_uv, bf16), ropes, T, 512)
    o_mla = _mla_attn(mla_q, mla_k, mla_v, B, T)

    z = _branch_merge(o_nsa, o_mla, w_branch_nsa.astype(bf16), w_branch_mla.astype(bf16), merge, 1024, 512)
    x1 = _proj_ln(z, w_mix_out.astype(bf16), x2d, row(ln_mix_g), row(ln_mix_b), 512)

    mem2d = mem.reshape(B * N_MEM, D).astype(bf16)
    km = _mm(mem2d, mem_w_k.astype(bf16), bf16, B * N_MEM, 256, name="mem_k_proj").reshape(B, N_MEM, -1)
    vm = _mm(mem2d, mem_w_v.astype(bf16), bf16, B * N_MEM, 256, name="mem_v_proj").reshape(B, N_MEM, -1)
    x2, x2b = _mem_attn(x1, mem_w_q.astype(bf16), km, vm, mem_w_o.astype(bf16), row(ln_mem_g), row(ln_mem_b), T, 512)

    pq = _mm(x2b, peer_w_query.astype(bf16), bf16, 1024, 512, name="peer_query_proj")
    e1, e2, gt = _peer_route(pq, peer_sub_keys)
    wts = _peer_weights(e1, e2, gt)
    y = _peer_dense(x2b, wts, peer_u, peer_v)
    x3 = _add_ln(x2, y, row(ln_ffn_g), row(ln_ffn_b), 512)
    return x3.reshape(B, T, D)


def kernel(x, mem, rel_bias, w_in, cmp_k_pe, cmp_k_w1, cmp_k_w2, cmp_v_pe, cmp_v_w1, cmp_v_w2, mla_kv_norm, mla_w_uk, mla_w_uv, w_branch_nsa, w_branch_mla, w_mix_out, ln_mix_g, ln_mix_b, mem_w_q, mem_w_k, mem_w_v, mem_w_o, ln_mem_g, ln_mem_b, peer_w_query, peer_sub_keys, peer_u, peer_v, ln_ffn_g, ln_ffn_b):
    ropes = _rope_slot_tables(x.shape[1])
    for l in range(DEPTH):
        x = _layer(x, mem, rel_bias, w_in[l], cmp_k_pe[l], cmp_k_w1[l], cmp_k_w2[l], cmp_v_pe[l], cmp_v_w1[l],
                   cmp_v_w2[l], mla_kv_norm[l], mla_w_uk[l], mla_w_uv[l], w_branch_nsa[l], w_branch_mla[l],
                   w_mix_out[l], ln_mix_g[l], ln_mix_b[l], mem_w_q[l], mem_w_k[l], mem_w_v[l], mem_w_o[l],
                   ln_mem_g[l], ln_mem_b[l], peer_w_query[l], peer_sub_keys[l], peer_u[l], peer_v[l],
                   ln_ffn_g[l], ln_ffn_b[l], ropes)
    return x
```

```python
import functools
import math

import numpy as np
import jax
import jax.numpy as jnp
from jax import lax
from jax.experimental import pallas as pl
from jax.experimental.pallas import tpu as pltpu

f32 = jnp.float32
bf16 = jnp.bfloat16
i32 = jnp.int32

D_MODEL = 2048
N_MEM = 256
NSA_HEADS = 16
NSA_GROUPS = 2
NSA_HPG = NSA_HEADS // NSA_GROUPS
HEAD_DIM = 128
CMP_BLOCK = 32
CMP_STRIDE = 16
CMP_HIDDEN = 256
SLC_BLOCK = 64
SLC_TOPK = 16
WINDOW = 512
FORCED_SCORE = 1e4
MLA_HEADS = 16
MLA_NOPE = 128
MLA_ROPE = 64
MLA_KV_RANK = 512
MLA_SLOT = 256
ROPE_THETA = 10000.0
REL_BUCKETS = 32
REL_MAX_EXACT = 16
REL_MAX_DIST = 4096
MEM_HEADS = 4
PEER_HEADS = 8
PEER_N_KEYS = 128
PEER_TOPK = 16
PEER_HK = PEER_HEADS * PEER_TOPK
DEPTH = 1
ALPHA = (2.0 * DEPTH) ** 0.25
NEG_INF = -1e30
M_INIT = -1e29
PAD_SCORE = -3e38
PAD_ID = 1e9
LOG2E = math.log2(math.e)
LN_EPS = 1e-5
RMS_EPS = 1e-6

NSA_Q_COLS = NSA_HEADS * HEAD_DIM
NSA_KV_COLS = NSA_GROUPS * HEAD_DIM
OFF_NSA_KV = NSA_Q_COLS
OFF_NSA_GATE = OFF_NSA_KV + 6 * NSA_KV_COLS
OFF_MLA_Q = OFF_NSA_GATE + 3 * NSA_HEADS
OFF_MLA_KV = OFF_MLA_Q + MLA_HEADS * (MLA_NOPE + MLA_ROPE)
OFF_MERGE = OFF_MLA_KV + MLA_KV_RANK + MLA_ROPE

LANES = 128
Q_TILE = 512
K_TILE = 512
WIN_TILE = 256
ROW_CHUNK = 32
CMP_TQ = 128
VMEM_LIMIT = 56 * 1024 * 1024

NT = (((1,), (1,)), ((), ()))


def _params(sem):
    return pltpu.CompilerParams(dimension_semantics=sem, vmem_limit_bytes=VMEM_LIMIT)


def _mm_body(a_ref, b_ref, o_ref, *, scale, act):
    acc = jnp.dot(a_ref[...].astype(bf16), b_ref[...], preferred_element_type=f32)
    if scale != 1.0:
        acc = acc * scale
    if act == "sigmoid":
        acc = jax.nn.sigmoid(acc)
    o_ref[...] = acc.astype(o_ref.dtype)


def _mm(a, b, out_dtype, tm, tn, scale=1.0, act=None, name="mm", cols=None):
    M, K = a.shape
    c0, Nc = (0, b.shape[1]) if cols is None else cols
    assert c0 % tn == 0 and Nc % tn == 0
    j0 = c0 // tn
    return pl.pallas_call(
        functools.partial(_mm_body, scale=scale, act=act),
        grid=(M // tm, Nc // tn),
        in_specs=[pl.BlockSpec((tm, K), lambda i, j: (i, 0)), pl.BlockSpec((K, tn), lambda i, j: (0, j0 + j))],
        out_specs=pl.BlockSpec((tm, tn), lambda i, j: (i, j)),
        out_shape=jax.ShapeDtypeStruct((M, Nc), out_dtype),
        compiler_params=_params(("parallel", "parallel")),
        name=name,
    )(a, b)


def _mm_slabs_body(a_ref, b_ref, o_ref):
    acc = jnp.dot(a_ref[...], b_ref[...], preferred_element_type=f32)
    for c in range(o_ref.shape[0]):
        o_ref[c] = acc[:, c * LANES:(c + 1) * LANES]


def _mm_slabs(a, b, cols, tm, tn, name):
    M, K = a.shape
    c0, Nc = cols
    assert c0 % tn == 0 and Nc % tn == 0
    j0 = c0 // tn
    return pl.pallas_call(
        _mm_slabs_body,
        grid=(M // tm, Nc // tn),
        in_specs=[pl.BlockSpec((tm, K), lambda i, j: (i, 0)), pl.BlockSpec((K, tn), lambda i, j: (0, j0 + j))],
        out_specs=pl.BlockSpec((tn // LANES, tm, LANES), lambda i, j: (j, i, 0)),
        out_shape=jax.ShapeDtypeStruct((Nc // LANES, M, LANES), f32),
        compiler_params=_params(("parallel", "parallel")),
        name=name,
    )(a, b)


def _rope_slot(y, ta, tb, tc):
    return y * ta + pltpu.roll(y, LANES - 32, 1) * tb + pltpu.roll(y, 32, 1) * tc


def _mlaq_body(a_ref, b_ref, ta_ref, tb_ref, tc_ref, o_ref, *, scale, nslot):
    acc = jnp.dot(a_ref[...], b_ref[...], preferred_element_type=f32)
    ta, tb, tc = ta_ref[...], tb_ref[...], tc_ref[...]
    for s in range(nslot):
        lo = s * MLA_SLOT
        o_ref[:, lo:lo + MLA_NOPE] = (acc[:, lo:lo + MLA_NOPE] * scale).astype(o_ref.dtype)
        rot = _rope_slot(acc[:, lo + MLA_NOPE:lo + MLA_SLOT], ta, tb, tc)
        o_ref[:, lo + MLA_NOPE:lo + MLA_SLOT] = (rot * scale).astype(o_ref.dtype)


def _mla_q(xb, w, cols, ropes, T, tm, tn):
    M, K = xb.shape
    c0, Nc = cols
    assert c0 % tn == 0 and Nc % tn == 0
    j0 = c0 // tn
    nt = T // tm
    tab = pl.BlockSpec((tm, LANES), lambda i, j: (i % nt, 0))
    return pl.pallas_call(
        functools.partial(_mlaq_body, scale=(MLA_NOPE + MLA_ROPE) ** -0.5 * LOG2E, nslot=tn // MLA_SLOT),
        grid=(M // tm, Nc // tn),
        in_specs=[pl.BlockSpec((tm, K), lambda i, j: (i, 0)), pl.BlockSpec((K, tn), lambda i, j: (0, j0 + j)),
                  tab, tab, tab],
        out_specs=pl.BlockSpec((tm, tn), lambda i, j: (i, j)),
        out_shape=jax.ShapeDtypeStruct((M, Nc), bf16),
        compiler_params=_params(("parallel", "parallel")),
        name="mla_q_proj_rope",
    )(xb, w, *ropes)


def _mlakv_body(c_ref, g_ref, wuk_ref, wuv_ref, ta_ref, tb_ref, tc_ref, k_ref, v_ref):
    x = c_ref[...]
    c = x[:, :MLA_KV_RANK]
    ms = jnp.mean(c * c, axis=-1, keepdims=True)
    cn = (c * lax.rsqrt(ms + RMS_EPS) * g_ref[...]).astype(bf16)
    kpe = _rope_slot(x[:, MLA_KV_RANK:], ta_ref[...], tb_ref[...], tc_ref[...]).astype(bf16)
    kn = jnp.dot(cn, wuk_ref[...], preferred_element_type=f32).astype(bf16)
    v_ref[...] = jnp.dot(cn, wuv_ref[...], preferred_element_type=f32).astype(bf16)
    for h in range(MLA_HEADS):
        k_ref[:, h * MLA_SLOT:h * MLA_SLOT + MLA_NOPE] = kn[:, h * MLA_NOPE:(h + 1) * MLA_NOPE]
        k_ref[:, h * MLA_SLOT + MLA_NOPE:(h + 1) * MLA_SLOT] = kpe


def _mla_kv(ckv, g, wuk, wuv, ropes, T, tm):
    M, K = ckv.shape
    nt = T // tm
    tab = pl.BlockSpec((tm, LANES), lambda i: (i % nt, 0))
    full = lambda shape: pl.BlockSpec(shape, lambda i: (0,) * len(shape))
    return pl.pallas_call(
        _mlakv_body,
        grid=(M // tm,),
        in_specs=[pl.BlockSpec((tm, K), lambda i: (i, 0)), full(g.shape), full(wuk.shape), full(wuv.shape), tab, tab, tab],
        out_specs=[pl.BlockSpec((tm, MLA_HEADS * MLA_SLOT), lambda i: (i, 0)),
                   pl.BlockSpec((tm, MLA_HEADS * HEAD_DIM), lambda i: (i, 0))],
        out_shape=[jax.ShapeDtypeStruct((M, MLA_HEADS * MLA_SLOT), bf16),
                   jax.ShapeDtypeStruct((M, MLA_HEADS * HEAD_DIM), bf16)],
        compiler_params=_params(("parallel",)),
        name="mla_kv_prep",
    )(ckv, g, wuk, wuv, *ropes)


FLAG_FIRST, FLAG_LAST, FLAG_CAUSAL = 1, 2, 4


def _tile_schedule(nq, tq, tk, first_key_tile):
    qs, ks, fl = [], [], []
    for qi in range(nq):
        lo = first_key_tile(qi)
        hi = (qi * tq + tq - 1) // tk
        for ki in range(lo, hi + 1):
            qs.append(qi)
            ks.append(ki)
            fl.append((FLAG_FIRST if ki == lo else 0) | (FLAG_LAST if ki == hi else 0)
                      | (FLAG_CAUSAL if ki * tk + tk - 1 > qi * tq else 0))
    return tuple(jnp.asarray(np.array(a, np.int32)) for a in (qs, ks, fl))


def _flash_scratch(heads, tq, tk):
    stat = pltpu.VMEM((heads, tq, LANES), f32)
    return [stat, stat, pltpu.VMEM((heads, tq, HEAD_DIM), f32),
            pltpu.VMEM((3, tq, tk), f32), pltpu.VMEM((2, tq, tk), bf16), pltpu.VMEM((2, tq, LANES), f32),
            pltpu.VMEM((2, tq, LANES), f32), pltpu.VMEM((tq, tk), f32)]


def _flash_init(m_ref, l_ref, acc_ref):
    m_ref[...] = jnp.full(m_ref.shape, M_INIT, f32)
    l_ref[...] = jnp.zeros(l_ref.shape, f32)
    acc_ref[...] = jnp.zeros(acc_ref.shape, f32)


def _flash_heads(nheads, tq, tk, score_fn, add_fn, value_fn, m_ref, l_ref, acc_ref, s_scr, p_scr, mx_scr, a_scr):
    nkc = tk // LANES
    chunks = [(pl.ds(c * ROW_CHUNK, ROW_CHUNK), c * ROW_CHUNK) for c in range(tq // ROW_CHUNK)]

    def scores(h):
        s_scr[h % 3] = score_fn(h)

    def pass_a(h):
        s_ref = s_scr.at[h % 3]
        for rows, r0 in chunks:
            mx = None
            for kc in range(nkc):
                cols = slice(kc * LANES, (kc + 1) * LANES)
                s = s_ref[rows, cols]
                if add_fn is not None:
                    for term in add_fn(h, rows, r0, kc):
                        s = s + term
                    s_ref[rows, cols] = s
                mx = s if mx is None else jnp.maximum(mx, s)
            mx_scr[h % 2, rows, :] = mx
        m_prev = m_ref[h]
        m_new = jnp.maximum(m_prev, jnp.max(mx_scr[h % 2], axis=-1, keepdims=True))
        a_scr[h % 2] = jnp.exp2(m_prev - m_new)
        m_ref[h] = m_new

    def pass_b(h):
        s_ref = s_scr.at[h % 3]
        for rows, _ in chunks:
            m_rows = m_ref[h, rows, :]
            for kc in range(nkc):
                cols = slice(kc * LANES, (kc + 1) * LANES)
                p_scr[h % 2, rows, cols] = jnp.exp2(s_ref[rows, cols] - m_rows).astype(bf16)
        pv = jnp.dot(p_scr[h % 2], value_fn(h), preferred_element_type=f32)
        alpha = a_scr[h % 2]
        acc_ref[h] = alpha * acc_ref[h] + pv[:, :HEAD_DIM]
        l_ref[h] = alpha * l_ref[h] + pv[:, HEAD_DIM:]

    scores(0)
    for h in range(nheads):
        if h + 1 < nheads:
            scores(h + 1)
        pass_a(h)
        if h >= 1:
            pass_b(h - 1)
    pass_b(nheads - 1)


def _with_ones(v):
    return jnp.concatenate([v, jnp.ones((v.shape[0], LANES), v.dtype)], axis=1)


def _mla_attn_body(qi_tab, ki_tab, fl_tab, q_ref, k_ref, v_ref, o_ref, m_ref, l_ref, acc_ref,
                   s_scr, p_scr, mx_scr, a_scr, mask_scr, *, hb, tq, tk):
    st = pl.program_id(2)
    qi, ki, fl = qi_tab[st], ki_tab[st], fl_tab[st]

    pl.when((fl & FLAG_FIRST) != 0)(lambda: _flash_init(m_ref, l_ref, acc_ref))

    def score(h):
        qk = slice(h * MLA_SLOT, (h + 1) * MLA_SLOT)
        return lax.dot_general(q_ref[:, qk], k_ref[:, qk], NT, preferred_element_type=f32)

    def value(h):
        return _with_ones(v_ref[:, h * HEAD_DIM:(h + 1) * HEAD_DIM])

    def tile(causal):
        add = None
        if causal:
            row = qi * tq + lax.broadcasted_iota(i32, (tq, tk), 0)
            col = ki * tk + lax.broadcasted_iota(i32, (tq, tk), 1)
            mask_scr[...] = jnp.where(row >= col, 0.0, NEG_INF)
            add = lambda h, rows, r0, kc: (mask_scr[rows, kc * LANES:(kc + 1) * LANES],)
        _flash_heads(hb, tq, tk, score, add, value, m_ref, l_ref, acc_ref, s_scr, p_scr, mx_scr, a_scr)

    pl.when((fl & FLAG_CAUSAL) != 0)(lambda: tile(True))
    pl.when((fl & FLAG_CAUSAL) == 0)(lambda: tile(False))

    @pl.when((fl & FLAG_LAST) != 0)
    def _():
        for h in range(hb):
            o_ref[:, h * HEAD_DIM:(h + 1) * HEAD_DIM] = (acc_ref[h] / l_ref[h]).astype(o_ref.dtype)


def _mla_attn(q, k, v, B, T, hb=MLA_HEADS, tq=Q_TILE, tk=K_TILE):
    tk = min(tk, T)
    nq, nk = T // tq, T // tk
    tabs = _tile_schedule(nq, tq, tk, lambda qi: 0)
    grid_spec = pltpu.PrefetchScalarGridSpec(
        num_scalar_prefetch=3,
        grid=(B, MLA_HEADS // hb, int(tabs[0].shape[0])),
        in_specs=[pl.BlockSpec((tq, hb * MLA_SLOT), lambda b, h, s, qt, kt, ft: (b * nq + qt[s], h)),
                  pl.BlockSpec((tk, hb * MLA_SLOT), lambda b, h, s, qt, kt, ft: (b * nk + kt[s], h)),
                  pl.BlockSpec((tk, hb * HEAD_DIM), lambda b, h, s, qt, kt, ft: (b * nk + kt[s], h))],
        out_specs=pl.BlockSpec((tq, hb * HEAD_DIM), lambda b, h, s, qt, kt, ft: (b * nq + qt[s], h)),
        scratch_shapes=_flash_scratch(hb, tq, tk))
    return pl.pallas_call(
        functools.partial(_mla_attn_body, hb=hb, tq=tq, tk=tk),
        grid_spec=grid_spec,
        out_shape=jax.ShapeDtypeStruct((B * T, MLA_HEADS * HEAD_DIM), bf16),
        compiler_params=_params(("parallel", "parallel", "arbitrary")),
        name="mla_flash_attention",
    )(*tabs, q, k, v)


def _bias_table_body(rev_ref, o_ref):
    width = rev_ref.shape[-1]
    x = jnp.broadcast_to(rev_ref[0], (LANES, width))
    rolled = pltpu.roll(x, 0, 1, stride=1, stride_axis=0)
    for kb in range(width // LANES):
        o_ref[0, kb] = rolled[:, kb * LANES:(kb + 1) * LANES]


def _bias_table(rev):
    H, _, U = rev.shape
    return pl.pallas_call(
        _bias_table_body,
        grid=(H,),
        in_specs=[pl.BlockSpec((1, 1, U), lambda h: (h, 0, 0))],
        out_specs=pl.BlockSpec((1, U // LANES, LANES, LANES), lambda h: (h, 0, 0, 0)),
        out_shape=jax.ShapeDtypeStruct((H, U // LANES, LANES, LANES), f32),
        compiler_params=_params(("parallel",)),
        name="relative_bias_toeplitz_table",
    )(rev)


def _nsa_band_body(qi_tab, ki_tab, fl_tab, q_ref, k_ref, v_ref, g_ref, *rest, tq, tk, seq, select, kb_lo):
    if select:
        sel_ref, e_ref, o_ref, m_ref, l_ref, acc_ref, s_scr, p_scr, mx_scr, a_scr, mask_scr = rest
    else:
        oc_ref, os_ref, gate_ref, o_ref, m_ref, l_ref, acc_ref, s_scr, p_scr, mx_scr, a_scr, mask_scr = rest
    st = pl.program_id(2)
    qi, ki, fl = qi_tab[st], ki_tab[st], fl_tab[st]

    pl.when((fl & FLAG_FIRST) != 0)(lambda: _flash_init(m_ref, l_ref, acc_ref))

    d0 = qi * tq - ki * tk
    dist = d0 + lax.broadcasted_iota(i32, (tq, tk), 0) - lax.broadcasted_iota(i32, (tq, tk), 1)
    if select:
        picked = jnp.dot(sel_ref[...], e_ref[...], preferred_element_type=f32)
        valid = (dist >= 0) & (picked > 0.5)
    else:
        valid = (dist >= 0) & (dist < WINDOW)
    mask_scr[...] = jnp.where(valid, 0.0, NEG_INF)
    ng = k_ref.shape[0]
    ks = [k_ref[gi].astype(bf16) for gi in range(ng)]
    vs = [_with_ones(v_ref[gi].astype(bf16)) for gi in range(ng)]
    kb0 = (seq - d0) // LANES - kb_lo

    def score(h):
        return lax.dot_general(q_ref[:, h * HEAD_DIM:(h + 1) * HEAD_DIM], ks[h // NSA_HPG], NT,
                               preferred_element_type=f32)

    def add(h, rows, r0, kc):
        bias = g_ref[h, kb0 - r0 // LANES + kc, pl.ds(r0 % LANES, ROW_CHUNK), :]
        return bias, mask_scr[rows, kc * LANES:(kc + 1) * LANES]

    _flash_heads(ng * NSA_HPG, tq, tk, score, add, lambda h: vs[h // NSA_HPG], m_ref, l_ref, acc_ref,
                 s_scr, p_scr, mx_scr, a_scr)

    @pl.when((fl & FLAG_LAST) != 0)
    def _():
        for h in range(ng * NSA_HPG):
            sl = slice(h * HEAD_DIM, (h + 1) * HEAD_DIM)
            o = acc_ref[h] / l_ref[h]
            if not select:
                gate = gate_ref[...]
                g0 = (h // NSA_HPG) * LANES + h % NSA_HPG
                o = (gate[:, g0:g0 + 1] * oc_ref[:, sl].astype(f32)
                     + gate[:, g0 + NSA_HPG:g0 + NSA_HPG + 1] * os_ref[:, sl].astype(f32)
                     + gate[:, g0 + 2 * NSA_HPG:g0 + 2 * NSA_HPG + 1] * o)
            o_ref[:, sl] = o.astype(o_ref.dtype)


def _nsa_band(q, kv, gtab, B, T, kcol, vcol, tk, sel=None, emat=None, merge=None, tq=Q_TILE):
    select = sel is not None
    nq, nk = T // tq, T // tk
    first = (lambda qi: 0) if select else (lambda qi: max(qi * tq - (WINDOW - 1), 0) // tk)
    tabs = _tile_schedule(nq, tq, tk, first)
    if select:
        ng, kb_lo = 1, 0
    else:
        ng = NSA_GROUPS
        kb_lo = (T - (WINDOW + tk - 1) // tk * tk) // LANES - tq // LANES
        gtab = gtab[:, kb_lo:T // LANES + tk // LANES]
    gw = ng * NSA_HPG * HEAD_DIM
    in_specs = [
        pl.BlockSpec((tq, gw), lambda b, g, s, qt, kt, ft: (b * nq + qt[s], g)),
        pl.BlockSpec((ng, tk, HEAD_DIM), lambda b, g, s, qt, kt, ft: ((kcol * NSA_GROUPS) // ng + g, b * nk + kt[s], 0)),
        pl.BlockSpec((ng, tk, HEAD_DIM), lambda b, g, s, qt, kt, ft: ((vcol * NSA_GROUPS) // ng + g, b * nk + kt[s], 0)),
        pl.BlockSpec((ng * NSA_HPG,) + gtab.shape[1:], lambda b, g, s, qt, kt, ft: (g, 0, 0, 0),
                     pipeline_mode=pl.Buffered(1)),
    ]
    args = [q, kv, kv, gtab]
    if select:
        in_specs += [pl.BlockSpec((tq, LANES), lambda b, g, s, qt, kt, ft: (b * nq + qt[s], g)),
                     pl.BlockSpec((LANES, tk), lambda b, g, s, qt, kt, ft: (0, kt[s]))]
        args += [sel, emat]
    else:
        o_other = pl.BlockSpec((tq, gw), lambda b, g, s, qt, kt, ft: (b * nq + qt[s], g))
        in_specs += [o_other, o_other,
                     pl.BlockSpec((tq, ng * LANES), lambda b, g, s, qt, kt, ft: (b * nq + qt[s], g))]
        args += list(merge)
    grid_spec = pltpu.PrefetchScalarGridSpec(
        num_scalar_prefetch=3,
        grid=(B, NSA_GROUPS // ng, int(tabs[0].shape[0])),
        in_specs=in_specs,
        out_specs=pl.BlockSpec((tq, gw), lambda b, g, s, qt, kt, ft: (b * nq + qt[s], g)),
        scratch_shapes=_flash_scratch(ng * NSA_HPG, tq, tk))
    return pl.pallas_call(
        functools.partial(_nsa_band_body, tq=tq, tk=tk, seq=T, select=select, kb_lo=kb_lo),
        grid_spec=grid_spec,
        out_shape=jax.ShapeDtypeStruct((B * T, NSA_HEADS * HEAD_DIM), bf16),
        compiler_params=_params(("parallel", "parallel", "arbitrary")),
        name="nsa_selected_attention" if select else "nsa_window_attention",
    )(*tabs, *args)


def _compress_body(x_ref, pe_ref, w1_ref, w2_ref, o_ref, ot_ref, *, nc):
    half = CMP_BLOCK // 2
    h_lo = jnp.zeros((nc, CMP_HIDDEN), f32)
    h_hi = jnp.zeros((nc, CMP_HIDDEN), f32)
    for p in range(half):
        rows = x_ref[0, pl.ds(p, nc, stride=CMP_STRIDE), :]
        h_lo += jnp.dot((rows + pe_ref[0, p:p + 1, :]).astype(bf16), w1_ref[0, p], preferred_element_type=f32)
        h_hi += jnp.dot((rows + pe_ref[0, half + p:half + p + 1, :]).astype(bf16), w1_ref[0, half + p],
                        preferred_element_type=f32)
    hid = h_lo + pltpu.roll(h_hi, nc - 1, 0)
    out = jnp.dot(jax.nn.gelu(hid).astype(bf16), w2_ref[0], preferred_element_type=f32)
    out = jnp.where(lax.broadcasted_iota(i32, out.shape, 0) < nc - 1, out, 0.0)
    o_ref[0, 0, 0] = out.astype(bf16)
    ot_ref[0, 0, 0] = out.T.astype(bf16)


def _compress(kv, pe, w1, w2, B, T):
    nc = T // CMP_STRIDE
    return pl.pallas_call(
        functools.partial(_compress_body, nc=nc),
        grid=(2, B, NSA_GROUPS),
        in_specs=[pl.BlockSpec((1, T, HEAD_DIM), lambda c, b, g: (c * NSA_GROUPS + g, b, 0)),
                  pl.BlockSpec((1, CMP_BLOCK, HEAD_DIM), lambda c, b, g: (c, 0, 0)),
                  pl.BlockSpec((1, CMP_BLOCK, HEAD_DIM, CMP_HIDDEN), lambda c, b, g: (c, 0, 0, 0)),
                  pl.BlockSpec((1, CMP_HIDDEN, HEAD_DIM), lambda c, b, g: (c, 0, 0))],
        out_specs=[pl.BlockSpec((1, 1, 1, nc, HEAD_DIM), lambda c, b, g: (c, b, g, 0, 0)),
                   pl.BlockSpec((1, 1, 1, HEAD_DIM, nc), lambda c, b, g: (c, b, g, 0, 0))],
        out_shape=[jax.ShapeDtypeStruct((2, B, NSA_GROUPS, nc, HEAD_DIM), bf16),
                   jax.ShapeDtypeStruct((2, B, NSA_GROUPS, HEAD_DIM, nc), bf16)],
        compiler_params=_params(("parallel", "parallel", "parallel")),
        name="nsa_compress_mlp",
    )(kv, pe, w1, w2)


def _cmp_body(q_ref, kc_ref, vct_ref, cw_ref, ovt_ref, o_ref, sel_ref, *, nc, nq):
    tq = CMP_TQ
    qi = pl.program_id(2)
    t0 = qi * tq
    key_row = lax.broadcasted_iota(i32, (nc, tq), 0)
    q_col = lax.broadcasted_iota(i32, (nc, tq), 1)
    valid = (t0 + q_col - CMP_STRIDE * key_row - (CMP_BLOCK - 1)) >= 0
    kc = kc_ref[0, 0, 0]
    vct = vct_ref[0, 0, 0]
    psum = jnp.zeros((nc, tq), f32)
    for h in range(NSA_HPG):
        bias = jnp.concatenate(
            [jnp.concatenate([cw_ref[h, qi * (tq // LANES) + b - c + nq] for b in range(tq // LANES)], axis=1)
             for c in range(nc // 8)], axis=0)
        s = lax.dot_general(kc, q_ref[:, h * HEAD_DIM:(h + 1) * HEAD_DIM], NT, preferred_element_type=f32)
        s = jnp.where(valid, s + bias, NEG_INF)
        m = jnp.max(s, axis=0, keepdims=True)
        e = jnp.where(valid, jnp.exp2(s - m), 0.0)
        l = jnp.sum(e, axis=0, keepdims=True)
        p = e / jnp.where(l > 0.0, l, 1.0)
        psum += p
        o_t = jnp.dot(vct, p.astype(bf16), preferred_element_type=f32)
        o_ref[:, h * HEAD_DIM:(h + 1) * HEAD_DIM] = o_t.T.astype(o_ref.dtype)

    imp = jnp.dot(ovt_ref[...], psum, preferred_element_type=f32, precision=lax.Precision.HIGHEST)
    nrow = imp.shape[0]
    n_slc = (nc * CMP_STRIDE) // SLC_BLOCK
    j = lax.broadcasted_iota(i32, (nrow, tq), 0)
    t = t0 + lax.broadcasted_iota(i32, (nrow, tq), 1)
    cur = jnp.right_shift(t, 6)
    ok = j * SLC_BLOCK <= t
    forced = (j == 0) | (j == cur) | (j == cur - 1)
    score = jnp.where(forced, FORCED_SCORE, jnp.where(ok, imp, NEG_INF))
    score = jnp.where(j < n_slc, score, PAD_SCORE)
    chosen = jnp.zeros((nrow, tq), f32)
    row_id = j.astype(f32)
    for _ in range(min(SLC_TOPK, n_slc)):
        top = jnp.max(score, axis=0, keepdims=True)
        first = jnp.min(jnp.where(score == top, row_id, PAD_ID), axis=0, keepdims=True)
        hit = row_id == first
        chosen = jnp.where(hit, 1.0, chosen)
        score = jnp.where(hit, PAD_SCORE, score)
    chosen = jnp.where(ok & (j < n_slc), chosen, 0.0)
    sel_ref[...] = chosen.T.astype(sel_ref.dtype)


def _nsa_cmp(q, kc, vct, cw, ovt, B, T):
    nc = T // CMP_STRIDE
    nq = T // CMP_TQ
    gw = NSA_HPG * HEAD_DIM
    return pl.pallas_call(
        functools.partial(_cmp_body, nc=nc, nq=T // LANES),
        grid=(B, NSA_GROUPS, nq),
        in_specs=[pl.BlockSpec((CMP_TQ, gw), lambda b, g, qi: (b * nq + qi, g)),
                  pl.BlockSpec((1, 1, 1, nc, HEAD_DIM), lambda b, g, qi: (0, b, g, 0, 0)),
                  pl.BlockSpec((1, 1, 1, HEAD_DIM, nc), lambda b, g, qi: (1, b, g, 0, 0)),
                  pl.BlockSpec((NSA_HPG,) + cw.shape[1:], lambda b, g, qi: (g, 0, 0, 0)),
                  pl.BlockSpec(ovt.shape, lambda b, g, qi: (0, 0))],
        out_specs=[pl.BlockSpec((CMP_TQ, gw), lambda b, g, qi: (b * nq + qi, g)),
                   pl.BlockSpec((CMP_TQ, LANES), lambda b, g, qi: (b * nq + qi, g))],
        out_shape=[jax.ShapeDtypeStruct((B * T, NSA_HEADS * HEAD_DIM), bf16),
                   jax.ShapeDtypeStruct((B * T, NSA_GROUPS * LANES), bf16)],
        compiler_params=_params(("parallel", "parallel", "parallel")),
        name="nsa_compressed_attention_select",
    )(q, kc, vct, cw, ovt)


def _layer_norm(y, g, b):
    mu = jnp.mean(y, axis=-1, keepdims=True)
    var = jnp.mean(jnp.square(y - mu), axis=-1, keepdims=True)
    return (y - mu) * lax.rsqrt(var + LN_EPS) * g + b


def _branch_merge_body(a_ref, b_ref, wa_ref, wb_ref, ga_ref, gb_ref, o_ref):
    ya = jnp.dot(a_ref[...], wa_ref[...], preferred_element_type=f32)
    yb = jnp.dot(b_ref[...], wb_ref[...], preferred_element_type=f32)
    o_ref[...] = (ga_ref[...] * ya + gb_ref[...] * yb).astype(o_ref.dtype)


def _branch_merge(a, b, wa, wb, mg, tm, tn):
    M, K = a.shape
    Nc = wa.shape[1]
    nj = Nc // tn
    return pl.pallas_call(
        _branch_merge_body,
        grid=(M // tm, nj),
        in_specs=[pl.BlockSpec((tm, K), lambda i, j: (i, 0)), pl.BlockSpec((tm, K), lambda i, j: (i, 0)),
                  pl.BlockSpec((K, tn), lambda i, j: (0, j)), pl.BlockSpec((K, tn), lambda i, j: (0, j)),
                  pl.BlockSpec((tm, tn), lambda i, j: (i, j)), pl.BlockSpec((tm, tn), lambda i, j: (i, nj + j))],
        out_specs=pl.BlockSpec((tm, tn), lambda i, j: (i, j)),
        out_shape=jax.ShapeDtypeStruct((M, Nc), bf16),
        compiler_params=_params(("parallel", "parallel")),
        name="branch_proj_merge",
    )(a, b, wa, wb, mg, mg)


def _proj_ln_body(z_ref, w_ref, x_ref, g_ref, b_ref, o_ref):
    y = jnp.dot(z_ref[...], w_ref[...], preferred_element_type=f32)
    o_ref[...] = _layer_norm(ALPHA * x_ref[...] + y, g_ref[...], b_ref[...])


def _proj_ln(z, w, x, g, b, tm):
    M, K = z.shape
    D = w.shape[1]
    vec = pl.BlockSpec((1, D), lambda i: (0, 0))
    return pl.pallas_call(
        _proj_ln_body,
        grid=(M // tm,),
        in_specs=[pl.BlockSpec((tm, K), lambda i: (i, 0)), pl.BlockSpec((K, D), lambda i: (0, 0)),
                  pl.BlockSpec((tm, D), lambda i: (i, 0)), vec, vec],
        out_specs=pl.BlockSpec((tm, D), lambda i: (i, 0)),
        out_shape=jax.ShapeDtypeStruct((M, D), f32),
        compiler_params=_params(("parallel",)),
        name="mix_out_proj_layernorm",
    )(z, w, x, g, b)


def _mem_attn_body(x_ref, wq_ref, k_ref, v_ref, wo_ref, g_ref, b_ref, o_ref, ob_ref):
    x = x_ref[...]
    q = (jnp.dot(x.astype(bf16), wq_ref[...], preferred_element_type=f32) * HEAD_DIM ** -0.5).astype(bf16)
    heads = []
    for h in range(MEM_HEADS):
        sl = slice(h * HEAD_DIM, (h + 1) * HEAD_DIM)
        s = lax.dot_general(q[:, sl], k_ref[0, :, sl], NT, preferred_element_type=f32)
        e = jnp.exp(s - jnp.max(s, axis=-1, keepdims=True))
        p = e / jnp.sum(e, axis=-1, keepdims=True)
        heads.append(jnp.dot(p.astype(bf16), v_ref[0, :, sl], preferred_element_type=f32).astype(bf16))
    y = jnp.dot(jnp.concatenate(heads, axis=1), wo_ref[...], preferred_element_type=f32)
    out = _layer_norm(ALPHA * x + y, g_ref[...], b_ref[...])
    o_ref[...] = out
    ob_ref[...] = out.astype(bf16)


def _mem_attn(x, wq, km, vm, wo, g, b, T, tm):
    M, D = x.shape
    nt = T // tm
    W = wq.shape[1]
    vec = pl.BlockSpec((1, D), lambda i: (0, 0))
    blk = pl.BlockSpec((tm, D), lambda i: (i, 0))
    return pl.pallas_call(
        _mem_attn_body,
        grid=(M // tm,),
        in_specs=[blk, pl.BlockSpec((D, W), lambda i: (0, 0)),
                  pl.BlockSpec((1, N_MEM, W), lambda i: (i // nt, 0, 0)),
                  pl.BlockSpec((1, N_MEM, W), lambda i: (i // nt, 0, 0)),
                  pl.BlockSpec((W, D), lambda i: (0, 0)), vec, vec],
        out_specs=[blk, blk],
        out_shape=[jax.ShapeDtypeStruct((M, D), f32), jax.ShapeDtypeStruct((M, D), bf16)],
        compiler_params=_params(("parallel",)),
        name="memory_cross_attention_layernorm",
    )(x, wq, km, vm, wo, g, b)


def _add_ln_body(x_ref, y_ref, g_ref, b_ref, o_ref):
    o_ref[...] = _layer_norm(ALPHA * x_ref[...] + y_ref[...], g_ref[...], b_ref[...])


def _add_ln(x, y, g, b, tm):
    M, D = x.shape
    vec = pl.BlockSpec((1, D), lambda i: (0, 0))
    blk = pl.BlockSpec((tm, D), lambda i: (i, 0))
    return pl.pallas_call(
        _add_ln_body,
        grid=(M // tm,),
        in_specs=[blk, blk, vec, vec],
        out_specs=blk,
        out_shape=jax.ShapeDtypeStruct((M, D), f32),
        compiler_params=_params(("parallel",)),
        name="residual_layernorm",
    )(x, y, g, b)


def _topk_rows(s, ids, k):
    n = s.shape[1]
    slot = lax.broadcasted_iota(i32, (k, n), 0)
    vals = jnp.zeros((k, n), f32)
    inds = jnp.zeros((k, n), f32)
    for r in range(k):
        top = jnp.max(s, axis=0, keepdims=True)
        first = jnp.min(jnp.where(s == top, ids, PAD_ID), axis=0, keepdims=True)
        vals = jnp.where(slot == r, top, vals)
        inds = jnp.where(slot == r, first, inds)
        s = jnp.where(ids == first, PAD_SCORE, s)
    return vals, inds


def _pick_rows(table, sel):
    out = jnp.zeros(sel.shape, table.dtype)
    for a in range(table.shape[0]):
        out = jnp.where(sel == a, table[a:a + 1, :], out)
    return out


CAND_COUNTS = tuple(PEER_TOPK // (a + 1) for a in range(PEER_TOPK))
CAND_ROWS = -(-sum(CAND_COUNTS) // 8) * 8


def _cand_ids(tn):
    ids = [a * PEER_TOPK + b for a, nb in enumerate(CAND_COUNTS) for b in range(nb)]
    ids = np.array(ids + [PAD_ID] * (CAND_ROWS - len(ids)), np.float32)
    return jnp.asarray(np.broadcast_to(ids[:, None], (CAND_ROWS, tn)))


def _peer_route_body(q_ref, keys_ref, ids_ref, e1_ref, e2_ref, gt_ref, e1_scr, e2_scr, g_scr, cand_scr):
    k1 = keys_ref[0].astype(bf16)
    k2 = keys_ref[1].astype(bf16)
    tn = q_ref.shape[0]
    key_ids = lax.broadcasted_iota(i32, (PEER_N_KEYS, tn), 0).astype(f32)
    cand_scr[...] = jnp.full(cand_scr.shape, PAD_SCORE, f32)
    for h in range(PEER_HEADS):
        q1 = q_ref[:, (2 * h) * LANES:(2 * h + 1) * LANES]
        q2 = q_ref[:, (2 * h + 1) * LANES:(2 * h + 2) * LANES]
        s1 = lax.dot_general(k1, q1, NT, preferred_element_type=f32)
        s2 = lax.dot_general(k2, q2, NT, preferred_element_type=f32)
        v1, i1 = _topk_rows(s1, key_ids, PEER_TOPK)
        v2, i2 = _topk_rows(s2, key_ids, PEER_TOPK)
        off = 0
        for a, nb in enumerate(CAND_COUNTS):
            cand_scr[off:off + nb, :] = v1[a:a + 1, :] + v2[0:nb, :]
            off += nb
        top_s, flat = _topk_rows(cand_scr[...], ids_ref[...], PEER_TOPK)
        flat = flat.astype(i32)
        e = jnp.exp(top_s - top_s[0:1, :])
        rows = slice(h * PEER_TOPK, (h + 1) * PEER_TOPK)
        g_scr[rows, :] = e / jnp.sum(e, axis=0, keepdims=True)
        e1_scr[rows, :] = _pick_rows(i1, jnp.right_shift(flat, 4)).astype(i32)
        e2_scr[rows, :] = _pick_rows(i2, jnp.bitwise_and(flat, PEER_TOPK - 1)).astype(i32)
    e1_ref[...] = e1_scr[...].T
    e2_ref[...] = e2_scr[...].T
    gt_ref[...] = g_scr[...].T


def _peer_route(q, keys, tn=LANES):
    M, W = q.shape
    out = pl.BlockSpec((tn, PEER_HK), lambda i: (i, 0))
    return pl.pallas_call(
        _peer_route_body,
        grid=(M // tn,),
        in_specs=[pl.BlockSpec((tn, W), lambda i: (i, 0)), pl.BlockSpec(keys.shape, lambda i: (0, 0, 0)),
                  pl.BlockSpec((CAND_ROWS, tn), lambda i: (0, 0))],
        out_specs=[out, out, out],
        out_shape=[jax.ShapeDtypeStruct((M, PEER_HK), i32), jax.ShapeDtypeStruct((M, PEER_HK), i32),
                   jax.ShapeDtypeStruct((M, PEER_HK), f32)],
        scratch_shapes=[pltpu.VMEM((PEER_HK, tn), i32), pltpu.VMEM((PEER_HK, tn), i32), pltpu.VMEM((PEER_HK, tn), f32),
                        pltpu.VMEM((CAND_ROWS, tn), f32)],
        compiler_params=_params(("parallel",)),
        name="peer_product_key_routing",
    )(q, keys, _cand_ids(tn))


TOKEN_GROUP = 16


def _peer_weights_body(e1_ref, e2_ref, g_ref, w_ref, *, tn):
    sub = lax.broadcasted_iota(i32, (PEER_N_KEYS, PEER_HK), 0)

    def one(n):
        r1 = e1_ref[pl.ds(n, 1), :]
        r2 = e2_ref[pl.ds(n, 1), :]
        g = g_ref[pl.ds(n, 1), :]
        a_t = (sub == r1).astype(bf16)
        b_t = jnp.where(sub == r2, g, 0.0).astype(bf16)
        return lax.dot_general(a_t, b_t, NT, preferred_element_type=f32)

    def group(i, carry):
        n0 = pl.multiple_of(i * TOKEN_GROUP, TOKEN_GROUP)
        tables = jnp.stack([one(n0 + t) for t in range(TOKEN_GROUP)], axis=0)
        w_ref[:, pl.ds(n0, TOKEN_GROUP), :] = jnp.swapaxes(tables, 0, 1).astype(w_ref.dtype)
        return carry

    lax.fori_loop(0, tn // TOKEN_GROUP, group, 0)


def _peer_weights(e1, e2, g, tn=256):
    M = e1.shape[0]
    blk = pl.BlockSpec((tn, PEER_HK), lambda i: (i, 0))
    return pl.pallas_call(
        functools.partial(_peer_weights_body, tn=tn),
        grid=(M // tn,),
        in_specs=[blk, blk, blk],
        out_specs=pl.BlockSpec((PEER_N_KEYS, tn, PEER_N_KEYS), lambda i: (0, i, 0)),
        out_shape=jax.ShapeDtypeStruct((PEER_N_KEYS, M, PEER_N_KEYS), bf16),
        compiler_params=_params(("parallel",)),
        name="peer_routing_weights",
    )(e1, e2, g)


GELU_SLOPE = 2.0 * math.sqrt(2.0 / math.pi)


def _gelu_tanh(x):
    minus_2z = x * ((-GELU_SLOPE * 0.044715 * LOG2E) * (x * x) - GELU_SLOPE * LOG2E)
    return x / (1.0 + jnp.exp2(minus_2z))


def _peer_dense_body(x_ref, w_ref, u_ref, v_ref, o_ref, *, rows):
    @pl.when(pl.program_id(1) == 0)
    def _():
        o_ref[...] = jnp.zeros(o_ref.shape, f32)

    act = _gelu_tanh(lax.dot_general(x_ref[...], u_ref[...].astype(bf16), NT, preferred_element_type=f32))
    w = jnp.concatenate([w_ref[r] for r in range(rows)], axis=1)
    o_ref[...] += jnp.dot(w * act.astype(bf16), v_ref[...].astype(bf16), preferred_element_type=f32)


def _peer_dense(xb, w, u, v, tm=1024, rows=8):
    M, D = xb.shape
    te = rows * PEER_N_KEYS
    return pl.pallas_call(
        functools.partial(_peer_dense_body, rows=rows),
        grid=(M // tm, PEER_N_KEYS // rows),
        in_specs=[pl.BlockSpec((tm, D), lambda i, e: (i, 0)),
                  pl.BlockSpec((rows, tm, PEER_N_KEYS), lambda i, e: (e, i, 0)),
                  pl.BlockSpec((te, D), lambda i, e: (e, 0)), pl.BlockSpec((te, D), lambda i, e: (e, 0))],
        out_specs=pl.BlockSpec((tm, D), lambda i, e: (i, 0)),
        out_shape=jax.ShapeDtypeStruct((M, D), f32),
        compiler_params=_params(("parallel", "arbitrary")),
        name="peer_dense_experts",
    )(xb, w, u, v)


def _t5_bucket(dist):
    dist = jnp.maximum(dist, 0)
    n_log = REL_BUCKETS - REL_MAX_EXACT
    large = REL_MAX_EXACT + (jnp.log(jnp.maximum(dist, 1).astype(f32) / REL_MAX_EXACT)
                             / math.log(REL_MAX_DIST / REL_MAX_EXACT) * n_log).astype(i32)
    large = jnp.minimum(large, REL_BUCKETS - 1)
    return jnp.where(dist < REL_MAX_EXACT, dist, large)


def _position_tables(rel_bias, T):
    by_dist = (rel_bias[_t5_bucket(jnp.arange(T))] * LOG2E).T.astype(f32)
    rev = jnp.pad(by_dist[:, ::-1], ((0, 0), (1, K_TILE - 1)), mode="edge")[:, None, :]
    shift = (CMP_BLOCK - 1) + CMP_STRIDE * 7
    ext = jnp.pad(by_dist, ((0, 0), (T + shift, 0)), mode="edge")
    rows = [ext[:, CMP_STRIDE * (7 - r):CMP_STRIDE * (7 - r) + 2 * T] for r in range(8)]
    cmpw = jnp.stack(rows, axis=1).reshape(NSA_HEADS, 8, 2 * T // LANES, LANES)
    return rev, jnp.transpose(cmpw, (0, 2, 1, 3))


def _rope_slot_tables(T):
    inv = ROPE_THETA ** (-jnp.arange(0, MLA_ROPE, 2, dtype=f32) / MLA_ROPE)
    ang = jnp.arange(T, dtype=f32)[:, None] * inv[None, :]
    cos, sin = jnp.cos(ang), jnp.sin(ang)
    z = jnp.zeros_like(cos)
    ta = jnp.concatenate([cos, cos, z, z], axis=1)
    tb = jnp.concatenate([-sin, z, z, z], axis=1)
    tc = jnp.concatenate([z, sin, z, z], axis=1)
    return ta, tb, tc


def _layer(x, mem, rel_bias, w_in, cmp_k_pe, cmp_k_w1, cmp_k_w2, cmp_v_pe, cmp_v_w1, cmp_v_w2,
           mla_kv_norm, mla_w_uk, mla_w_uv, w_branch_nsa, w_branch_mla, w_mix_out, ln_mix_g, ln_mix_b,
           mem_w_q, mem_w_k, mem_w_v, mem_w_o, ln_mem_g, ln_mem_b,
           peer_w_query, peer_sub_keys, peer_u, peer_v, ln_ffn_g, ln_ffn_b, ropes):
    B, T, D = x.shape
    N = B * T
    x2d = x.reshape(N, D)
    xb = x2d.astype(bf16)
    row = lambda a: a.reshape(1, -1)

    gate_cols = np.array([[br * NSA_HEADS + g * NSA_HPG + h for br in range(3) for h in range(NSA_HPG)]
                          for g in range(NSA_GROUPS)])
    w_gate = jnp.pad(w_in[:, OFF_NSA_GATE:OFF_MLA_Q][:, gate_cols],
                     ((0, 0), (0, 0), (0, LANES - 3 * NSA_HPG))).reshape(D, NSA_GROUPS * LANES)
    w_mla_q = jnp.pad(w_in[:, OFF_MLA_Q:OFF_MLA_KV].reshape(D, MLA_HEADS, MLA_NOPE + MLA_ROPE),
                      ((0, 0), (0, 0), (0, MLA_SLOT - MLA_NOPE - MLA_ROPE))).reshape(D, MLA_HEADS * MLA_SLOT)
    w_mla_kv = jnp.pad(w_in[:, OFF_MLA_KV:OFF_MERGE], ((0, 0), (0, LANES - MLA_ROPE)))
    segs = [w_mla_kv, jnp.zeros((D, LANES), f32), w_gate,
            w_in[:, :OFF_NSA_KV], w_mla_q, w_in[:, OFF_MERGE:], w_in[:, OFF_NSA_KV:OFF_NSA_GATE]]
    starts = np.concatenate([[0], np.cumsum([s.shape[1] for s in segs])])
    col = lambda i: (int(starts[i]), int(segs[i].shape[1]))
    w_all = jnp.concatenate(segs, axis=1).astype(bf16)

    mla_ckv = _mm(xb, w_all, f32, 1024, MLA_KV_RANK + LANES, name="mla_kv_proj", cols=col(0))
    gates = _mm(xb, w_all, f32, 1024, NSA_GROUPS * LANES, act="sigmoid", name="nsa_gate_proj", cols=col(2))
    nsa_q = _mm(xb, w_all, bf16, 1024, 1024, scale=HEAD_DIM ** -0.5 * LOG2E, name="nsa_q_proj", cols=col(3))
    mla_q = _mla_q(xb, w_all, col(4), ropes, T, 1024, 1024)
    merge = _mm(xb, w_all, f32, 1024, 1024, act="sigmoid", name="merge_gate_proj", cols=col(5))
    nsa_kv = _mm_slabs(xb, w_all, col(6), 1024, 512, "nsa_kv_proj")

    slc_tk = min(K_TILE, T)
    rev_tab, cmp_tab = _position_tables(rel_bias, T)
    bias_tab = _bias_table(rev_tab)
    pe = jnp.stack([cmp_k_pe, cmp_v_pe])
    w1 = jnp.stack([cmp_k_w1, cmp_v_w1]).reshape(2, CMP_BLOCK, HEAD_DIM, CMP_HIDDEN).astype(bf16)
    w2 = jnp.stack([cmp_k_w2, cmp_v_w2]).astype(bf16)
    kvc, kvc_t = _compress(nsa_kv, pe, w1, w2, B, T)
    nc = T // CMP_STRIDE
    n_slc = T // SLC_BLOCK
    cs = np.arange(nc)[None, :] * CMP_STRIDE
    ss = np.arange(LANES)[:, None] * SLC_BLOCK
    ovt = ((cs < ss + SLC_BLOCK) & (cs + CMP_BLOCK > ss) & (np.arange(nc)[None, :] < nc - 1)
           & (np.arange(LANES)[:, None] < n_slc)).astype(np.float32)
    o_cmp, sel = _nsa_cmp(nsa_q, kvc, kvc_t, cmp_tab, jnp.asarray(ovt), B, T)
    emat = (np.arange(T)[None, :] // SLC_BLOCK == np.arange(LANES)[:, None]).astype(np.float32)
    o_slc = _nsa_band(nsa_q, nsa_kv, bias_tab, B, T, 2, 3, slc_tk, sel=sel, emat=jnp.asarray(emat, dtype=bf16))
    o_nsa = _nsa_band(nsa_q, nsa_kv, bias_tab, B, T, 4, 5, WIN_TILE, merge=(o_cmp, o_slc, gates), tq=WIN_TILE)

    mla_k, mla_v = _mla_kv(mla_ckv, row(mla_kv_norm),
                           jnp.asarray(mla_w_uk, bf16), jnp.asarray(mla_w_uv, bf16), ropes, T, 512)
    o_mla = _mla_attn(mla_q, mla_k, mla_v, B, T)

    z = _branch_merge(o_nsa, o_mla, w_branch_nsa.astype(bf16), w_branch_mla.astype(bf16), merge, 1024, 512)
    x1 = _proj_ln(z, w_mix_out.astype(bf16), x2d, row(ln_mix_g), row(ln_mix_b), 512)

    mem2d = mem.reshape(B * N_MEM, D).astype(bf16)
    km = _mm(mem2d, mem_w_k.astype(bf16), bf16, B * N_MEM, 256, name="mem_k_proj").reshape(B, N_MEM, -1)
    vm = _mm(mem2d, mem_w_v.astype(bf16), bf16, B * N_MEM, 256, name="mem_v_proj").reshape(B, N_MEM, -1)
    x2, x2b = _mem_attn(x1, mem_w_q.astype(bf16), km, vm, mem_w_o.astype(bf16), row(ln_mem_g), row(ln_mem_b), T, 512)

    pq = _mm(x2b, peer_w_query.astype(bf16), bf16, 1024, 512, name="peer_query_proj")
    e1, e2, gt = _peer_route(pq, peer_sub_keys)
    wts = _peer_weights(e1, e2, gt)
    y = _peer_dense(x2b, wts, peer_u.astype(bf16), peer_v.astype(bf16))
    x3 = _add_ln(x2, y, row(ln_ffn_g), row(ln_ffn_b), 512)
    return x3.reshape(B, T, D)


def kernel(x, mem, rel_bias, w_in, cmp_k_pe, cmp_k_w1, cmp_k_w2, cmp_v_pe, cmp_v_w1, cmp_v_w2, mla_kv_norm, mla_w_uk, mla_w_uv, w_branch_nsa, w_branch_mla, w_mix_out, ln_mix_g, ln_mix_b, mem_w_q, mem_w_k, mem_w_v, mem_w_o, ln_mem_g, ln_mem_b, peer_w_query, peer_sub_keys, peer_u, peer_v, ln_ffn_g, ln_ffn_b):
    ropes = _rope_slot_tables(x.shape[1])
    for l in range(DEPTH):
        x = _layer(x, mem, rel_bias, w_in[l], cmp_k_pe[l], cmp_k_w1[l], cmp_k_w2[l], cmp_v_pe[l], cmp_v_w1[l],
                   cmp_v_w2[l], mla_kv_norm[l], mla_w_uk[l], mla_w_uv[l], w_branch_nsa[l], w_branch_mla[l],
                   w_mix_out[l], ln_mix_g[l], ln_mix_b[l], mem_w_q[l], mem_w_k[l], mem_w_v[l], mem_w_o[l],
                   ln_mem_g[l], ln_mem_b[l], peer_w_query[l], peer_sub_keys[l], peer_u[l], peer_v[l],
                   ln_ffn_g[l], ln_ffn_b[l], ropes)
    return x
```

```python
import functools
import math

import numpy as np
import jax
import jax.numpy as jnp
from jax import lax
from jax.experimental import pallas as pl
from jax.experimental.pallas import tpu as pltpu

f32 = jnp.float32
bf16 = jnp.bfloat16
i32 = jnp.int32

D_MODEL = 2048
N_MEM = 256
NSA_HEADS = 16
NSA_GROUPS = 2
NSA_HPG = NSA_HEADS // NSA_GROUPS
HEAD_DIM = 128
CMP_BLOCK = 32
CMP_STRIDE = 16
CMP_HIDDEN = 256
SLC_BLOCK = 64
SLC_TOPK = 16
WINDOW = 512
FORCED_SCORE = 1e4
MLA_HEADS = 16
MLA_NOPE = 128
MLA_ROPE = 64
MLA_KV_RANK = 512
MLA_SLOT = 256
ROPE_THETA = 10000.0
REL_BUCKETS = 32
REL_MAX_EXACT = 16
REL_MAX_DIST = 4096
MEM_HEADS = 4
PEER_HEADS = 8
PEER_N_KEYS = 128
PEER_TOPK = 16
PEER_HK = PEER_HEADS * PEER_TOPK
DEPTH = 1
ALPHA = (2.0 * DEPTH) ** 0.25
NEG_INF = -1e30
M_INIT = -1e29
PAD_SCORE = -3e38
PAD_ID = 1e9
LOG2E = math.log2(math.e)
LN_EPS = 1e-5
RMS_EPS = 1e-6

NSA_Q_COLS = NSA_HEADS * HEAD_DIM
NSA_KV_COLS = NSA_GROUPS * HEAD_DIM
OFF_NSA_KV = NSA_Q_COLS
OFF_NSA_GATE = OFF_NSA_KV + 6 * NSA_KV_COLS
OFF_MLA_Q = OFF_NSA_GATE + 3 * NSA_HEADS
OFF_MLA_KV = OFF_MLA_Q + MLA_HEADS * (MLA_NOPE + MLA_ROPE)
OFF_MERGE = OFF_MLA_KV + MLA_KV_RANK + MLA_ROPE

LANES = 128
Q_TILE = 512
K_TILE = 512
WIN_TILE = 256
ROW_CHUNK = 32
CMP_TQ = 128
VMEM_LIMIT = 56 * 1024 * 1024

NT = (((1,), (1,)), ((), ()))


def _params(sem):
    return pltpu.CompilerParams(dimension_semantics=sem, vmem_limit_bytes=VMEM_LIMIT)


def _mm_body(a_ref, b_ref, o_ref, *, scale, act):
    acc = jnp.dot(a_ref[...].astype(bf16), b_ref[...], preferred_element_type=f32)
    if scale != 1.0:
        acc = acc * scale
    if act == "sigmoid":
        acc = jax.nn.sigmoid(acc)
    o_ref[...] = acc.astype(o_ref.dtype)


def _mm(a, b, out_dtype, tm, tn, scale=1.0, act=None, name="mm", cols=None):
    M, K = a.shape
    c0, Nc = (0, b.shape[1]) if cols is None else cols
    assert c0 % tn == 0 and Nc % tn == 0
    j0 = c0 // tn
    return pl.pallas_call(
        functools.partial(_mm_body, scale=scale, act=act),
        grid=(M // tm, Nc // tn),
        in_specs=[pl.BlockSpec((tm, K), lambda i, j: (i, 0)), pl.BlockSpec((K, tn), lambda i, j: (0, j0 + j))],
        out_specs=pl.BlockSpec((tm, tn), lambda i, j: (i, j)),
        out_shape=jax.ShapeDtypeStruct((M, Nc), out_dtype),
        compiler_params=_params(("parallel", "parallel")),
        name=name,
    )(a, b)


def _mm_slabs_body(a_ref, b_ref, o_ref):
    acc = jnp.dot(a_ref[...], b_ref[...], preferred_element_type=f32)
    for c in range(o_ref.shape[0]):
        o_ref[c] = acc[:, c * LANES:(c + 1) * LANES]


def _mm_slabs(a, b, cols, tm, tn, name):
    M, K = a.shape
    c0, Nc = cols
    assert c0 % tn == 0 and Nc % tn == 0
    j0 = c0 // tn
    return pl.pallas_call(
        _mm_slabs_body,
        grid=(M // tm, Nc // tn),
        in_specs=[pl.BlockSpec((tm, K), lambda i, j: (i, 0)), pl.BlockSpec((K, tn), lambda i, j: (0, j0 + j))],
        out_specs=pl.BlockSpec((tn // LANES, tm, LANES), lambda i, j: (j, i, 0)),
        out_shape=jax.ShapeDtypeStruct((Nc // LANES, M, LANES), f32),
        compiler_params=_params(("parallel", "parallel")),
        name=name,
    )(a, b)


def _rope_slot(y, ta, tb, tc):
    return y * ta + pltpu.roll(y, LANES - 32, 1) * tb + pltpu.roll(y, 32, 1) * tc


def _mlaq_body(a_ref, b_ref, ta_ref, tb_ref, tc_ref, o_ref, *, scale, nslot):
    acc = jnp.dot(a_ref[...], b_ref[...], preferred_element_type=f32)
    ta, tb, tc = ta_ref[...], tb_ref[...], tc_ref[...]
    for s in range(nslot):
        lo = s * MLA_SLOT
        o_ref[:, lo:lo + MLA_NOPE] = (acc[:, lo:lo + MLA_NOPE] * scale).astype(o_ref.dtype)
        rot = _rope_slot(acc[:, lo + MLA_NOPE:lo + MLA_SLOT], ta, tb, tc)
        o_ref[:, lo + MLA_NOPE:lo + MLA_SLOT] = (rot * scale).astype(o_ref.dtype)


def _mla_q(xb, w, cols, ropes, T, tm, tn):
    M, K = xb.shape
    c0, Nc = cols
    assert c0 % tn == 0 and Nc % tn == 0
    j0 = c0 // tn
    nt = T // tm
    tab = pl.BlockSpec((tm, LANES), lambda i, j: (i % nt, 0))
    return pl.pallas_call(
        functools.partial(_mlaq_body, scale=(MLA_NOPE + MLA_ROPE) ** -0.5 * LOG2E, nslot=tn // MLA_SLOT),
        grid=(M // tm, Nc // tn),
        in_specs=[pl.BlockSpec((tm, K), lambda i, j: (i, 0)), pl.BlockSpec((K, tn), lambda i, j: (0, j0 + j)),
                  tab, tab, tab],
        out_specs=pl.BlockSpec((tm, tn), lambda i, j: (i, j)),
        out_shape=jax.ShapeDtypeStruct((M, Nc), bf16),
        compiler_params=_params(("parallel", "parallel")),
        name="mla_q_proj_rope",
    )(xb, w, *ropes)


def _mlakv_body(c_ref, g_ref, wuk_ref, wuv_ref, ta_ref, tb_ref, tc_ref, k_ref, v_ref):
    x = c_ref[...]
    c = x[:, :MLA_KV_RANK]
    ms = jnp.mean(c * c, axis=-1, keepdims=True)
    cn = (c * lax.rsqrt(ms + RMS_EPS) * g_ref[...]).astype(bf16)
    kpe = _rope_slot(x[:, MLA_KV_RANK:], ta_ref[...], tb_ref[...], tc_ref[...]).astype(bf16)
    kn = jnp.dot(cn, wuk_ref[...], preferred_element_type=f32).astype(bf16)
    v_ref[...] = jnp.dot(cn, wuv_ref[...], preferred_element_type=f32).astype(bf16)
    for h in range(MLA_HEADS):
        k_ref[:, h * MLA_SLOT:h * MLA_SLOT + MLA_NOPE] = kn[:, h * MLA_NOPE:(h + 1) * MLA_NOPE]
        k_ref[:, h * MLA_SLOT + MLA_NOPE:(h + 1) * MLA_SLOT] = kpe


def _mla_kv(ckv, g, wuk, wuv, ropes, T, tm):
    M, K = ckv.shape
    nt = T // tm
    tab = pl.BlockSpec((tm, LANES), lambda i: (i % nt, 0))
    full = lambda shape: pl.BlockSpec(shape, lambda i: (0,) * len(shape))
    return pl.pallas_call(
        _mlakv_body,
        grid=(M // tm,),
        in_specs=[pl.BlockSpec((tm, K), lambda i: (i, 0)), full(g.shape), full(wuk.shape), full(wuv.shape), tab, tab, tab],
        out_specs=[pl.BlockSpec((tm, MLA_HEADS * MLA_SLOT), lambda i: (i, 0)),
                   pl.BlockSpec((tm, MLA_HEADS * HEAD_DIM), lambda i: (i, 0))],
        out_shape=[jax.ShapeDtypeStruct((M, MLA_HEADS * MLA_SLOT), bf16),
                   jax.ShapeDtypeStruct((M, MLA_HEADS * HEAD_DIM), bf16)],
        compiler_params=_params(("parallel",)),
        name="mla_kv_prep",
    )(ckv, g, wuk, wuv, *ropes)


FLAG_FIRST, FLAG_LAST, FLAG_CAUSAL = 1, 2, 4


def _tile_schedule(nq, tq, tk, first_key_tile):
    qs, ks, fl = [], [], []
    for qi in range(nq):
        lo = first_key_tile(qi)
        hi = (qi * tq + tq - 1) // tk
        for ki in range(lo, hi + 1):
            qs.append(qi)
            ks.append(ki)
            fl.append((FLAG_FIRST if ki == lo else 0) | (FLAG_LAST if ki == hi else 0)
                      | (FLAG_CAUSAL if ki * tk + tk - 1 > qi * tq else 0))
    return tuple(jnp.asarray(np.array(a, np.int32)) for a in (qs, ks, fl))


def _flash_scratch(heads, tq, tk):
    stat = pltpu.VMEM((heads, tq, LANES), f32)
    return [stat, stat, pltpu.VMEM((heads, tq, HEAD_DIM), f32),
            pltpu.VMEM((3, tq, tk), f32), pltpu.VMEM((2, tq, tk), bf16), pltpu.VMEM((2, tq, LANES), f32),
            pltpu.VMEM((2, tq, LANES), f32), pltpu.VMEM((tq, tk), f32)]


def _flash_init(m_ref, l_ref, acc_ref):
    m_ref[...] = jnp.full(m_ref.shape, M_INIT, f32)
    l_ref[...] = jnp.zeros(l_ref.shape, f32)
    acc_ref[...] = jnp.zeros(acc_ref.shape, f32)


def _flash_heads(nheads, tq, tk, score_fn, add_fn, value_fn, m_ref, l_ref, acc_ref, s_scr, p_scr, mx_scr, a_scr):
    nkc = tk // LANES
    chunks = [(pl.ds(c * ROW_CHUNK, ROW_CHUNK), c * ROW_CHUNK) for c in range(tq // ROW_CHUNK)]

    def scores(h):
        s_scr[h % 3] = score_fn(h)

    def pass_a(h):
        s_ref = s_scr.at[h % 3]
        for rows, r0 in chunks:
            mx = None
            for kc in range(nkc):
                cols = slice(kc * LANES, (kc + 1) * LANES)
                s = s_ref[rows, cols]
                if add_fn is not None:
                    for term in add_fn(h, rows, r0, kc):
                        s = s + term
                    s_ref[rows, cols] = s
                mx = s if mx is None else jnp.maximum(mx, s)
            mx_scr[h % 2, rows, :] = mx
        m_prev = m_ref[h]
        m_new = jnp.maximum(m_prev, jnp.max(mx_scr[h % 2], axis=-1, keepdims=True))
        a_scr[h % 2] = jnp.exp2(m_prev - m_new)
        m_ref[h] = m_new

    def pass_b(h):
        s_ref = s_scr.at[h % 3]
        for rows, _ in chunks:
            m_rows = m_ref[h, rows, :]
            for kc in range(nkc):
                cols = slice(kc * LANES, (kc + 1) * LANES)
                p_scr[h % 2, rows, cols] = jnp.exp2(s_ref[rows, cols] - m_rows).astype(bf16)
        pv = jnp.dot(p_scr[h % 2], value_fn(h), preferred_element_type=f32)
        alpha = a_scr[h % 2]
        acc_ref[h] = alpha * acc_ref[h] + pv[:, :HEAD_DIM]
        l_ref[h] = alpha * l_ref[h] + pv[:, HEAD_DIM:]

    scores(0)
    for h in range(nheads):
        if h + 1 < nheads:
            scores(h + 1)
        pass_a(h)
        if h >= 1:
            pass_b(h - 1)
    pass_b(nheads - 1)


def _with_ones(v):
    return jnp.concatenate([v, jnp.ones((v.shape[0], LANES), v.dtype)], axis=1)


def _mla_attn_body(qi_tab, ki_tab, fl_tab, q_ref, k_ref, v_ref, o_ref, m_ref, l_ref, acc_ref,
                   s_scr, p_scr, mx_scr, a_scr, mask_scr, *, hb, tq, tk):
    st = pl.program_id(2)
    qi, ki, fl = qi_tab[st], ki_tab[st], fl_tab[st]

    pl.when((fl & FLAG_FIRST) != 0)(lambda: _flash_init(m_ref, l_ref, acc_ref))

    def score(h):
        qk = slice(h * MLA_SLOT, (h + 1) * MLA_SLOT)
        return lax.dot_general(q_ref[:, qk], k_ref[:, qk], NT, preferred_element_type=f32)

    def value(h):
        return _with_ones(v_ref[:, h * HEAD_DIM:(h + 1) * HEAD_DIM])

    def tile(causal):
        add = None
        if causal:
            row = qi * tq + lax.broadcasted_iota(i32, (tq, tk), 0)
            col = ki * tk + lax.broadcasted_iota(i32, (tq, tk), 1)
            mask_scr[...] = jnp.where(row >= col, 0.0, NEG_INF)
            add = lambda h, rows, r0, kc: (mask_scr[rows, kc * LANES:(kc + 1) * LANES],)
        _flash_heads(hb, tq, tk, score, add, value, m_ref, l_ref, acc_ref, s_scr, p_scr, mx_scr, a_scr)

    pl.when((fl & FLAG_CAUSAL) != 0)(lambda: tile(True))
    pl.when((fl & FLAG_CAUSAL) == 0)(lambda: tile(False))

    @pl.when((fl & FLAG_LAST) != 0)
    def _():
        for h in range(hb):
            o_ref[:, h * HEAD_DIM:(h + 1) * HEAD_DIM] = (acc_ref[h] / l_ref[h]).astype(o_ref.dtype)


def _mla_attn(q, k, v, B, T, hb=MLA_HEADS, tq=Q_TILE, tk=K_TILE):
    tk = min(tk, T)
    nq, nk = T // tq, T // tk
    tabs = _tile_schedule(nq, tq, tk, lambda qi: 0)
    grid_spec = pltpu.PrefetchScalarGridSpec(
        num_scalar_prefetch=3,
        grid=(B, MLA_HEADS // hb, int(tabs[0].shape[0])),
        in_specs=[pl.BlockSpec((tq, hb * MLA_SLOT), lambda b, h, s, qt, kt, ft: (b * nq + qt[s], h)),
                  pl.BlockSpec((tk, hb * MLA_SLOT), lambda b, h, s, qt, kt, ft: (b * nk + kt[s], h)),
                  pl.BlockSpec((tk, hb * HEAD_DIM), lambda b, h, s, qt, kt, ft: (b * nk + kt[s], h))],
        out_specs=pl.BlockSpec((tq, hb * HEAD_DIM), lambda b, h, s, qt, kt, ft: (b * nq + qt[s], h)),
        scratch_shapes=_flash_scratch(hb, tq, tk))
    return pl.pallas_call(
        functools.partial(_mla_attn_body, hb=hb, tq=tq, tk=tk),
        grid_spec=grid_spec,
        out_shape=jax.ShapeDtypeStruct((B * T, MLA_HEADS * HEAD_DIM), bf16),
        compiler_params=_params(("parallel", "parallel", "arbitrary")),
        name="mla_flash_attention",
    )(*tabs, q, k, v)


def _bias_table_body(rev_ref, o_ref):
    width = rev_ref.shape[-1]
    x = jnp.broadcast_to(rev_ref[0], (LANES, width))
    rolled = pltpu.roll(x, 0, 1, stride=1, stride_axis=0)
    for kb in range(width // LANES):
        o_ref[0, kb] = rolled[:, kb * LANES:(kb + 1) * LANES]


def _bias_table(rev):
    H, _, U = rev.shape
    return pl.pallas_call(
        _bias_table_body,
        grid=(H,),
        in_specs=[pl.BlockSpec((1, 1, U), lambda h: (h, 0, 0))],
        out_specs=pl.BlockSpec((1, U // LANES, LANES, LANES), lambda h: (h, 0, 0, 0)),
        out_shape=jax.ShapeDtypeStruct((H, U // LANES, LANES, LANES), f32),
        compiler_params=_params(("parallel",)),
        name="relative_bias_toeplitz_table",
    )(rev)


def _nsa_band_body(qi_tab, ki_tab, fl_tab, q_ref, k_ref, v_ref, g_ref, *rest, tq, tk, seq, select, kb_lo):
    if select:
        sel_ref, e_ref, o_ref, m_ref, l_ref, acc_ref, s_scr, p_scr, mx_scr, a_scr, mask_scr = rest
    else:
        oc_ref, os_ref, gate_ref, o_ref, m_ref, l_ref, acc_ref, s_scr, p_scr, mx_scr, a_scr, mask_scr = rest
    st = pl.program_id(2)
    qi, ki, fl = qi_tab[st], ki_tab[st], fl_tab[st]

    pl.when((fl & FLAG_FIRST) != 0)(lambda: _flash_init(m_ref, l_ref, acc_ref))

    d0 = qi * tq - ki * tk
    dist = d0 + lax.broadcasted_iota(i32, (tq, tk), 0) - lax.broadcasted_iota(i32, (tq, tk), 1)
    if select:
        picked = jnp.dot(sel_ref[...], e_ref[...], preferred_element_type=f32)
        valid = (dist >= 0) & (picked > 0.5)
    else:
        valid = (dist >= 0) & (dist < WINDOW)
    mask_scr[...] = jnp.where(valid, 0.0, NEG_INF)
    ng = k_ref.shape[0]
    ks = [k_ref[gi].astype(bf16) for gi in range(ng)]
    vs = [_with_ones(v_ref[gi].astype(bf16)) for gi in range(ng)]
    kb0 = (seq - d0) // LANES - kb_lo

    def score(h):
        return lax.dot_general(q_ref[:, h * HEAD_DIM:(h + 1) * HEAD_DIM], ks[h // NSA_HPG], NT,
                               preferred_element_type=f32)

    def add(h, rows, r0, kc):
        bias = g_ref[h, kb0 - r0 // LANES + kc, pl.ds(r0 % LANES, ROW_CHUNK), :]
        return bias, mask_scr[rows, kc * LANES:(kc + 1) * LANES]

    _flash_heads(ng * NSA_HPG, tq, tk, score, add, lambda h: vs[h // NSA_HPG], m_ref, l_ref, acc_ref,
                 s_scr, p_scr, mx_scr, a_scr)

    @pl.when((fl & FLAG_LAST) != 0)
    def _():
        for h in range(ng * NSA_HPG):
            sl = slice(h * HEAD_DIM, (h + 1) * HEAD_DIM)
            o = acc_ref[h] / l_ref[h]
            if not select:
                gate = gate_ref[...]
                g0 = (h // NSA_HPG) * LANES + h % NSA_HPG
                o = (gate[:, g0:g0 + 1] * oc_ref[:, sl].astype(f32)
                     + gate[:, g0 + NSA_HPG:g0 + NSA_HPG + 1] * os_ref[:, sl].astype(f32)
                     + gate[:, g0 + 2 * NSA_HPG:g0 + 2 * NSA_HPG + 1] * o)
            o_ref[:, sl] = o.astype(o_ref.dtype)


def _nsa_band(q, kv, gtab, B, T, kcol, vcol, tk, sel=None, emat=None, merge=None, tq=Q_TILE):
    select = sel is not None
    nq, nk = T // tq, T // tk
    first = (lambda qi: 0) if select else (lambda qi: max(qi * tq - (WINDOW - 1), 0) // tk)
    tabs = _tile_schedule(nq, tq, tk, first)
    if select:
        ng, kb_lo = 1, 0
    else:
        ng = NSA_GROUPS
        kb_lo = (T - (WINDOW + tk - 1) // tk * tk) // LANES - tq // LANES
        gtab = gtab[:, kb_lo:T // LANES + tk // LANES]
    gw = ng * NSA_HPG * HEAD_DIM
    in_specs = [
        pl.BlockSpec((tq, gw), lambda b, g, s, qt, kt, ft: (b * nq + qt[s], g)),
        pl.BlockSpec((ng, tk, HEAD_DIM), lambda b, g, s, qt, kt, ft: ((kcol * NSA_GROUPS) // ng + g, b * nk + kt[s], 0)),
        pl.BlockSpec((ng, tk, HEAD_DIM), lambda b, g, s, qt, kt, ft: ((vcol * NSA_GROUPS) // ng + g, b * nk + kt[s], 0)),
        pl.BlockSpec((ng * NSA_HPG,) + gtab.shape[1:], lambda b, g, s, qt, kt, ft: (g, 0, 0, 0),
                     pipeline_mode=pl.Buffered(1)),
    ]
    args = [q, kv, kv, gtab]
    if select:
        in_specs += [pl.BlockSpec((tq, LANES), lambda b, g, s, qt, kt, ft: (b * nq + qt[s], g)),
                     pl.BlockSpec((LANES, tk), lambda b, g, s, qt, kt, ft: (0, kt[s]))]
        args += [sel, emat]
    else:
        o_other = pl.BlockSpec((tq, gw), lambda b, g, s, qt, kt, ft: (b * nq + qt[s], g))
        in_specs += [o_other, o_other,
                     pl.BlockSpec((tq, ng * LANES), lambda b, g, s, qt, kt, ft: (b * nq + qt[s], g))]
        args += list(merge)
    grid_spec = pltpu.PrefetchScalarGridSpec(
        num_scalar_prefetch=3,
        grid=(B, NSA_GROUPS // ng, int(tabs[0].shape[0])),
        in_specs=in_specs,
        out_specs=pl.BlockSpec((tq, gw), lambda b, g, s, qt, kt, ft: (b * nq + qt[s], g)),
        scratch_shapes=_flash_scratch(ng * NSA_HPG, tq, tk))
    return pl.pallas_call(
        functools.partial(_nsa_band_body, tq=tq, tk=tk, seq=T, select=select, kb_lo=kb_lo),
        grid_spec=grid_spec,
        out_shape=jax.ShapeDtypeStruct((B * T, NSA_HEADS * HEAD_DIM), bf16),
        compiler_params=_params(("parallel", "parallel", "arbitrary")),
        name="nsa_selected_attention" if select else "nsa_window_attention",
    )(*tabs, *args)


def _compress_body(x_ref, pe_ref, w1_ref, w2_ref, o_ref, ot_ref, *, nc):
    half = CMP_BLOCK // 2
    h_lo = jnp.zeros((nc, CMP_HIDDEN), f32)
    h_hi = jnp.zeros((nc, CMP_HIDDEN), f32)
    for p in range(half):
        rows = x_ref[0, pl.ds(p, nc, stride=CMP_STRIDE), :]
        h_lo += jnp.dot((rows + pe_ref[0, p:p + 1, :]).astype(bf16), w1_ref[0, p], preferred_element_type=f32)
        h_hi += jnp.dot((rows + pe_ref[0, half + p:half + p + 1, :]).astype(bf16), w1_ref[0, half + p],
                        preferred_element_type=f32)
    hid = h_lo + pltpu.roll(h_hi, nc - 1, 0)
    out = jnp.dot(jax.nn.gelu(hid).astype(bf16), w2_ref[0], preferred_element_type=f32)
    out = jnp.where(lax.broadcasted_iota(i32, out.shape, 0) < nc - 1, out, 0.0)
    o_ref[0, 0, 0] = out.astype(bf16)
    ot_ref[0, 0, 0] = out.T.astype(bf16)


def _compress(kv, pe, w1, w2, B, T):
    nc = T // CMP_STRIDE
    return pl.pallas_call(
        functools.partial(_compress_body, nc=nc),
        grid=(2, B, NSA_GROUPS),
        in_specs=[pl.BlockSpec((1, T, HEAD_DIM), lambda c, b, g: (c * NSA_GROUPS + g, b, 0)),
                  pl.BlockSpec((1, CMP_BLOCK, HEAD_DIM), lambda c, b, g: (c, 0, 0)),
                  pl.BlockSpec((1, CMP_BLOCK, HEAD_DIM, CMP_HIDDEN), lambda c, b, g: (c, 0, 0, 0)),
                  pl.BlockSpec((1, CMP_HIDDEN, HEAD_DIM), lambda c, b, g: (c, 0, 0))],
        out_specs=[pl.BlockSpec((1, 1, 1, nc, HEAD_DIM), lambda c, b, g: (c, b, g, 0, 0)),
                   pl.BlockSpec((1, 1, 1, HEAD_DIM, nc), lambda c, b, g: (c, b, g, 0, 0))],
        out_shape=[jax.ShapeDtypeStruct((2, B, NSA_GROUPS, nc, HEAD_DIM), bf16),
                   jax.ShapeDtypeStruct((2, B, NSA_GROUPS, HEAD_DIM, nc), bf16)],
        compiler_params=_params(("parallel", "parallel", "parallel")),
        name="nsa_compress_mlp",
    )(kv, pe, w1, w2)


def _cmp_body(q_ref, kc_ref, vct_ref, cw_ref, ovt_ref, o_ref, sel_ref, *, nc, nq):
    tq = CMP_TQ
    qi = pl.program_id(2)
    t0 = qi * tq
    key_row = lax.broadcasted_iota(i32, (nc, tq), 0)
    q_col = lax.broadcasted_iota(i32, (nc, tq), 1)
    valid = (t0 + q_col - CMP_STRIDE * key_row - (CMP_BLOCK - 1)) >= 0
    kc = kc_ref[0, 0, 0]
    vct = vct_ref[0, 0, 0]
    psum = jnp.zeros((nc, tq), f32)
    for h in range(NSA_HPG):
        bias = jnp.concatenate(
            [jnp.concatenate([cw_ref[h, qi * (tq // LANES) + b - c + nq] for b in range(tq // LANES)], axis=1)
             for c in range(nc // 8)], axis=0)
        s = lax.dot_general(kc, q_ref[:, h * HEAD_DIM:(h + 1) * HEAD_DIM], NT, preferred_element_type=f32)
        s = jnp.where(valid, s + bias, NEG_INF)
        m = jnp.max(s, axis=0, keepdims=True)
        e = jnp.where(valid, jnp.exp2(s - m), 0.0)
        l = jnp.sum(e, axis=0, keepdims=True)
        p = e / jnp.where(l > 0.0, l, 1.0)
        psum += p
        o_t = jnp.dot(vct, p.astype(bf16), preferred_element_type=f32)
        o_ref[:, h * HEAD_DIM:(h + 1) * HEAD_DIM] = o_t.T.astype(o_ref.dtype)

    imp = jnp.dot(ovt_ref[...], psum, preferred_element_type=f32, precision=lax.Precision.HIGHEST)
    nrow = imp.shape[0]
    n_slc = (nc * CMP_STRIDE) // SLC_BLOCK
    j = lax.broadcasted_iota(i32, (nrow, tq), 0)
    t = t0 + lax.broadcasted_iota(i32, (nrow, tq), 1)
    cur = jnp.right_shift(t, 6)
    ok = j * SLC_BLOCK <= t
    forced = (j == 0) | (j == cur) | (j == cur - 1)
    score = jnp.where(forced, FORCED_SCORE, jnp.where(ok, imp, NEG_INF))
    score = jnp.where(j < n_slc, score, PAD_SCORE)
    chosen = jnp.zeros((nrow, tq), f32)
    row_id = j.astype(f32)
    for _ in range(min(SLC_TOPK, n_slc)):
        top = jnp.max(score, axis=0, keepdims=True)
        first = jnp.min(jnp.where(score == top, row_id, PAD_ID), axis=0, keepdims=True)
        hit = row_id == first
        chosen = jnp.where(hit, 1.0, chosen)
        score = jnp.where(hit, PAD_SCORE, score)
    chosen = jnp.where(ok & (j < n_slc), chosen, 0.0)
    sel_ref[...] = chosen.T.astype(sel_ref.dtype)


def _nsa_cmp(q, kc, vct, cw, ovt, B, T):
    nc = T // CMP_STRIDE
    nq = T // CMP_TQ
    gw = NSA_HPG * HEAD_DIM
    return pl.pallas_call(
        functools.partial(_cmp_body, nc=nc, nq=T // LANES),
        grid=(B, NSA_GROUPS, nq),
        in_specs=[pl.BlockSpec((CMP_TQ, gw), lambda b, g, qi: (b * nq + qi, g)),
                  pl.BlockSpec((1, 1, 1, nc, HEAD_DIM), lambda b, g, qi: (0, b, g, 0, 0)),
                  pl.BlockSpec((1, 1, 1, HEAD_DIM, nc), lambda b, g, qi: (1, b, g, 0, 0)),
                  pl.BlockSpec((NSA_HPG,) + cw.shape[1:], lambda b, g, qi: (g, 0, 0, 0)),
                  pl.BlockSpec(ovt.shape, lambda b, g, qi: (0, 0))],
        out_specs=[pl.BlockSpec((CMP_TQ, gw), lambda b, g, qi: (b * nq + qi, g)),
                   pl.BlockSpec((CMP_TQ, LANES), lambda b, g, qi: (b * nq + qi, g))],
        out_shape=[jax.ShapeDtypeStruct((B * T, NSA_HEADS * HEAD_DIM), bf16),
                   jax.ShapeDtypeStruct((B * T, NSA_GROUPS * LANES), bf16)],
        compiler_params=_params(("parallel", "parallel", "parallel")),
        name="nsa_compressed_attention_select",
    )(q, kc, vct, cw, ovt)


def _layer_norm(y, g, b):
    mu = jnp.mean(y, axis=-1, keepdims=True)
    var = jnp.mean(jnp.square(y - mu), axis=-1, keepdims=True)
    return (y - mu) * lax.rsqrt(var + LN_EPS) * g + b


def _branch_merge_body(a_ref, b_ref, wa_ref, wb_ref, ga_ref, gb_ref, o_ref):
    ya = jnp.dot(a_ref[...], wa_ref[...], preferred_element_type=f32)
    yb = jnp.dot(b_ref[...], wb_ref[...], preferred_element_type=f32)
    o_ref[...] = (ga_ref[...] * ya + gb_ref[...] * yb).astype(o_ref.dtype)


def _branch_merge(a, b, wa, wb, mg, tm, tn):
    M, K = a.shape
    Nc = wa.shape[1]
    nj = Nc // tn
    return pl.pallas_call(
        _branch_merge_body,
        grid=(M // tm, nj),
        in_specs=[pl.BlockSpec((tm, K), lambda i, j: (i, 0)), pl.BlockSpec((tm, K), lambda i, j: (i, 0)),
                  pl.BlockSpec((K, tn), lambda i, j: (0, j)), pl.BlockSpec((K, tn), lambda i, j: (0, j)),
                  pl.BlockSpec((tm, tn), lambda i, j: (i, j)), pl.BlockSpec((tm, tn), lambda i, j: (i, nj + j))],
        out_specs=pl.BlockSpec((tm, tn), lambda i, j: (i, j)),
        out_shape=jax.ShapeDtypeStruct((M, Nc), bf16),
        compiler_params=_params(("parallel", "parallel")),
        name="branch_proj_merge",
    )(a, b, wa, wb, mg, mg)


def _proj_ln_body(z_ref, w_ref, x_ref, g_ref, b_ref, o_ref):
    y = jnp.dot(z_ref[...], w_ref[...], preferred_element_type=f32)
    o_ref[...] = _layer_norm(ALPHA * x_ref[...] + y, g_ref[...], b_ref[...])


def _proj_ln(z, w, x, g, b, tm):
    M, K = z.shape
    D = w.shape[1]
    vec = pl.BlockSpec((1, D), lambda i: (0, 0))
    return pl.pallas_call(
        _proj_ln_body,
        grid=(M // tm,),
        in_specs=[pl.BlockSpec((tm, K), lambda i: (i, 0)), pl.BlockSpec((K, D), lambda i: (0, 0)),
                  pl.BlockSpec((tm, D), lambda i: (i, 0)), vec, vec],
        out_specs=pl.BlockSpec((tm, D), lambda i: (i, 0)),
        out_shape=jax.ShapeDtypeStruct((M, D), f32),
        compiler_params=_params(("parallel",)),
        name="mix_out_proj_layernorm",
    )(z, w, x, g, b)


def _mem_attn_body(x_ref, wq_ref, k_ref, v_ref, wo_ref, g_ref, b_ref, o_ref, ob_ref):
    x = x_ref[...]
    q = (jnp.dot(x.astype(bf16), wq_ref[...], preferred_element_type=f32) * HEAD_DIM ** -0.5).astype(bf16)
    heads = []
    for h in range(MEM_HEADS):
        sl = slice(h * HEAD_DIM, (h + 1) * HEAD_DIM)
        s = lax.dot_general(q[:, sl], k_ref[0, :, sl], NT, preferred_element_type=f32)
        e = jnp.exp(s - jnp.max(s, axis=-1, keepdims=True))
        p = e / jnp.sum(e, axis=-1, keepdims=True)
        heads.append(jnp.dot(p.astype(bf16), v_ref[0, :, sl], preferred_element_type=f32).astype(bf16))
    y = jnp.dot(jnp.concatenate(heads, axis=1), wo_ref[...], preferred_element_type=f32)
    out = _layer_norm(ALPHA * x + y, g_ref[...], b_ref[...])
    o_ref[...] = out
    ob_ref[...] = out.astype(bf16)


def _mem_attn(x, wq, km, vm, wo, g, b, T, tm):
    M, D = x.shape
    nt = T // tm
    W = wq.shape[1]
    vec = pl.BlockSpec((1, D), lambda i: (0, 0))
    blk = pl.BlockSpec((tm, D), lambda i: (i, 0))
    return pl.pallas_call(
        _mem_attn_body,
        grid=(M // tm,),
        in_specs=[blk, pl.BlockSpec((D, W), lambda i: (0, 0)),
                  pl.BlockSpec((1, N_MEM, W), lambda i: (i // nt, 0, 0)),
                  pl.BlockSpec((1, N_MEM, W), lambda i: (i // nt, 0, 0)),
                  pl.BlockSpec((W, D), lambda i: (0, 0)), vec, vec],
        out_specs=[blk, blk],
        out_shape=[jax.ShapeDtypeStruct((M, D), f32), jax.ShapeDtypeStruct((M, D), bf16)],
        compiler_params=_params(("parallel",)),
        name="memory_cross_attention_layernorm",
    )(x, wq, km, vm, wo, g, b)


def _add_ln_body(x_ref, y_ref, g_ref, b_ref, o_ref):
    o_ref[...] = _layer_norm(ALPHA * x_ref[...] + y_ref[...], g_ref[...], b_ref[...])


def _add_ln(x, y, g, b, tm):
    M, D = x.shape
    vec = pl.BlockSpec((1, D), lambda i: (0, 0))
    blk = pl.BlockSpec((tm, D), lambda i: (i, 0))
    return pl.pallas_call(
        _add_ln_body,
        grid=(M // tm,),
        in_specs=[blk, blk, vec, vec],
        out_specs=blk,
        out_shape=jax.ShapeDtypeStruct((M, D), f32),
        compiler_params=_params(("parallel",)),
        name="residual_layernorm",
    )(x, y, g, b)


def _topk_rows(s, ids, k):
    n = s.shape[1]
    slot = lax.broadcasted_iota(i32, (k, n), 0)
    vals = jnp.zeros((k, n), f32)
    inds = jnp.zeros((k, n), f32)
    for r in range(k):
        top = jnp.max(s, axis=0, keepdims=True)
        first = jnp.min(jnp.where(s == top, ids, PAD_ID), axis=0, keepdims=True)
        vals = jnp.where(slot == r, top, vals)
        inds = jnp.where(slot == r, first, inds)
        s = jnp.where(ids == first, PAD_SCORE, s)
    return vals, inds


def _pick_rows(table, sel):
    out = jnp.zeros(sel.shape, table.dtype)
    for a in range(table.shape[0]):
        out = jnp.where(sel == a, table[a:a + 1, :], out)
    return out


CAND_COUNTS = tuple(PEER_TOPK // (a + 1) for a in range(PEER_TOPK))
CAND_ROWS = -(-sum(CAND_COUNTS) // 8) * 8


def _cand_ids(tn):
    ids = [a * PEER_TOPK + b for a, nb in enumerate(CAND_COUNTS) for b in range(nb)]
    ids = np.array(ids + [PAD_ID] * (CAND_ROWS - len(ids)), np.float32)
    return jnp.asarray(np.broadcast_to(ids[:, None], (CAND_ROWS, tn)))


def _peer_route_body(q_ref, keys_ref, ids_ref, e1_ref, e2_ref, gt_ref, e1_scr, e2_scr, g_scr, cand_scr):
    k1 = keys_ref[0].astype(bf16)
    k2 = keys_ref[1].astype(bf16)
    tn = q_ref.shape[0]
    key_ids = lax.broadcasted_iota(i32, (PEER_N_KEYS, tn), 0).astype(f32)
    cand_scr[...] = jnp.full(cand_scr.shape, PAD_SCORE, f32)
    for h in range(PEER_HEADS):
        q1 = q_ref[:, (2 * h) * LANES:(2 * h + 1) * LANES]
        q2 = q_ref[:, (2 * h + 1) * LANES:(2 * h + 2) * LANES]
        s1 = lax.dot_general(k1, q1, NT, preferred_element_type=f32)
        s2 = lax.dot_general(k2, q2, NT, preferred_element_type=f32)
        v1, i1 = _topk_rows(s1, key_ids, PEER_TOPK)
        v2, i2 = _topk_rows(s2, key_ids, PEER_TOPK)
        off = 0
        for a, nb in enumerate(CAND_COUNTS):
            cand_scr[off:off + nb, :] = v1[a:a + 1, :] + v2[0:nb, :]
            off += nb
        top_s, flat = _topk_rows(cand_scr[...], ids_ref[...], PEER_TOPK)
        flat = flat.astype(i32)
        e = jnp.exp(top_s - top_s[0:1, :])
        rows = slice(h * PEER_TOPK, (h + 1) * PEER_TOPK)
        g_scr[rows, :] = e / jnp.sum(e, axis=0, keepdims=True)
        e1_scr[rows, :] = _pick_rows(i1, jnp.right_shift(flat, 4)).astype(i32)
        e2_scr[rows, :] = _pick_rows(i2, jnp.bitwise_and(flat, PEER_TOPK - 1)).astype(i32)
    e1_ref[...] = e1_scr[...].T
    e2_ref[...] = e2_scr[...].T
    gt_ref[...] = g_scr[...].T


def _peer_route(q, keys, tn=LANES):
    M, W = q.shape
    out = pl.BlockSpec((tn, PEER_HK), lambda i: (i, 0))
    return pl.pallas_call(
        _peer_route_body,
        grid=(M // tn,),
        in_specs=[pl.BlockSpec((tn, W), lambda i: (i, 0)), pl.BlockSpec(keys.shape, lambda i: (0, 0, 0)),
                  pl.BlockSpec((CAND_ROWS, tn), lambda i: (0, 0))],
        out_specs=[out, out, out],
        out_shape=[jax.ShapeDtypeStruct((M, PEER_HK), i32), jax.ShapeDtypeStruct((M, PEER_HK), i32),
                   jax.ShapeDtypeStruct((M, PEER_HK), f32)],
        scratch_shapes=[pltpu.VMEM((PEER_HK, tn), i32), pltpu.VMEM((PEER_HK, tn), i32), pltpu.VMEM((PEER_HK, tn), f32),
                        pltpu.VMEM((CAND_ROWS, tn), f32)],
        compiler_params=_params(("parallel",)),
        name="peer_product_key_routing",
    )(q, keys, _cand_ids(tn))


TOKEN_GROUP = 16


def _peer_weights_body(e1_ref, e2_ref, g_ref, w_ref, *, tn):
    sub = lax.broadcasted_iota(i32, (PEER_N_KEYS, PEER_HK), 0)

    def one(n):
        r1 = e1_ref[pl.ds(n, 1), :]
        r2 = e2_ref[pl.ds(n, 1), :]
        g = g_ref[pl.ds(n, 1), :]
        a_t = (sub == r1).astype(bf16)
        b_t = jnp.where(sub == r2, g, 0.0).astype(bf16)
        return lax.dot_general(a_t, b_t, NT, preferred_element_type=f32)

    def group(i, carry):
        n0 = pl.multiple_of(i * TOKEN_GROUP, TOKEN_GROUP)
        tables = jnp.stack([one(n0 + t) for t in range(TOKEN_GROUP)], axis=0)
        w_ref[:, pl.ds(n0, TOKEN_GROUP), :] = jnp.swapaxes(tables, 0, 1).astype(w_ref.dtype)
        return carry

    lax.fori_loop(0, tn // TOKEN_GROUP, group, 0, unroll=2)


def _peer_weights(e1, e2, g, tn=256):
    M = e1.shape[0]
    blk = pl.BlockSpec((tn, PEER_HK), lambda i: (i, 0))
    return pl.pallas_call(
        functools.partial(_peer_weights_body, tn=tn),
        grid=(M // tn,),
        in_specs=[blk, blk, blk],
        out_specs=pl.BlockSpec((PEER_N_KEYS, tn, PEER_N_KEYS), lambda i: (0, i, 0)),
        out_shape=jax.ShapeDtypeStruct((PEER_N_KEYS, M, PEER_N_KEYS), bf16),
        compiler_params=_params(("parallel",)),
        name="peer_routing_weights",
    )(e1, e2, g)


GELU_SLOPE = 2.0 * math.sqrt(2.0 / math.pi)


def _gelu_tanh(x):
    minus_2z = x * ((-GELU_SLOPE * 0.044715 * LOG2E) * (x * x) - GELU_SLOPE * LOG2E)
    return x / (1.0 + jnp.exp2(minus_2z))


def _peer_dense_body(x_ref, w_ref, u_ref, v_ref, o_ref, *, rows):
    @pl.when(pl.program_id(1) == 0)
    def _():
        o_ref[...] = jnp.zeros(o_ref.shape, f32)

    act = _gelu_tanh(lax.dot_general(x_ref[...], u_ref[...].astype(bf16), NT, preferred_element_type=f32))
    w = jnp.concatenate([w_ref[r] for r in range(rows)], axis=1)
    o_ref[...] += jnp.dot(w * act.astype(bf16), v_ref[...].astype(bf16), preferred_element_type=f32)


def _peer_dense(xb, w, u, v, tm=1024, rows=8):
    M, D = xb.shape
    te = rows * PEER_N_KEYS
    once = dict(pipeline_mode=pl.Buffered(1))
    return pl.pallas_call(
        functools.partial(_peer_dense_body, rows=rows),
        grid=(M // tm, PEER_N_KEYS // rows),
        in_specs=[pl.BlockSpec((tm, D), lambda i, e: (i, 0), **once),
                  pl.BlockSpec((rows, tm, PEER_N_KEYS), lambda i, e: (e, i, 0)),
                  pl.BlockSpec((te, D), lambda i, e: (e, 0)), pl.BlockSpec((te, D), lambda i, e: (e, 0))],
        out_specs=pl.BlockSpec((tm, D), lambda i, e: (i, 0), **once),
        out_shape=jax.ShapeDtypeStruct((M, D), f32),
        compiler_params=_params(("parallel", "arbitrary")),
        name="peer_dense_experts",
    )(xb, w, u, v)


def _t5_bucket(dist):
    dist = jnp.maximum(dist, 0)
    n_log = REL_BUCKETS - REL_MAX_EXACT
    large = REL_MAX_EXACT + (jnp.log(jnp.maximum(dist, 1).astype(f32) / REL_MAX_EXACT)
                             / math.log(REL_MAX_DIST / REL_MAX_EXACT) * n_log).astype(i32)
    large = jnp.minimum(large, REL_BUCKETS - 1)
    return jnp.where(dist < REL_MAX_EXACT, dist, large)


def _position_tables(rel_bias, T):
    by_dist = (rel_bias[_t5_bucket(jnp.arange(T))] * LOG2E).T.astype(f32)
    rev = jnp.pad(by_dist[:, ::-1], ((0, 0), (1, K_TILE - 1)), mode="edge")[:, None, :]
    shift = (CMP_BLOCK - 1) + CMP_STRIDE * 7
    ext = jnp.pad(by_dist, ((0, 0), (T + shift, 0)), mode="edge")
    rows = [ext[:, CMP_STRIDE * (7 - r):CMP_STRIDE * (7 - r) + 2 * T] for r in range(8)]
    cmpw = jnp.stack(rows, axis=1).reshape(NSA_HEADS, 8, 2 * T // LANES, LANES)
    return rev, jnp.transpose(cmpw, (0, 2, 1, 3))


def _rope_slot_tables(T):
    inv = ROPE_THETA ** (-jnp.arange(0, MLA_ROPE, 2, dtype=f32) / MLA_ROPE)
    ang = jnp.arange(T, dtype=f32)[:, None] * inv[None, :]
    cos, sin = jnp.cos(ang), jnp.sin(ang)
    z = jnp.zeros_like(cos)
    ta = jnp.concatenate([cos, cos, z, z], axis=1)
    tb = jnp.concatenate([-sin, z, z, z], axis=1)
    tc = jnp.concatenate([z, sin, z, z], axis=1)
    return ta, tb, tc


def _layer(x, mem, rel_bias, w_in, cmp_k_pe, cmp_k_w1, cmp_k_w2, cmp_v_pe, cmp_v_w1, cmp_v_w2,
           mla_kv_norm, mla_w_uk, mla_w_uv, w_branch_nsa, w_branch_mla, w_mix_out, ln_mix_g, ln_mix_b,
           mem_w_q, mem_w_k, mem_w_v, mem_w_o, ln_mem_g, ln_mem_b,
           peer_w_query, peer_sub_keys, peer_u, peer_v, ln_ffn_g, ln_ffn_b, ropes):
    B, T, D = x.shape
    N = B * T
    x2d = x.reshape(N, D)
    xb = x2d.astype(bf16)
    row = lambda a: a.reshape(1, -1)

    gate_cols = np.array([[br * NSA_HEADS + g * NSA_HPG + h for br in range(3) for h in range(NSA_HPG)]
                          for g in range(NSA_GROUPS)])
    w_gate = jnp.pad(w_in[:, OFF_NSA_GATE:OFF_MLA_Q][:, gate_cols],
                     ((0, 0), (0, 0), (0, LANES - 3 * NSA_HPG))).reshape(D, NSA_GROUPS * LANES)
    w_mla_q = jnp.pad(w_in[:, OFF_MLA_Q:OFF_MLA_KV].reshape(D, MLA_HEADS, MLA_NOPE + MLA_ROPE),
                      ((0, 0), (0, 0), (0, MLA_SLOT - MLA_NOPE - MLA_ROPE))).reshape(D, MLA_HEADS * MLA_SLOT)
    w_mla_kv = jnp.pad(w_in[:, OFF_MLA_KV:OFF_MERGE], ((0, 0), (0, LANES - MLA_ROPE)))
    segs = [w_mla_kv, jnp.zeros((D, LANES), f32), w_gate,
            w_in[:, :OFF_NSA_KV], w_mla_q, w_in[:, OFF_MERGE:], w_in[:, OFF_NSA_KV:OFF_NSA_GATE]]
    starts = np.concatenate([[0], np.cumsum([s.shape[1] for s in segs])])
    col = lambda i: (int(starts[i]), int(segs[i].shape[1]))
    w_all = jnp.concatenate(segs, axis=1).astype(bf16)

    mla_ckv = _mm(xb, w_all, f32, 1024, MLA_KV_RANK + LANES, name="mla_kv_proj", cols=col(0))
    gates = _mm(xb, w_all, f32, 1024, NSA_GROUPS * LANES, act="sigmoid", name="nsa_gate_proj", cols=col(2))
    nsa_q = _mm(xb, w_all, bf16, 1024, 1024, scale=HEAD_DIM ** -0.5 * LOG2E, name="nsa_q_proj", cols=col(3))
    mla_q = _mla_q(xb, w_all, col(4), ropes, T, 1024, 1024)
    merge = _mm(xb, w_all, f32, 1024, 1024, act="sigmoid", name="merge_gate_proj", cols=col(5))
    nsa_kv = _mm_slabs(xb, w_all, col(6), 1024, 512, "nsa_kv_proj")

    slc_tk = min(K_TILE, T)
    rev_tab, cmp_tab = _position_tables(rel_bias, T)
    bias_tab = _bias_table(rev_tab)
    pe = jnp.stack([cmp_k_pe, cmp_v_pe])
    w1 = jnp.stack([cmp_k_w1, cmp_v_w1]).reshape(2, CMP_BLOCK, HEAD_DIM, CMP_HIDDEN).astype(bf16)
    w2 = jnp.stack([cmp_k_w2, cmp_v_w2]).astype(bf16)
    kvc, kvc_t = _compress(nsa_kv, pe, w1, w2, B, T)
    nc = T // CMP_STRIDE
    n_slc = T // SLC_BLOCK
    cs = np.arange(nc)[None, :] * CMP_STRIDE
    ss = np.arange(LANES)[:, None] * SLC_BLOCK
    ovt = ((cs < ss + SLC_BLOCK) & (cs + CMP_BLOCK > ss) & (np.arange(nc)[None, :] < nc - 1)
           & (np.arange(LANES)[:, None] < n_slc)).astype(np.float32)
    o_cmp, sel = _nsa_cmp(nsa_q, kvc, kvc_t, cmp_tab, jnp.asarray(ovt), B, T)
    emat = (np.arange(T)[None, :] // SLC_BLOCK == np.arange(LANES)[:, None]).astype(np.float32)
    o_slc = _nsa_band(nsa_q, nsa_kv, bias_tab, B, T, 2, 3, slc_tk, sel=sel, emat=jnp.asarray(emat, dtype=bf16))
    o_nsa = _nsa_band(nsa_q, nsa_kv, bias_tab, B, T, 4, 5, WIN_TILE, merge=(o_cmp, o_slc, gates), tq=WIN_TILE)

    mla_k, mla_v = _mla_kv(mla_ckv, row(mla_kv_norm),
                           jnp.asarray(mla_w_uk, bf16), jnp.asarray(mla_w_uv, bf16), ropes, T, 512)
    o_mla = _mla_attn(mla_q, mla_k, mla_v, B, T)

    z = _branch_merge(o_nsa, o_mla, w_branch_nsa.astype(bf16), w_branch_mla.astype(bf16), merge, 1024, 512)
    x1 = _proj_ln(z, w_mix_out.astype(bf16), x2d, row(ln_mix_g), row(ln_mix_b), 512)

    mem2d = mem.reshape(B * N_MEM, D).astype(bf16)
    km = _mm(mem2d, mem_w_k.astype(bf16), bf16, B * N_MEM, 256, name="mem_k_proj").reshape(B, N_MEM, -1)
    vm = _mm(mem2d, mem_w_v.astype(bf16), bf16, B * N_MEM, 256, name="mem_v_proj").reshape(B, N_MEM, -1)
    x2, x2b = _mem_attn(x1, mem_w_q.astype(bf16), km, vm, mem_w_o.astype(bf16), row(ln_mem_g), row(ln_mem_b), T, 512)

    pq = _mm(x2b, peer_w_query.astype(bf16), bf16, 1024, 512, name="peer_query_proj")
    e1, e2, gt = _peer_route(pq, peer_sub_keys)
    wts = _peer_weights(e1, e2, gt)
    y = _peer_dense(x2b, wts, peer_u, peer_v)
    x3 = _add_ln(x2, y, row(ln_ffn_g), row(ln_ffn_b), 512)
    return x3.reshape(B, T, D)


def kernel(x, mem, rel_bias, w_in, cmp_k_pe, cmp_k_w1, cmp_k_w2, cmp_v_pe, cmp_v_w1, cmp_v_w2, mla_kv_norm, mla_w_uk, mla_w_uv, w_branch_nsa, w_branch_mla, w_mix_out, ln_mix_g, ln_mix_b, mem_w_q, mem_w_k, mem_w_v, mem_w_o, ln_mem_g, ln_mem_b, peer_w_query, peer_sub_keys, peer_u, peer_v, ln_ffn_g, ln_ffn_b):
    ropes = _rope_slot_tables(x.shape[1])
    for l in range(DEPTH):
        x = _layer(x, mem, rel_bias, w_in[l], cmp_k_pe[l], cmp_k_w1[l], cmp_k_w2[l], cmp_v_pe[l], cmp_v_w1[l],
                   cmp_v_w2[l], mla_kv_norm[l], mla_w_uk[l], mla_w_uv[l], w_branch_nsa[l], w_branch_mla[l],
                   w_mix_out[l], ln_mix_g[l], ln_mix_b[l], mem_w_q[l], mem_w_k[l], mem_w_v[l], mem_w_o[l],
                   ln_mem_g[l], ln_mem_b[l], peer_w_query[l], peer_sub_keys[l], peer_u[l], peer_v[l],
                   ln_ffn_g[l], ln_ffn_b[l], ropes)
    return x
```
